```python
import jax
import jax.numpy as jnp
from jax import lax
import numpy as np

D_MODEL = 2048
BATCH = 32
SEQ = 256
DEPTH = 2
DEC_BATCH = 4
DEC_SEQ = 1024
PAST_LEN = 256

GRID_W = 64
HEAD_DIM = 64
H_RET = 12
H_NA = 10
H_WIN = 10
KV_WIN = 2
G_WIN = H_WIN // KV_WIN
W_RET = H_RET * HEAD_DIM
W_NA = H_NA * HEAD_DIM
W_WIN = H_WIN * HEAD_DIM
W_MIX = W_RET + W_NA + W_WIN
W_IN = 4 * W_RET + 3 * W_NA + W_WIN + 2 * KV_WIN * HEAD_DIM
RET_CHUNK = 128
NA_ROWS = 8
NA_COLS = 16
WIN_HALF = 128
WIN_BLOCK = 128
ROPE_BASE = 10000.0
D_FF = 5632
N_EXPERTS = 8
TOP_K = 2
D_FF_EXPERT = 7168
EPS = 1e-6
NEG = -1e30
SCALE = HEAD_DIM ** -0.5

kernel_name = 'hybrid_diffusion_prefix_trunk_step'


def _rmsnorm(x, g):
    xf = x.astype(jnp.float32)
    y = xf * lax.rsqrt(jnp.mean(xf * xf, axis=-1, keepdims=True) + EPS)
    return (y * g.astype(jnp.float32)).astype(x.dtype)


def _modulate(h, shift, scale):
    return h * (1 + scale) + shift


def _adaln(cond, w, b):
    return jnp.split(jax.nn.silu(cond) @ w + b, 6, axis=-1)


def _heads(t, n):
    b, l, _ = t.shape
    return t.reshape(b, l, n, HEAD_DIM).transpose(0, 2, 1, 3)


def _merge(t):
    b, h, l, d = t.shape
    return t.transpose(0, 2, 1, 3).reshape(b, l, h * d)


def _project(h, w_in):
    sizes = [W_RET] * 4 + [W_NA] * 3 + [W_WIN, KV_WIN * HEAD_DIM, KV_WIN * HEAD_DIM]
    offs = np.cumsum(sizes)[:-1].tolist()
    rq, rk, rv, rg, nq, nk, nv, wq, wk, wv = jnp.split(h @ w_in, offs, axis=-1)
    return (_heads(rq, H_RET), _heads(rk, H_RET), _heads(rv, H_RET), rg,
            _heads(nq, H_NA), _heads(nk, H_NA), _heads(nv, H_NA),
            _heads(wq, H_WIN), _heads(wk, KV_WIN), _heads(wv, KV_WIN))


def _rope2d(x):
    b, h, L, d = x.shape
    t = jnp.arange(L)
    quarter = HEAD_DIM // 4
    half = HEAD_DIM // 2
    inv = ROPE_BASE ** (-jnp.arange(quarter, dtype=jnp.float32) / quarter)

    def rot(xa, pos):
        ang = pos.astype(jnp.float32)[:, None] * inv[None, :]
        cos, sin = jnp.cos(ang), jnp.sin(ang)
        x1, x2 = xa[..., :quarter], xa[..., quarter:]
        return jnp.concatenate([x1 * cos - x2 * sin, x2 * cos + x1 * sin], axis=-1)

    xf = x.astype(jnp.float32)
    out = jnp.concatenate([rot(xf[..., :half], t // GRID_W), rot(xf[..., half:], t % GRID_W)], axis=-1)
    return out.astype(x.dtype)


def _retention_scan(q, k, v, log_gamma, s0):
    b, h, L, d = q.shape
    n = L // RET_CHUNK
    qc = q.reshape(b, h, n, RET_CHUNK, d)
    kc = k.reshape(b, h, n, RET_CHUNK, d)
    vc = v.reshape(b, h, n, RET_CHUNK, d)
    pos = jnp.arange(RET_CHUNK, dtype=jnp.float32)
    diff = pos[:, None] - pos[None, :]
    decay = jnp.where(diff >= 0, jnp.exp(log_gamma[:, None, None] * jnp.maximum(diff, 0.0)), 0.0)
    att = jnp.einsum('bhnid,bhnjd->bhnij', qc, kc) * decay[:, None]
    o_intra = jnp.einsum('bhnij,bhnjd->bhnid', att, vc)
    k_w = jnp.exp(log_gamma[:, None] * (RET_CHUNK - 1 - pos)[None, :])
    kv = jnp.einsum('bhncd,hc,bhnce->nbhde', kc, k_w, vc)
    g_chunk = jnp.exp(log_gamma * RET_CHUNK)[None, :, None, None]

    def step(s, kv_n):
        return g_chunk * s + kv_n, s

    s_fin, s_before = lax.scan(step, s0, kv)
    q_w = jnp.exp(log_gamma[:, None] * (pos + 1.0)[None, :])
    o_cross = jnp.einsum('bhncd,hc,nbhde->bhnce', qc, q_w, s_before)
    return (o_intra + o_cross).reshape(b, h, L, d), s_fin


def _retention_bidir(q, k, v, raw_f, raw_b, s0_f, s0_b):
    q = q.astype(jnp.float32)
    k = k.astype(jnp.float32) * SCALE
    v = v.astype(jnp.float32)
    lg_f = jax.nn.log_sigmoid(raw_f.astype(jnp.float32))
    lg_b = jax.nn.log_sigmoid(raw_b.astype(jnp.float32))
    o_f, s_f = _retention_scan(q, k, v, lg_f, s0_f.astype(jnp.float32))
    fl = lambda t: jnp.flip(t, axis=2)
    o_b, s_b = _retention_scan(fl(q), fl(k), fl(v), lg_b, s0_b.astype(jnp.float32))
    return o_f + fl(o_b), s_f, s_b


def _retention_out(o, gate):
    mu = jnp.mean(o, axis=-1, keepdims=True)
    var = jnp.mean(jnp.square(o - mu), axis=-1, keepdims=True)
    o = (o - mu) * lax.rsqrt(var + EPS)
    return _merge(o).astype(gate.dtype) * jax.nn.silu(gate)


def _full_attn(q, k, v):
    s = jnp.einsum('bhqd,bhkd->bhqk', q, k).astype(jnp.float32) * SCALE
    p = jax.nn.softmax(s, axis=-1).astype(v.dtype)
    return jnp.einsum('bhqk,bhkd->bhqd', p, v)


def _gqa_sink_ctx(q, k, v, sink):
    b, h, L, d = q.shape
    qg = q.reshape(b, KV_WIN, G_WIN, L, d)
    s = jnp.einsum('bkgqd,bkcd->bkgqc', qg, k).astype(jnp.float32) * SCALE
    sk = jnp.broadcast_to(sink.astype(jnp.float32).reshape(1, KV_WIN, G_WIN, 1, 1), s.shape[:-1] + (1,))
    p = jax.nn.softmax(jnp.concatenate([s, sk], axis=-1), axis=-1)[..., :-1].astype(v.dtype)
    return jnp.einsum('bkgqc,bkcd->bkgqd', p, v).reshape(b, h, L, d)


def _na_latent(q, k, v, kc, vc, rpb):
    b, h, L, d = q.shape
    rows = L // GRID_W
    wr = min(NA_ROWS, rows)
    r = jnp.arange(rows)
    r0 = jnp.clip(r - wr // 2, 0, rows - wr)
    ridx = r0[:, None] + jnp.arange(wr)[None, :]
    cols = jnp.arange(GRID_W)
    c0 = jnp.clip(cols - NA_COLS // 2, 0, GRID_W - NA_COLS)
    col_ok = (cols[None, :] >= c0[:, None]) & (cols[None, :] < c0[:, None] + NA_COLS)
    qg = q.reshape(b, h, rows, GRID_W, d)
    kg = k.reshape(b, h, rows, GRID_W, d)[:, :, ridx]
    vg = v.reshape(b, h, rows, GRID_W, d)[:, :, ridx]
    s = jnp.einsum('bhrqd,bhrwkd->bhrqwk', qg, kg).astype(jnp.float32) * SCALE
    dr = ridx - r[:, None] + (NA_ROWS - 1)
    dc = jnp.clip(cols[None, :] - cols[:, None], -(NA_COLS - 1), NA_COLS - 1) + (NA_COLS - 1)
    bias = rpb.astype(jnp.float32)[:, dr[:, None, :, None], dc[None, :, None, :]]
    s = jnp.where(col_ok[None, None, None, :, None, :], s + bias[None], NEG)
    n_nb = wr * GRID_W
    s = s.reshape(b, h, rows, GRID_W, n_nb)
    sc = jnp.einsum('bhrqd,bhcd->bhrqc', qg, kc).astype(jnp.float32) * SCALE
    p = jax.nn.softmax(jnp.concatenate([s, sc], axis=-1), axis=-1).astype(v.dtype)
    o = (jnp.einsum('bhrqn,bhrnd->bhrqd', p[..., :n_nb], vg.reshape(b, h, rows, n_nb, d))
         + jnp.einsum('bhrqc,bhcd->bhrqd', p[..., n_nb:], vc))
    return o.reshape(b, h, L, d)


def _win_latent(q, k, v, kc, vc, sink):
    b, h, L, d = q.shape
    nb = L // WIN_BLOCK
    qb = q.reshape(b, KV_WIN, G_WIN, nb, WIN_BLOCK, d)

    def band(t):
        tp = jnp.pad(t, ((0, 0), (0, 0), (WIN_BLOCK, WIN_BLOCK), (0, 0))).reshape(b, KV_WIN, nb + 2, WIN_BLOCK, d)
        return jnp.concatenate([tp[:, :, 0:nb], tp[:, :, 1:nb + 1], tp[:, :, 2:nb + 2]], axis=3)

    kb, vb = band(k), band(v)
    qpos = jnp.arange(nb)[:, None] * WIN_BLOCK + jnp.arange(WIN_BLOCK)[None, :]
    kpos = (jnp.arange(nb)[:, None] - 1) * WIN_BLOCK + jnp.arange(3 * WIN_BLOCK)[None, :]
    ok = ((jnp.abs(qpos[:, :, None] - kpos[:, None, :]) <= WIN_HALF)
          & (kpos[:, None, :] >= 0) & (kpos[:, None, :] < L))
    s = jnp.einsum('bkgnqd,bkncd->bkgnqc', qb, kb).astype(jnp.float32) * SCALE
    s = jnp.where(ok, s, NEG)
    sc = jnp.einsum('bkgnqd,bkcd->bkgnqc', qb, kc).astype(jnp.float32) * SCALE
    sk = jnp.broadcast_to(sink.astype(jnp.float32).reshape(1, KV_WIN, G_WIN, 1, 1, 1), s.shape[:-1] + (1,))
    p = jax.nn.softmax(jnp.concatenate([s, sc, sk], axis=-1), axis=-1).astype(v.dtype)
    n_band = 3 * WIN_BLOCK
    lc = kc.shape[2]
    o = (jnp.einsum('bkgnqc,bkncd->bkgnqd', p[..., :n_band], vb)
         + jnp.einsum('bkgnqc,bkcd->bkgnqd', p[..., n_band:n_band + lc], vc))
    return o.reshape(b, h, L, d)


def _mix_context(h, w_in, w_out, dec_f, dec_b, sink):
    rq, rk, rv, rg, nq, nk, nv, wq, wk, wv = _project(h, w_in)
    zero = jnp.zeros((h.shape[0], H_RET, HEAD_DIM, HEAD_DIM), jnp.float32)
    o_r, s_f, s_b = _retention_bidir(rq, rk, rv, dec_f, dec_b, zero, zero)
    o_n = _full_attn(nq, nk, nv)
    o_w = _gqa_sink_ctx(wq, wk, wv, sink)
    o = jnp.concatenate([_retention_out(o_r, rg), _merge(o_n), _merge(o_w)], axis=-1) @ w_out
    return o, (s_f, s_b, nk, nv, wk, wv)


def _mix_latent(h, s_f0, s_b0, kc_n, vc_n, kc_w, vc_w, w_in, w_out, dec_f, dec_b, rpb, sink):
    rq, rk, rv, rg, nq, nk, nv, wq, wk, wv = _project(h, w_in)
    o_r, _, _ = _retention_bidir(rq, rk, rv, dec_f, dec_b, s_f0, s_b0)
    o_n = _na_latent(nq, nk, nv, kc_n, vc_n, rpb)
    o_w = _win_latent(_rope2d(wq), _rope2d(wk), wv, kc_w, vc_w, sink)
    return jnp.concatenate([_retention_out(o_r, rg), _merge(o_n), _merge(o_w)], axis=-1) @ w_out


def _swiglu(h, w_in, w_out):
    a, g = jnp.split(h @ w_in, 2, axis=-1)
    return (jax.nn.silu(a) * g) @ w_out


def _moe(h, router, w_in, w_out):
    b, L, D = h.shape
    t = h.reshape(b * L, D)
    logits = (t @ router).astype(jnp.float32)
    vals, idx = lax.top_k(logits, TOP_K)
    wts = jax.nn.softmax(vals, axis=-1)
    gate = jnp.sum(jax.nn.one_hot(idx, N_EXPERTS, dtype=jnp.float32) * wts[..., None], axis=1)
    out = jnp.zeros((b * L, D), jnp.float32)
    for e in range(N_EXPERTS):
        out = out + gate[:, e:e + 1] * _swiglu(t, w_in[e], w_out[e]).astype(jnp.float32)
    return out.astype(h.dtype).reshape(b, L, D)


def _channel(h, l, ffn_w_in, ffn_w_out, moe_router, moe_w_in, moe_w_out):
    i = l // 2
    if l % 2 == 0:
        return _swiglu(h, ffn_w_in[i], ffn_w_out[i])
    return _moe(h, moe_router[i], moe_w_in[i], moe_w_out[i])


def setup_inputs(seed: int = 0) -> dict:
    key = jax.random.key(seed)
    ks = jax.random.split(key, 32)
    f32 = jnp.float32
    nd = (DEPTH + 1) // 2
    nm = DEPTH // 2
    nrm = lambda k, shape, s: jax.random.normal(k, shape, f32) * s
    decay_base = jnp.log(2.0 ** (5.0 + jnp.arange(H_RET, dtype=f32)) - 1.0)
    return {
        'x_prompt': nrm(ks[0], (BATCH, SEQ, D_MODEL), 1.0),
        'x_sample': nrm(ks[1], (DEC_BATCH, DEC_SEQ, D_MODEL), 1.0),
        'c': nrm(ks[2], (DEC_BATCH, D_MODEL), 1.0),
        'state_ret_fwd': nrm(ks[3], (DEC_BATCH, DEPTH, H_RET, HEAD_DIM, HEAD_DIM), 0.5),
        'state_ret_bwd': nrm(ks[4], (DEC_BATCH, DEPTH, H_RET, HEAD_DIM, HEAD_DIM), 0.5),
        'cache_na_k': nrm(ks[5], (DEC_BATCH, DEPTH, H_NA, PAST_LEN, HEAD_DIM), 1.0),
        'cache_na_v': nrm(ks[6], (DEC_BATCH, DEPTH, H_NA, PAST_LEN, HEAD_DIM), 1.0),
        'cache_win_k': nrm(ks[7], (DEC_BATCH, DEPTH, KV_WIN, PAST_LEN, HEAD_DIM), 1.0),
        'cache_win_v': nrm(ks[8], (DEC_BATCH, DEPTH, KV_WIN, PAST_LEN, HEAD_DIM), 1.0),
        'c_ctx': nrm(ks[9], (D_MODEL,), 1.0),
        'norm1_g': 1.0 + nrm(ks[10], (DEPTH, D_MODEL), 0.02),
        'norm2_g': 1.0 + nrm(ks[11], (DEPTH, D_MODEL), 0.02),
        'ada_w': nrm(ks[12], (DEPTH, D_MODEL, 6 * D_MODEL), D_MODEL ** -0.5),
        'ada_b': nrm(ks[13], (DEPTH, 6 * D_MODEL), 0.02),
        'w_in': nrm(ks[14], (DEPTH, D_MODEL, W_IN), D_MODEL ** -0.5),
        'w_out': nrm(ks[15], (DEPTH, W_MIX, D_MODEL), W_MIX ** -0.5),
        'ret_decay_fwd': decay_base[None, :] + nrm(ks[16], (DEPTH, H_RET), 0.1),
        'ret_decay_bwd': decay_base[None, :] + nrm(ks[17], (DEPTH, H_RET), 0.1),
        'na_rpb': nrm(ks[18], (DEPTH, H_NA, 2 * NA_ROWS - 1, 2 * NA_COLS - 1), 0.1),
        'win_sink': nrm(ks[19], (DEPTH, H_WIN), 0.5),
        'ffn_w_in': nrm(ks[20], (nd, D_MODEL, 2 * D_FF), D_MODEL ** -0.5),
        'ffn_w_out': nrm(ks[21], (nd, D_FF, D_MODEL), D_FF ** -0.5),
        'moe_router': nrm(ks[22], (nm, D_MODEL, N_EXPERTS), D_MODEL ** -0.5),
        'moe_w_in': nrm(ks[23], (nm, N_EXPERTS, D_MODEL, 2 * D_FF_EXPERT), D_MODEL ** -0.5),
        'moe_w_out': nrm(ks[24], (nm, N_EXPERTS, D_FF_EXPERT, D_MODEL), D_FF_EXPERT ** -0.5),
        'final_norm_g': 1.0 + nrm(ks[25], (D_MODEL,), 0.02),
    }


def reference(x_prompt, x_sample, c, state_ret_fwd, state_ret_bwd, cache_na_k, cache_na_v,
              cache_win_k, cache_win_v, c_ctx, norm1_g, norm2_g, ada_w, ada_b, w_in, w_out,
              ret_decay_fwd, ret_decay_bwd, na_rpb, win_sink, ffn_w_in, ffn_w_out,
              moe_router, moe_w_in, moe_w_out, final_norm_g):
    xp = x_prompt
    xs = x_sample
    ctx_states = []
    for l in range(DEPTH):
        m = _adaln(c_ctx, ada_w[l], ada_b[l])
        h = _modulate(_rmsnorm(xp, norm1_g[l]), m[0], m[1])
        o, ctx = _mix_context(h, w_in[l], w_out[l], ret_decay_fwd[l], ret_decay_bwd[l], win_sink[l])
        xp = xp + m[2] * o
        h = _modulate(_rmsnorm(xp, norm2_g[l]), m[3], m[4])
        xp = xp + m[5] * _channel(h, l, ffn_w_in, ffn_w_out, moe_router, moe_w_in, moe_w_out)
        ctx_states.append(ctx)
        m = [t[:, None, :] for t in _adaln(c, ada_w[l], ada_b[l])]
        h = _modulate(_rmsnorm(xs, norm1_g[l]), m[0], m[1])
        o = _mix_latent(h, state_ret_fwd[:, l], state_ret_bwd[:, l], cache_na_k[:, l], cache_na_v[:, l],
                        cache_win_k[:, l], cache_win_v[:, l], w_in[l], w_out[l],
                        ret_decay_fwd[l], ret_decay_bwd[l], na_rpb[l], win_sink[l])
        xs = xs + m[2] * o
        h = _modulate(_rmsnorm(xs, norm2_g[l]), m[3], m[4])
        xs = xs + m[5] * _channel(h, l, ffn_w_in, ffn_w_out, moe_router, moe_w_in, moe_w_out)
    y_prompt = _rmsnorm(xp, final_norm_g)
    y_sample = _rmsnorm(xs, final_norm_g)
    dt = x_prompt.dtype
    new_ret_fwd = jnp.stack([s[0] for s in ctx_states], axis=1).astype(dt)
    new_ret_bwd = jnp.stack([s[1] for s in ctx_states], axis=1).astype(dt)
    new_na_k = jnp.stack([s[2] for s in ctx_states], axis=1).astype(dt)
    new_na_v = jnp.stack([s[3] for s in ctx_states], axis=1).astype(dt)
    new_win_k = jnp.stack([s[4] for s in ctx_states], axis=1).astype(dt)
    new_win_v = jnp.stack([s[5] for s in ctx_states], axis=1).astype(dt)
    return (y_prompt, y_sample, new_ret_fwd, new_ret_bwd, new_na_k, new_na_v, new_win_k, new_win_v)
```

```python
import functools

import jax
import jax.numpy as jnp
from jax import lax
import numpy as np
from jax.experimental import pallas as pl
from jax.experimental.pallas import tpu as pltpu

D_MODEL = 2048
BATCH = 32
SEQ = 256
DEPTH = 2
DEC_BATCH = 4
DEC_SEQ = 1024
PAST_LEN = 256

GRID_W = 64
HEAD_DIM = 64
H_RET = 12
H_NA = 10
H_WIN = 10
KV_WIN = 2
G_WIN = H_WIN // KV_WIN
W_RET = H_RET * HEAD_DIM
W_NA = H_NA * HEAD_DIM
W_WIN = H_WIN * HEAD_DIM
W_MIX = W_RET + W_NA + W_WIN
W_IN = 4 * W_RET + 3 * W_NA + W_WIN + 2 * KV_WIN * HEAD_DIM
RET_CHUNK = 128
NA_ROWS = 8
NA_COLS = 16
WIN_HALF = 128
WIN_BLOCK = 128
ROPE_BASE = 10000.0
D_FF = 5632
N_EXPERTS = 8
TOP_K = 2
D_FF_EXPERT = 7168
EPS = 1e-6
NEG = -1e30
SCALE = HEAD_DIM ** -0.5

T_CTX = BATCH * SEQ
T_LAT = DEC_BATCH * DEC_SEQ
T_ALL = T_CTX + T_LAT
N_GROUPS = 8
LANES = 128

F32 = jnp.float32
BF16 = jnp.bfloat16

VMEM_LIMIT = 56 * 1024 * 1024


def _group_of_tile(i, tm):
    return jnp.maximum((i * tm - T_CTX) // DEC_SEQ + 1, 0)


def _cparams(sem):
    return pltpu.CompilerParams(dimension_semantics=sem, vmem_limit_bytes=VMEM_LIMIT)


def _adaln_kernel(c_ref, w_ref, b_ref, o_ref):
    c = c_ref[...]
    s = (c * jax.nn.sigmoid(c)).astype(BF16)
    o_ref[...] = jnp.dot(s, w_ref[...].astype(BF16), preferred_element_type=F32) + b_ref[...]


def _adaln(cond, ada_w, ada_b, l):
    tn = 1024
    n = 6 * D_MODEL
    out = pl.pallas_call(
        _adaln_kernel,
        out_shape=jax.ShapeDtypeStruct((N_GROUPS, n), F32),
        grid=(n // tn,),
        in_specs=[
            pl.BlockSpec((N_GROUPS, D_MODEL), lambda j: (0, 0)),
            pl.BlockSpec((None, D_MODEL, tn), lambda j: (l, 0, j)),
            pl.BlockSpec((None, 1, tn), lambda j: (l, 0, j)),
        ],
        out_specs=pl.BlockSpec((N_GROUPS, tn), lambda j: (0, j)),
        compiler_params=_cparams(("arbitrary",)),
        name="adaln",
    )(cond, ada_w, ada_b.reshape(DEPTH, 1, n))
    return out.reshape(N_GROUPS, 6, 1, D_MODEL)


def _norm_mod_body(x_ref, g_ref, sh_ref, sc_ref):
    x = x_ref[...]
    ms = jnp.mean(x * x, axis=-1, keepdims=True)
    y = x * lax.rsqrt(ms + EPS) * g_ref[...]
    return y * (1.0 + sc_ref[...]) + sh_ref[...]


def _norm_mod_kernel(x_ref, g_ref, sh_ref, sc_ref, h_ref):
    h_ref[...] = _norm_mod_body(x_ref, g_ref, sh_ref, sc_ref).astype(h_ref.dtype)


def _norm_mod_router_kernel(x_ref, g_ref, sh_ref, sc_ref, r_ref, h_ref, lg_ref):
    h = _norm_mod_body(x_ref, g_ref, sh_ref, sc_ref)
    h_ref[...] = h.astype(h_ref.dtype)
    lg_ref[...] = jnp.dot(h, r_ref[...], preferred_element_type=F32,
                          precision=lax.Precision.HIGHEST)


def _norm_mod(x, g, mod, l, k_shift, k_scale, router=None):
    tm = 512
    g3 = g.reshape(DEPTH, 1, D_MODEL)
    in_specs = [
        pl.BlockSpec((tm, D_MODEL), lambda i: (i, 0)),
        pl.BlockSpec((None, 1, D_MODEL), lambda i: (l, 0, 0)),
        pl.BlockSpec((None, None, 1, D_MODEL), lambda i: (_group_of_tile(i, tm), k_shift, 0, 0)),
        pl.BlockSpec((None, None, 1, D_MODEL), lambda i: (_group_of_tile(i, tm), k_scale, 0, 0)),
    ]
    h_shape = jax.ShapeDtypeStruct((T_ALL, D_MODEL), BF16)
    h_spec = pl.BlockSpec((tm, D_MODEL), lambda i: (i, 0))
    if router is None:
        return pl.pallas_call(
            _norm_mod_kernel, out_shape=h_shape, grid=(T_ALL // tm,),
            in_specs=in_specs, out_specs=h_spec,
            compiler_params=_cparams(("parallel",)), name="norm_mod",
        )(x, g3, mod, mod)
    in_specs.append(pl.BlockSpec((D_MODEL, LANES), lambda i: (0, 0)))
    return pl.pallas_call(
        _norm_mod_router_kernel,
        out_shape=(h_shape, jax.ShapeDtypeStruct((T_ALL, LANES), F32)),
        grid=(T_ALL // tm,),
        in_specs=in_specs,
        out_specs=(h_spec, pl.BlockSpec((tm, LANES), lambda i: (i, 0))),
        compiler_params=_cparams(("parallel",)), name="norm_mod_router",
    )(x, g3, mod, mod, router)


def _mm_kernel(a_ref, w_ref, o_ref):
    o_ref[...] = jnp.dot(a_ref[...].astype(BF16), w_ref[...].astype(BF16),
                         preferred_element_type=F32)


def _mm_res_kernel(a_ref, w_ref, x_ref, gate_ref, o_ref):
    acc = jnp.dot(a_ref[...].astype(BF16), w_ref[...].astype(BF16),
                  preferred_element_type=F32)
    o_ref[...] = x_ref[...] + gate_ref[...] * acc


def _matmul(a, w, l, *, resid=None, mod=None, k_gate=None):
    tm, tn = 1024, 512
    t, k = a.shape
    n = w.shape[-1]
    grid = (t // tm, pl.cdiv(n, tn))
    in_specs = [
        pl.BlockSpec((tm, k), lambda i, j: (i, 0)),
        pl.BlockSpec((None, k, tn), lambda i, j: (l, 0, j)),
    ]
    args = [a, w]
    kern = _mm_kernel
    if resid is not None:
        in_specs += [
            pl.BlockSpec((tm, tn), lambda i, j: (i, j)),
            pl.BlockSpec((None, None, 1, tn), lambda i, j: (_group_of_tile(i, tm), k_gate, 0, j)),
        ]
        args += [resid, mod]
        kern = _mm_res_kernel
    return pl.pallas_call(
        kern, out_shape=jax.ShapeDtypeStruct((t, n), F32), grid=grid,
        in_specs=in_specs, out_specs=pl.BlockSpec((tm, tn), lambda i, j: (i, j)),
        compiler_params=_cparams(("parallel", "arbitrary")), name="proj",
    )(*args)


def _ffn_kernel(te_ref, nt_ref, h_ref, wa_ref, wg_ref, wo_ref, rw_ref, o_ref, *, n_f):
    i = pl.program_id(0)
    j = pl.program_id(1)
    valid = i < nt_ref[0]

    @pl.when(valid)
    def _():
        h = h_ref[...]
        a = jnp.dot(h, wa_ref[...].astype(BF16), preferred_element_type=F32)
        g = jnp.dot(h, wg_ref[...].astype(BF16), preferred_element_type=F32)
        act = (a * jax.nn.sigmoid(a) * g).astype(BF16)
        part = jnp.dot(act, wo_ref[...].astype(BF16), preferred_element_type=F32)

        @pl.when(j == 0)
        def _():
            o_ref[...] = part

        @pl.when(j > 0)
        def _():
            o_ref[...] += part

        @pl.when(j == n_f - 1)
        def _():
            o_ref[...] = o_ref[...] * rw_ref[...]

    @pl.when(jnp.logical_and(jnp.logical_not(valid), j == 0))
    def _():
        o_ref[...] = jnp.zeros_like(o_ref)


def _ffn(hs, te, nt, row_w, w_in, w_out, d_ff, tm):
    tf = 256
    r = hs.shape[0]
    n_f = d_ff // tf

    def jj(i, j, nt_ref):
        return jnp.where(i < nt_ref[0], j, n_f - 1)

    grid_spec = pltpu.PrefetchScalarGridSpec(
        num_scalar_prefetch=2,
        grid=(r // tm, n_f),
        in_specs=[
            pl.BlockSpec((tm, D_MODEL), lambda i, j, te_ref, nt_ref: (i, 0)),
            pl.BlockSpec((None, D_MODEL, tf), lambda i, j, te_ref, nt_ref: (te_ref[i], 0, jj(i, j, nt_ref))),
            pl.BlockSpec((None, D_MODEL, tf), lambda i, j, te_ref, nt_ref: (te_ref[i], 0, jj(i, j, nt_ref) + n_f)),
            pl.BlockSpec((None, tf, D_MODEL), lambda i, j, te_ref, nt_ref: (te_ref[i], jj(i, j, nt_ref), 0)),
            pl.BlockSpec((tm, 1), lambda i, j, te_ref, nt_ref: (i, 0)),
        ],
        out_specs=pl.BlockSpec((tm, D_MODEL), lambda i, j, te_ref, nt_ref: (i, 0)),
    )
    return pl.pallas_call(
        functools.partial(_ffn_kernel, n_f=n_f),
        out_shape=jax.ShapeDtypeStruct((r, D_MODEL), F32),
        grid_spec=grid_spec,
        compiler_params=_cparams(("arbitrary", "arbitrary")), name="ffn",
    )(te, nt, hs, w_in, w_in, w_out, row_w)


def _resid_kernel(x_ref, y_ref, gate_ref, o_ref):
    o_ref[...] = x_ref[...] + gate_ref[...] * y_ref[...]


def _resid2_kernel(x_ref, y_ref, gate_ref, o_ref):
    o_ref[...] = x_ref[...] + gate_ref[...] * (y_ref[:, :D_MODEL] + y_ref[:, D_MODEL:])


def _resid(x, y, mod, k_gate):
    tm = 512
    two = y.shape[-1] == 2 * D_MODEL
    y_spec = pl.BlockSpec((tm, y.shape[-1]), lambda i: (i, 0))
    return pl.pallas_call(
        _resid2_kernel if two else _resid_kernel,
        out_shape=jax.ShapeDtypeStruct((T_ALL, D_MODEL), F32), grid=(T_ALL // tm,),
        in_specs=[
            pl.BlockSpec((tm, D_MODEL), lambda i: (i, 0)),
            y_spec,
            pl.BlockSpec((None, None, 1, D_MODEL), lambda i: (_group_of_tile(i, tm), k_gate, 0, 0)),
        ],
        out_specs=pl.BlockSpec((tm, D_MODEL), lambda i: (i, 0)),
        compiler_params=_cparams(("parallel",)), name="resid",
    )(x, y, mod)


def _final_norm_kernel(x_ref, g_ref, o_ref):
    x = x_ref[...]
    ms = jnp.mean(x * x, axis=-1, keepdims=True)
    o_ref[...] = x * lax.rsqrt(ms + EPS) * g_ref[...]


def _final_norm(x, g):
    tm = 512
    return pl.pallas_call(
        _final_norm_kernel,
        out_shape=jax.ShapeDtypeStruct((T_ALL, D_MODEL), F32), grid=(T_ALL // tm,),
        in_specs=[pl.BlockSpec((tm, D_MODEL), lambda i: (i, 0)),
                  pl.BlockSpec((1, D_MODEL), lambda i: (0, 0))],
        out_specs=pl.BlockSpec((tm, D_MODEL), lambda i: (i, 0)),
        compiler_params=_cparams(("parallel",)), name="final_norm",
    )(x, g.reshape(1, D_MODEL))


FFN_TM = 1024
MOE_TM = 1024
MOE_TILES = (TOP_K * T_ALL) // MOE_TM + N_EXPERTS


def _dense_ffn(h, w_in, w_out, i):
    n_t = T_ALL // FFN_TM
    te = jnp.full((n_t,), i, jnp.int32)
    nt = jnp.full((1,), n_t, jnp.int32)
    ones = jnp.ones((T_ALL, 1), F32)
    return _ffn(h, te, nt, ones, w_in, w_out, D_FF, FFN_TM)


def _moe(h, logits, w_in, w_out):
    tm = MOE_TM
    vals, idx = lax.top_k(logits[:, :N_EXPERTS], TOP_K)
    wts = jax.nn.softmax(vals, axis=-1)
    eid = idx.reshape(-1).astype(jnp.int32)
    onehot = (eid[:, None] == jnp.arange(N_EXPERTS, dtype=jnp.int32)[None, :]).astype(jnp.int32)
    csum = jnp.cumsum(onehot, axis=0)
    rank = jnp.take_along_axis(csum, eid[:, None], axis=1)[:, 0] - 1
    counts = csum[-1]
    ptiles = (counts + tm - 1) // tm
    tile_end = jnp.cumsum(ptiles)
    tile_start = tile_end - ptiles
    pos = tile_start[eid] * tm + rank
    n_tiles = tile_end[-1]
    tile_ids = jnp.arange(MOE_TILES, dtype=jnp.int32)
    te = jnp.searchsorted(tile_end, tile_ids, side="right").astype(jnp.int32)
    te = jnp.minimum(te, N_EXPERTS - 1)
    te_last = te[jnp.maximum(n_tiles - 1, 0)]
    te = jnp.where(tile_ids < n_tiles, te, te_last)
    rows = MOE_TILES * tm
    tok = jnp.arange(TOP_K * T_ALL, dtype=jnp.int32) // TOP_K
    row_token = jnp.zeros((rows,), jnp.int32).at[pos].set(tok)
    row_w = jnp.zeros((rows,), F32).at[pos].set(wts.reshape(-1))
    hs = jnp.take(h, row_token, axis=0)
    ys = _ffn(hs, te, n_tiles.reshape(1).astype(jnp.int32), row_w.reshape(rows, 1),
              w_in, w_out, D_FF_EXPERT, tm)
    return jnp.take(ys, pos, axis=0).reshape(T_ALL, TOP_K * D_MODEL)


def _heads(t, n):
    b, l, _ = t.shape
    return t.reshape(b, l, n, HEAD_DIM).transpose(0, 2, 1, 3)


def _merge(t):
    b, h, l, d = t.shape
    return t.transpose(0, 2, 1, 3).reshape(b, l, h * d)


def _split_proj(p):
    sizes = [W_RET] * 4 + [W_NA] * 3 + [W_WIN, KV_WIN * HEAD_DIM, KV_WIN * HEAD_DIM]
    offs = np.cumsum(sizes)[:-1].tolist()
    rq, rk, rv, rg, nq, nk, nv, wq, wk, wv = jnp.split(p, offs, axis=-1)
    return (_heads(rq, H_RET), _heads(rk, H_RET), _heads(rv, H_RET), rg,
            _heads(nq, H_NA), _heads(nk, H_NA), _heads(nv, H_NA),
            _heads(wq, H_WIN), _heads(wk, KV_WIN), _heads(wv, KV_WIN))


def _rope2d(x):
    b, h, L, d = x.shape
    t = jnp.arange(L)
    quarter = HEAD_DIM // 4
    half = HEAD_DIM // 2
    inv = ROPE_BASE ** (-jnp.arange(quarter, dtype=jnp.float32) / quarter)

    def rot(xa, pos):
        ang = pos.astype(jnp.float32)[:, None] * inv[None, :]
        cos, sin = jnp.cos(ang), jnp.sin(ang)
        x1, x2 = xa[..., :quarter], xa[..., quarter:]
        return jnp.concatenate([x1 * cos - x2 * sin, x2 * cos + x1 * sin], axis=-1)

    xf = x.astype(jnp.float32)
    out = jnp.concatenate([rot(xf[..., :half], t // GRID_W), rot(xf[..., half:], t % GRID_W)], axis=-1)
    return out.astype(x.dtype)


def _retention_scan(q, k, v, log_gamma, s0):
    b, h, L, d = q.shape
    n = L // RET_CHUNK
    qc = q.reshape(b, h, n, RET_CHUNK, d)
    kc = k.reshape(b, h, n, RET_CHUNK, d)
    vc = v.reshape(b, h, n, RET_CHUNK, d)
    pos = jnp.arange(RET_CHUNK, dtype=jnp.float32)
    diff = pos[:, None] - pos[None, :]
    decay = jnp.where(diff >= 0, jnp.exp(log_gamma[:, None, None] * jnp.maximum(diff, 0.0)), 0.0)
    att = jnp.einsum('bhnid,bhnjd->bhnij', qc, kc) * decay[:, None]
    o_intra = jnp.einsum('bhnij,bhnjd->bhnid', att, vc)
    k_w = jnp.exp(log_gamma[:, None] * (RET_CHUNK - 1 - pos)[None, :])
    kv = jnp.einsum('bhncd,hc,bhnce->nbhde', kc, k_w, vc)
    g_chunk = jnp.exp(log_gamma * RET_CHUNK)[None, :, None, None]

    def step(s, kv_n):
        return g_chunk * s + kv_n, s

    s_fin, s_before = lax.scan(step, s0, kv)
    q_w = jnp.exp(log_gamma[:, None] * (pos + 1.0)[None, :])
    o_cross = jnp.einsum('bhncd,hc,nbhde->bhnce', qc, q_w, s_before)
    return (o_intra + o_cross).reshape(b, h, L, d), s_fin


def _retention_bidir(q, k, v, raw_f, raw_b, s0_f, s0_b):
    q = q.astype(jnp.float32)
    k = k.astype(jnp.float32) * SCALE
    v = v.astype(jnp.float32)
    lg_f = jax.nn.log_sigmoid(raw_f.astype(jnp.float32))
    lg_b = jax.nn.log_sigmoid(raw_b.astype(jnp.float32))
    o_f, s_f = _retention_scan(q, k, v, lg_f, s0_f.astype(jnp.float32))
    fl = lambda t: jnp.flip(t, axis=2)
    o_b, s_b = _retention_scan(fl(q), fl(k), fl(v), lg_b, s0_b.astype(jnp.float32))
    return o_f + fl(o_b), s_f, s_b


def _retention_out(o, gate):
    mu = jnp.mean(o, axis=-1, keepdims=True)
    var = jnp.mean(jnp.square(o - mu), axis=-1, keepdims=True)
    o = (o - mu) * lax.rsqrt(var + EPS)
    return _merge(o).astype(gate.dtype) * jax.nn.silu(gate)


def _full_attn(q, k, v):
    s = jnp.einsum('bhqd,bhkd->bhqk', q, k).astype(jnp.float32) * SCALE
    p = jax.nn.softmax(s, axis=-1).astype(v.dtype)
    return jnp.einsum('bhqk,bhkd->bhqd', p, v)


def _gqa_sink_ctx(q, k, v, sink):
    b, h, L, d = q.shape
    qg = q.reshape(b, KV_WIN, G_WIN, L, d)
    s = jnp.einsum('bkgqd,bkcd->bkgqc', qg, k).astype(jnp.float32) * SCALE
    sk = jnp.broadcast_to(sink.astype(jnp.float32).reshape(1, KV_WIN, G_WIN, 1, 1), s.shape[:-1] + (1,))
    p = jax.nn.softmax(jnp.concatenate([s, sk], axis=-1), axis=-1)[..., :-1].astype(v.dtype)
    return jnp.einsum('bkgqc,bkcd->bkgqd', p, v).reshape(b, h, L, d)


def _na_latent(q, k, v, kc, vc, rpb):
    b, h, L, d = q.shape
    rows = L // GRID_W
    wr = min(NA_ROWS, rows)
    r = jnp.arange(rows)
    r0 = jnp.clip(r - wr // 2, 0, rows - wr)
    ridx = r0[:, None] + jnp.arange(wr)[None, :]
    cols = jnp.arange(GRID_W)
    c0 = jnp.clip(cols - NA_COLS // 2, 0, GRID_W - NA_COLS)
    col_ok = (cols[None, :] >= c0[:, None]) & (cols[None, :] < c0[:, None] + NA_COLS)
    qg = q.reshape(b, h, rows, GRID_W, d)
    kg = k.reshape(b, h, rows, GRID_W, d)[:, :, ridx]
    vg = v.reshape(b, h, rows, GRID_W, d)[:, :, ridx]
    s = jnp.einsum('bhrqd,bhrwkd->bhrqwk', qg, kg).astype(jnp.float32) * SCALE
    dr = ridx - r[:, None] + (NA_ROWS - 1)
    dc = jnp.clip(cols[None, :] - cols[:, None], -(NA_COLS - 1), NA_COLS - 1) + (NA_COLS - 1)
    bias = rpb.astype(jnp.float32)[:, dr[:, None, :, None], dc[None, :, None, :]]
    s = jnp.where(col_ok[None, None, None, :, None, :], s + bias[None], NEG)
    n_nb = wr * GRID_W
    s = s.reshape(b, h, rows, GRID_W, n_nb)
    sc = jnp.einsum('bhrqd,bhcd->bhrqc', qg, kc).astype(jnp.float32) * SCALE
    p = jax.nn.softmax(jnp.concatenate([s, sc], axis=-1), axis=-1).astype(v.dtype)
    o = (jnp.einsum('bhrqn,bhrnd->bhrqd', p[..., :n_nb], vg.reshape(b, h, rows, n_nb, d))
         + jnp.einsum('bhrqc,bhcd->bhrqd', p[..., n_nb:], vc))
    return o.reshape(b, h, L, d)


def _win_latent(q, k, v, kc, vc, sink):
    b, h, L, d = q.shape
    nb = L // WIN_BLOCK
    qb = q.reshape(b, KV_WIN, G_WIN, nb, WIN_BLOCK, d)

    def band(t):
        tp = jnp.pad(t, ((0, 0), (0, 0), (WIN_BLOCK, WIN_BLOCK), (0, 0))).reshape(b, KV_WIN, nb + 2, WIN_BLOCK, d)
        return jnp.concatenate([tp[:, :, 0:nb], tp[:, :, 1:nb + 1], tp[:, :, 2:nb + 2]], axis=3)

    kb, vb = band(k), band(v)
    qpos = jnp.arange(nb)[:, None] * WIN_BLOCK + jnp.arange(WIN_BLOCK)[None, :]
    kpos = (jnp.arange(nb)[:, None] - 1) * WIN_BLOCK + jnp.arange(3 * WIN_BLOCK)[None, :]
    ok = ((jnp.abs(qpos[:, :, None] - kpos[:, None, :]) <= WIN_HALF)
          & (kpos[:, None, :] >= 0) & (kpos[:, None, :] < L))
    s = jnp.einsum('bkgnqd,bkncd->bkgnqc', qb, kb).astype(jnp.float32) * SCALE
    s = jnp.where(ok, s, NEG)
    sc = jnp.einsum('bkgnqd,bkcd->bkgnqc', qb, kc).astype(jnp.float32) * SCALE
    sk = jnp.broadcast_to(sink.astype(jnp.float32).reshape(1, KV_WIN, G_WIN, 1, 1, 1), s.shape[:-1] + (1,))
    p = jax.nn.softmax(jnp.concatenate([s, sc, sk], axis=-1), axis=-1).astype(v.dtype)
    n_band = 3 * WIN_BLOCK
    lc = kc.shape[2]
    o = (jnp.einsum('bkgnqc,bkncd->bkgnqd', p[..., :n_band], vb)
         + jnp.einsum('bkgnqc,bkcd->bkgnqd', p[..., n_band:n_band + lc], vc))
    return o.reshape(b, h, L, d)


def _mix_context(p, dec_f, dec_b, sink):
    rq, rk, rv, rg, nq, nk, nv, wq, wk, wv = _split_proj(p)
    zero = jnp.zeros((p.shape[0], H_RET, HEAD_DIM, HEAD_DIM), jnp.float32)
    o_r, s_f, s_b = _retention_bidir(rq, rk, rv, dec_f, dec_b, zero, zero)
    o_n = _full_attn(nq, nk, nv)
    o_w = _gqa_sink_ctx(wq, wk, wv, sink)
    o = jnp.concatenate([_retention_out(o_r, rg), _merge(o_n), _merge(o_w)], axis=-1)
    return o, (s_f, s_b, nk, nv, wk, wv)


def _mix_latent(p, s_f0, s_b0, kc_n, vc_n, kc_w, vc_w, dec_f, dec_b, rpb, sink):
    rq, rk, rv, rg, nq, nk, nv, wq, wk, wv = _split_proj(p)
    o_r, _, _ = _retention_bidir(rq, rk, rv, dec_f, dec_b, s_f0, s_b0)
    o_n = _na_latent(nq, nk, nv, kc_n, vc_n, rpb)
    o_w = _win_latent(_rope2d(wq), _rope2d(wk), wv, kc_w, vc_w, sink)
    return jnp.concatenate([_retention_out(o_r, rg), _merge(o_n), _merge(o_w)], axis=-1)


def kernel(x_prompt, x_sample, c, state_ret_fwd, state_ret_bwd, cache_na_k, cache_na_v, cache_win_k, cache_win_v, c_ctx, norm1_g, norm2_g, ada_w, ada_b, w_in, w_out, ret_decay_fwd, ret_decay_bwd, na_rpb, win_sink, ffn_w_in, ffn_w_out, moe_router, moe_w_in, moe_w_out, final_norm_g):
    x = jnp.concatenate([x_prompt.reshape(T_CTX, D_MODEL), x_sample.reshape(T_LAT, D_MODEL)], axis=0)
    cond = jnp.concatenate([c_ctx[None, :], c, jnp.zeros((N_GROUPS - 1 - DEC_BATCH, D_MODEL), F32)], axis=0)
    ctx_states = []
    for l in range(DEPTH):
        mod = _adaln(cond, ada_w, ada_b, l)
        h = _norm_mod(x, norm1_g, mod, l, 0, 1)
        proj = _matmul(h, w_in, l)
        o_c, ctx = _mix_context(proj[:T_CTX].reshape(BATCH, SEQ, W_IN),
                                ret_decay_fwd[l], ret_decay_bwd[l], win_sink[l])
        o_l = _mix_latent(proj[T_CTX:].reshape(DEC_BATCH, DEC_SEQ, W_IN),
                          state_ret_fwd[:, l], state_ret_bwd[:, l], cache_na_k[:, l], cache_na_v[:, l],
                          cache_win_k[:, l], cache_win_v[:, l],
                          ret_decay_fwd[l], ret_decay_bwd[l], na_rpb[l], win_sink[l])
        ctx_states.append(ctx)
        o = jnp.concatenate([o_c.reshape(T_CTX, W_MIX), o_l.reshape(T_LAT, W_MIX)], axis=0)
        x = _matmul(o, w_out, l, resid=x, mod=mod, k_gate=2)
        i = l // 2
        if l % 2 == 0:
            h = _norm_mod(x, norm2_g, mod, l, 3, 4)
            y = _dense_ffn(h, ffn_w_in, ffn_w_out, i)
        else:
            router = jnp.pad(moe_router[i], ((0, 0), (0, LANES - N_EXPERTS)))
            h, logits = _norm_mod(x, norm2_g, mod, l, 3, 4, router=router)
            y = _moe(h, logits, moe_w_in[i], moe_w_out[i])
        x = _resid(x, y, mod, 5)
    y = _final_norm(x, final_norm_g)
    y_prompt = y[:T_CTX].reshape(BATCH, SEQ, D_MODEL)
    y_sample = y[T_CTX:].reshape(DEC_BATCH, DEC_SEQ, D_MODEL)
    dt = x_prompt.dtype
    outs = [jnp.stack([s[k] for s in ctx_states], axis=1).astype(dt) for k in range(6)]
    return (y_prompt, y_sample, *outs)
```

```python
import functools

import jax
import jax.numpy as jnp
from jax import lax
import numpy as np
from jax.experimental import pallas as pl
from jax.experimental.pallas import tpu as pltpu

D_MODEL = 2048
BATCH = 32
SEQ = 256
DEPTH = 2
DEC_BATCH = 4
DEC_SEQ = 1024
PAST_LEN = 256

GRID_W = 64
HEAD_DIM = 64
H_RET = 12
H_NA = 10
H_WIN = 10
KV_WIN = 2
G_WIN = H_WIN // KV_WIN
W_RET = H_RET * HEAD_DIM
W_NA = H_NA * HEAD_DIM
W_WIN = H_WIN * HEAD_DIM
W_MIX = W_RET + W_NA + W_WIN
W_IN = 4 * W_RET + 3 * W_NA + W_WIN + 2 * KV_WIN * HEAD_DIM
RET_CHUNK = 128
NA_ROWS = 8
NA_COLS = 16
WIN_HALF = 128
WIN_BLOCK = 128
ROPE_BASE = 10000.0
D_FF = 5632
N_EXPERTS = 8
TOP_K = 2
D_FF_EXPERT = 7168
EPS = 1e-6
NEG = -1e30
SCALE = HEAD_DIM ** -0.5

T_CTX = BATCH * SEQ
T_LAT = DEC_BATCH * DEC_SEQ
T_ALL = T_CTX + T_LAT
N_GROUPS = 8
LANES = 128

F32 = jnp.float32
BF16 = jnp.bfloat16

VMEM_LIMIT = 56 * 1024 * 1024


def _group_of_tile(i, tm):
    return jnp.maximum((i * tm - T_CTX) // DEC_SEQ + 1, 0)


def _cparams(sem):
    return pltpu.CompilerParams(dimension_semantics=sem, vmem_limit_bytes=VMEM_LIMIT)


def _adaln_kernel(c_ref, w_ref, b_ref, o_ref):
    c = c_ref[...]
    s = (c * jax.nn.sigmoid(c)).astype(BF16)
    o_ref[...] = jnp.dot(s, w_ref[...].astype(BF16), preferred_element_type=F32) + b_ref[...]


def _adaln(cond, ada_w, ada_b, l):
    tn = 1024
    n = 6 * D_MODEL
    out = pl.pallas_call(
        _adaln_kernel,
        out_shape=jax.ShapeDtypeStruct((N_GROUPS, n), F32),
        grid=(n // tn,),
        in_specs=[
            pl.BlockSpec((N_GROUPS, D_MODEL), lambda j: (0, 0)),
            pl.BlockSpec((None, D_MODEL, tn), lambda j: (l, 0, j)),
            pl.BlockSpec((None, 1, tn), lambda j: (l, 0, j)),
        ],
        out_specs=pl.BlockSpec((N_GROUPS, tn), lambda j: (0, j)),
        compiler_params=_cparams(("arbitrary",)),
        name="adaln",
    )(cond, ada_w, ada_b.reshape(DEPTH, 1, n))
    return out.reshape(N_GROUPS, 6, 1, D_MODEL)


def _norm_mod_body(x_ref, g_ref, sh_ref, sc_ref):
    x = x_ref[...]
    ms = jnp.mean(x * x, axis=-1, keepdims=True)
    y = x * lax.rsqrt(ms + EPS) * g_ref[...]
    return y * (1.0 + sc_ref[...]) + sh_ref[...]


def _norm_mod_kernel(x_ref, g_ref, sh_ref, sc_ref, h_ref):
    h_ref[...] = _norm_mod_body(x_ref, g_ref, sh_ref, sc_ref).astype(h_ref.dtype)


def _norm_mod_router_kernel(x_ref, g_ref, sh_ref, sc_ref, r_ref, h_ref, lg_ref):
    h = _norm_mod_body(x_ref, g_ref, sh_ref, sc_ref)
    h_ref[...] = h.astype(h_ref.dtype)
    lg_ref[...] = jnp.dot(h, r_ref[...], preferred_element_type=F32,
                          precision=lax.Precision.HIGHEST)


def _norm_mod(x, g, mod, l, k_shift, k_scale, router=None):
    tm = 512
    g3 = g.reshape(DEPTH, 1, D_MODEL)
    in_specs = [
        pl.BlockSpec((tm, D_MODEL), lambda i: (i, 0)),
        pl.BlockSpec((None, 1, D_MODEL), lambda i: (l, 0, 0)),
        pl.BlockSpec((None, None, 1, D_MODEL), lambda i: (_group_of_tile(i, tm), k_shift, 0, 0)),
        pl.BlockSpec((None, None, 1, D_MODEL), lambda i: (_group_of_tile(i, tm), k_scale, 0, 0)),
    ]
    h_shape = jax.ShapeDtypeStruct((T_ALL, D_MODEL), BF16)
    h_spec = pl.BlockSpec((tm, D_MODEL), lambda i: (i, 0))
    if router is None:
        return pl.pallas_call(
            _norm_mod_kernel, out_shape=h_shape, grid=(T_ALL // tm,),
            in_specs=in_specs, out_specs=h_spec,
            compiler_params=_cparams(("parallel",)), name="norm_mod",
        )(x, g3, mod, mod)
    in_specs.append(pl.BlockSpec((D_MODEL, LANES), lambda i: (0, 0)))
    return pl.pallas_call(
        _norm_mod_router_kernel,
        out_shape=(h_shape, jax.ShapeDtypeStruct((T_ALL, LANES), F32)),
        grid=(T_ALL // tm,),
        in_specs=in_specs,
        out_specs=(h_spec, pl.BlockSpec((tm, LANES), lambda i: (i, 0))),
        compiler_params=_cparams(("parallel",)), name="norm_mod_router",
    )(x, g3, mod, mod, router)


def _mm_kernel(a_ref, w_ref, o_ref):
    o_ref[...] = jnp.dot(a_ref[...].astype(BF16), w_ref[...].astype(BF16),
                         preferred_element_type=F32)


def _mm_res_kernel(a_ref, w_ref, x_ref, gate_ref, o_ref):
    acc = jnp.dot(a_ref[...].astype(BF16), w_ref[...].astype(BF16),
                  preferred_element_type=F32)
    o_ref[...] = x_ref[...] + gate_ref[...] * acc


def _matmul(a, w, l, *, resid=None, mod=None, k_gate=None):
    tm, tn = 1024, 512
    t, k = a.shape
    n = w.shape[-1]
    grid = (t // tm, pl.cdiv(n, tn))
    in_specs = [
        pl.BlockSpec((tm, k), lambda i, j: (i, 0)),
        pl.BlockSpec((None, k, tn), lambda i, j: (l, 0, j)),
    ]
    args = [a, w]
    kern = _mm_kernel
    if resid is not None:
        in_specs += [
            pl.BlockSpec((tm, tn), lambda i, j: (i, j)),
            pl.BlockSpec((None, None, 1, tn), lambda i, j: (_group_of_tile(i, tm), k_gate, 0, j)),
        ]
        args += [resid, mod]
        kern = _mm_res_kernel
    return pl.pallas_call(
        kern, out_shape=jax.ShapeDtypeStruct((t, n), F32), grid=grid,
        in_specs=in_specs, out_specs=pl.BlockSpec((tm, tn), lambda i, j: (i, j)),
        compiler_params=_cparams(("parallel", "arbitrary")), name="proj",
    )(*args)


def _ffn_kernel(te_ref, nt_ref, h_ref, wa_ref, wg_ref, wo_ref, rw_ref, o_ref, *, n_f):
    i = pl.program_id(0)
    j = pl.program_id(1)
    valid = i < nt_ref[0]

    @pl.when(valid)
    def _():
        h = h_ref[...]
        a = jnp.dot(h, wa_ref[...].astype(BF16), preferred_element_type=F32)
        g = jnp.dot(h, wg_ref[...].astype(BF16), preferred_element_type=F32)
        act = (a * jax.nn.sigmoid(a) * g).astype(BF16)
        part = jnp.dot(act, wo_ref[...].astype(BF16), preferred_element_type=F32)

        @pl.when(j == 0)
        def _():
            o_ref[...] = part

        @pl.when(j > 0)
        def _():
            o_ref[...] += part

        @pl.when(j == n_f - 1)
        def _():
            o_ref[...] = o_ref[...] * rw_ref[...]

    @pl.when(jnp.logical_and(jnp.logical_not(valid), j == 0))
    def _():
        o_ref[...] = jnp.zeros_like(o_ref)


def _ffn(hs, te, nt, row_w, w_in, w_out, d_ff, tm):
    tf = 256
    r = hs.shape[0]
    n_f = d_ff // tf

    def jj(i, j, nt_ref):
        return jnp.where(i < nt_ref[0], j, n_f - 1)

    grid_spec = pltpu.PrefetchScalarGridSpec(
        num_scalar_prefetch=2,
        grid=(r // tm, n_f),
        in_specs=[
            pl.BlockSpec((tm, D_MODEL), lambda i, j, te_ref, nt_ref: (i, 0)),
            pl.BlockSpec((None, D_MODEL, tf), lambda i, j, te_ref, nt_ref: (te_ref[i], 0, jj(i, j, nt_ref))),
            pl.BlockSpec((None, D_MODEL, tf), lambda i, j, te_ref, nt_ref: (te_ref[i], 0, jj(i, j, nt_ref) + n_f)),
            pl.BlockSpec((None, tf, D_MODEL), lambda i, j, te_ref, nt_ref: (te_ref[i], jj(i, j, nt_ref), 0)),
            pl.BlockSpec((tm, 1), lambda i, j, te_ref, nt_ref: (i, 0)),
        ],
        out_specs=pl.BlockSpec((tm, D_MODEL), lambda i, j, te_ref, nt_ref: (i, 0)),
    )
    return pl.pallas_call(
        functools.partial(_ffn_kernel, n_f=n_f),
        out_shape=jax.ShapeDtypeStruct((r, D_MODEL), F32),
        grid_spec=grid_spec,
        compiler_params=_cparams(("arbitrary", "arbitrary")), name="ffn",
    )(te, nt, hs, w_in, w_in, w_out, row_w)


def _resid_kernel(x_ref, y_ref, gate_ref, o_ref):
    o_ref[...] = x_ref[...] + gate_ref[...] * y_ref[...]


def _resid2_kernel(x_ref, y_ref, gate_ref, o_ref):
    o_ref[...] = x_ref[...] + gate_ref[...] * (y_ref[:, :D_MODEL] + y_ref[:, D_MODEL:])


def _resid(x, y, mod, k_gate):
    tm = 512
    two = y.shape[-1] == 2 * D_MODEL
    y_spec = pl.BlockSpec((tm, y.shape[-1]), lambda i: (i, 0))
    return pl.pallas_call(
        _resid2_kernel if two else _resid_kernel,
        out_shape=jax.ShapeDtypeStruct((T_ALL, D_MODEL), F32), grid=(T_ALL // tm,),
        in_specs=[
            pl.BlockSpec((tm, D_MODEL), lambda i: (i, 0)),
            y_spec,
            pl.BlockSpec((None, None, 1, D_MODEL), lambda i: (_group_of_tile(i, tm), k_gate, 0, 0)),
        ],
        out_specs=pl.BlockSpec((tm, D_MODEL), lambda i: (i, 0)),
        compiler_params=_cparams(("parallel",)), name="resid",
    )(x, y, mod)


def _final_norm_kernel(x_ref, g_ref, o_ref):
    x = x_ref[...]
    ms = jnp.mean(x * x, axis=-1, keepdims=True)
    o_ref[...] = x * lax.rsqrt(ms + EPS) * g_ref[...]


def _final_norm(x, g):
    tm = 512
    return pl.pallas_call(
        _final_norm_kernel,
        out_shape=jax.ShapeDtypeStruct((T_ALL, D_MODEL), F32), grid=(T_ALL // tm,),
        in_specs=[pl.BlockSpec((tm, D_MODEL), lambda i: (i, 0)),
                  pl.BlockSpec((1, D_MODEL), lambda i: (0, 0))],
        out_specs=pl.BlockSpec((tm, D_MODEL), lambda i: (i, 0)),
        compiler_params=_cparams(("parallel",)), name="final_norm",
    )(x, g.reshape(1, D_MODEL))


FFN_TM = 1024
MOE_TM = 1024
MOE_TILES = (TOP_K * T_ALL) // MOE_TM + N_EXPERTS


def _dense_ffn(h, w_in, w_out, i):
    n_t = T_ALL // FFN_TM
    te = jnp.full((n_t,), i, jnp.int32)
    nt = jnp.full((1,), n_t, jnp.int32)
    ones = jnp.ones((T_ALL, 1), F32)
    return _ffn(h, te, nt, ones, w_in, w_out, D_FF, FFN_TM)


def _moe(h, logits, w_in, w_out):
    tm = MOE_TM
    lg = logits[:, :N_EXPERTS]
    lane = lax.broadcasted_iota(jnp.int32, lg.shape, 1)
    v1 = jnp.max(lg, axis=-1, keepdims=True)
    i1 = jnp.min(jnp.where(lg == v1, lane, N_EXPERTS), axis=-1, keepdims=True)
    lg2 = jnp.where(lane == i1, -jnp.inf, lg)
    v2 = jnp.max(lg2, axis=-1, keepdims=True)
    i2 = jnp.min(jnp.where(lg2 == v2, lane, N_EXPERTS), axis=-1, keepdims=True)
    idx = jnp.concatenate([i1, i2], axis=-1)
    wts = jax.nn.softmax(jnp.concatenate([v1, v2], axis=-1), axis=-1)
    eid = idx.reshape(-1).astype(jnp.int32)
    onehot = (eid[:, None] == jnp.arange(N_EXPERTS, dtype=jnp.int32)[None, :]).astype(jnp.int32)
    csum = jnp.cumsum(onehot, axis=0)
    rank = jnp.take_along_axis(csum, eid[:, None], axis=1)[:, 0] - 1
    counts = csum[-1]
    ptiles = (counts + tm - 1) // tm
    tile_end = jnp.cumsum(ptiles)
    tile_start = tile_end - ptiles
    pos = tile_start[eid] * tm + rank
    n_tiles = tile_end[-1]
    tile_ids = jnp.arange(MOE_TILES, dtype=jnp.int32)
    te = jnp.sum((tile_ids[:, None] >= tile_end[None, :]).astype(jnp.int32), axis=1)
    te = jnp.minimum(te, N_EXPERTS - 1)
    te_last = te[jnp.maximum(n_tiles - 1, 0)]
    te = jnp.where(tile_ids < n_tiles, te, te_last)
    rows = MOE_TILES * tm
    tok = jnp.arange(TOP_K * T_ALL, dtype=jnp.int32) // TOP_K
    row_token = jnp.zeros((rows,), jnp.int32).at[pos].set(tok)
    row_w = jnp.zeros((rows,), F32).at[pos].set(wts.reshape(-1))
    hs = jnp.take(h, row_token, axis=0)
    ys = _ffn(hs, te, n_tiles.reshape(1).astype(jnp.int32), row_w.reshape(rows, 1),
              w_in, w_out, D_FF_EXPERT, tm)
    return jnp.take(ys, pos, axis=0).reshape(T_ALL, TOP_K * D_MODEL)


CB_RQ, CB_RK, CB_RV, CB_RG = 0, 6, 12, 18
CB_NQ, CB_NK, CB_NV = 24, 29, 34
CB_WQ, CB_WK, CB_WV = 39, 44, 45
OB_RET, OB_NA, OB_WIN = 0, 6, 11
N_RET_PAIRS = H_RET // 2
N_NA_PAIRS = H_NA // 2
N_WIN_PAIRS = H_WIN // 2
NA_WIN_KEYS = NA_ROWS * GRID_W
LAT_ROWS = DEC_SEQ // GRID_W


def _lane_lo(shape):
    return lax.broadcasted_iota(jnp.int32, shape, len(shape) - 1) < HEAD_DIM


def _dot_nt(a, b):
    return lax.dot_general(a, b, (((1,), (1,)), ((), ())), preferred_element_type=F32)


def _dot(a, b):
    return jnp.dot(a, b, preferred_element_type=F32)


def _attn_pair(q, ks, vs, biases, sinks):
    lo = _lane_lo(q.shape)
    outs = []
    for half in (0, 1):
        qm = jnp.where(lo if half == 0 else jnp.logical_not(lo), q, 0.0).astype(BF16)
        ss = []
        for kb, bb in zip(ks, biases):
            s = _dot_nt(qm, kb)
            if bb is not None:
                s = s + bb[half]
            ss.append(s)
        m = jnp.max(ss[0], axis=-1, keepdims=True)
        for s in ss[1:]:
            m = jnp.maximum(m, jnp.max(s, axis=-1, keepdims=True))
        if sinks is not None:
            m = jnp.maximum(m, sinks[half])
        ps = [jnp.exp(s - m) for s in ss]
        den = jnp.sum(ps[0], axis=-1, keepdims=True)
        for p in ps[1:]:
            den = den + jnp.sum(p, axis=-1, keepdims=True)
        if sinks is not None:
            den = den + jnp.exp(sinks[half] - m)
        o = _dot(ps[0].astype(BF16), vs[0])
        for p, vb in zip(ps[1:], vs[1:]):
            o = o + _dot(p.astype(BF16), vb)
        outs.append(o / den)
    return jnp.where(lo, outs[0], outs[1])


def _block_diag(s0, s1):
    z = jnp.zeros_like(s0)
    return jnp.concatenate([jnp.concatenate([s0, z], axis=1), jnp.concatenate([z, s1], axis=1)], axis=0)


def _ret_pair(q, k, v, g, lgf, lgb, s0f, s0b, seq):
    c = RET_CHUNK
    n = seq // c
    lo1 = _lane_lo((1, LANES))
    lo = _lane_lo((c, LANES))
    lgf_v = jnp.where(lo1, lgf[0], lgf[1])
    lgb_v = jnp.where(lo1, lgb[0], lgb[1])
    pos = lax.broadcasted_iota(jnp.int32, (c, LANES), 0).astype(F32)
    qw_f = jnp.exp(lgf_v * (pos + 1.0))
    kw_f = jnp.exp(lgf_v * (c - 1.0 - pos))
    qw_b = jnp.exp(lgb_v * (c - pos))
    kw_b = jnp.exp(lgb_v * pos)
    gc_f = jnp.exp(lgf_v * float(c))
    gc_b = jnp.exp(lgb_v * float(c))
    diff = (lax.broadcasted_iota(jnp.int32, (c, c), 0)
            - lax.broadcasted_iota(jnp.int32, (c, c), 1)).astype(F32)
    decay = [jnp.where(diff >= 0, jnp.exp(lgf[h] * jnp.maximum(diff, 0.0)), 0.0)
             + jnp.where(diff <= 0, jnp.exp(lgb[h] * jnp.maximum(-diff, 0.0)), 0.0) for h in (0, 1)]
    bd = ((lax.broadcasted_iota(jnp.int32, (LANES, LANES), 0) < HEAD_DIM)
          == (lax.broadcasted_iota(jnp.int32, (LANES, LANES), 1) < HEAD_DIM))
    zero = jnp.zeros((LANES, LANES), F32)
    sf = zero if s0f is None else s0f
    sb = zero if s0b is None else s0b
    qs = [q[i * c:(i + 1) * c] for i in range(n)]
    ks = [k[i * c:(i + 1) * c] * SCALE for i in range(n)]
    vs = [v[i * c:(i + 1) * c].astype(BF16) for i in range(n)]
    outs = []
    for i in range(n):
        kb = ks[i].astype(BF16)
        p0 = (_dot_nt(jnp.where(lo, qs[i], 0.0).astype(BF16), kb) * decay[0]).astype(BF16)
        p1 = (_dot_nt(jnp.where(lo, 0.0, qs[i]).astype(BF16), kb) * decay[1]).astype(BF16)
        o = jnp.where(lo, _dot(p0, vs[i]), _dot(p1, vs[i]))
        o = o + _dot((qs[i] * qw_f).astype(BF16), sf.astype(BF16))
        sf = gc_f * sf + jnp.where(bd, _dot((ks[i] * kw_f).T.astype(BF16), vs[i]), 0.0)
        outs.append(o)
    for i in reversed(range(n)):
        outs[i] = outs[i] + _dot((qs[i] * qw_b).astype(BF16), sb.astype(BF16))
        sb = gc_b * sb + jnp.where(bd, _dot((ks[i] * kw_b).T.astype(BF16), vs[i]), 0.0)
    o = jnp.concatenate(outs, axis=0) if n > 1 else outs[0]
    lo_s = _lane_lo((seq, LANES))
    inv_d = 1.0 / HEAD_DIM
    mu = jnp.where(lo_s, jnp.sum(jnp.where(lo_s, o, 0.0), axis=-1, keepdims=True),
                   jnp.sum(jnp.where(lo_s, 0.0, o), axis=-1, keepdims=True)) * inv_d
    d = o - mu
    d2 = d * d
    var = jnp.where(lo_s, jnp.sum(jnp.where(lo_s, d2, 0.0), axis=-1, keepdims=True),
                    jnp.sum(jnp.where(lo_s, 0.0, d2), axis=-1, keepdims=True)) * inv_d
    y = d * lax.rsqrt(var + EPS) * (g * jax.nn.sigmoid(g))
    return y, sf, sb


def _ctx_mixer_kernel(lgf_ref, lgb_ref, sink_ref, p_ref, o_ref, sf_ref, sb_ref,
                      nk_ref, nv_ref, wk_ref, wv_ref):
    def col(blk):
        return p_ref[:, blk * LANES:(blk + 1) * LANES]

    for hp in range(N_RET_PAIRS):
        y, sf, sb = _ret_pair(col(CB_RQ + hp), col(CB_RK + hp), col(CB_RV + hp), col(CB_RG + hp),
                              (lgf_ref[2 * hp], lgf_ref[2 * hp + 1]),
                              (lgb_ref[2 * hp], lgb_ref[2 * hp + 1]), None, None, SEQ)
        o_ref[:, (OB_RET + hp) * LANES:(OB_RET + hp + 1) * LANES] = y.astype(o_ref.dtype)
        sf_ref[2 * hp] = sf[:HEAD_DIM, :HEAD_DIM]
        sf_ref[2 * hp + 1] = sf[HEAD_DIM:, HEAD_DIM:]
        sb_ref[2 * hp] = sb[:HEAD_DIM, :HEAD_DIM]
        sb_ref[2 * hp + 1] = sb[HEAD_DIM:, HEAD_DIM:]

    for hp in range(N_NA_PAIRS):
        k = col(CB_NK + hp)
        v = col(CB_NV + hp)
        o = _attn_pair(col(CB_NQ + hp) * SCALE, [k.astype(BF16)], [v.astype(BF16)], [None], None)
        o_ref[:, (OB_NA + hp) * LANES:(OB_NA + hp + 1) * LANES] = o.astype(o_ref.dtype)
        nk_ref[2 * hp] = k[:, :HEAD_DIM]
        nk_ref[2 * hp + 1] = k[:, HEAD_DIM:]
        nv_ref[2 * hp] = v[:, :HEAD_DIM]
        nv_ref[2 * hp + 1] = v[:, HEAD_DIM:]

    k = col(CB_WK)
    v = col(CB_WV)
    for kv in range(KV_WIN):
        wk_ref[kv] = k[:, kv * HEAD_DIM:(kv + 1) * HEAD_DIM]
        wv_ref[kv] = v[:, kv * HEAD_DIM:(kv + 1) * HEAD_DIM]
    lo = _lane_lo(k.shape)
    k_sw = pltpu.roll(k, HEAD_DIM, 1)
    v_sw = pltpu.roll(v, HEAD_DIM, 1)
    for hp in range(N_WIN_PAIRS):
        kv_lo = (2 * hp) // G_WIN
        kv_hi = (2 * hp + 1) // G_WIN
        kk = jnp.where(lo, k if kv_lo == 0 else k_sw, k if kv_hi == 1 else k_sw).astype(BF16)
        vv = jnp.where(lo, v if kv_lo == 0 else v_sw, v if kv_hi == 1 else v_sw).astype(BF16)
        o = _attn_pair(col(CB_WQ + hp) * SCALE, [kk], [vv], [None],
                       (sink_ref[2 * hp], sink_ref[2 * hp + 1]))
        o_ref[:, (OB_WIN + hp) * LANES:(OB_WIN + hp + 1) * LANES] = o.astype(o_ref.dtype)


def _smem_spec():
    return pl.BlockSpec(memory_space=pltpu.SMEM)


def _ctx_mixer(proj, lgf, lgb, sink, n_ctx):
    t = proj.shape[0]
    out_shape = (
        jax.ShapeDtypeStruct((t, W_MIX), BF16),
        jax.ShapeDtypeStruct((n_ctx, H_RET, HEAD_DIM, HEAD_DIM), F32),
        jax.ShapeDtypeStruct((n_ctx, H_RET, HEAD_DIM, HEAD_DIM), F32),
        jax.ShapeDtypeStruct((n_ctx, H_NA, SEQ, HEAD_DIM), F32),
        jax.ShapeDtypeStruct((n_ctx, H_NA, SEQ, HEAD_DIM), F32),
        jax.ShapeDtypeStruct((n_ctx, KV_WIN, SEQ, HEAD_DIM), F32),
        jax.ShapeDtypeStruct((n_ctx, KV_WIN, SEQ, HEAD_DIM), F32),
    )
    st = lambda h, a, b: pl.BlockSpec((None, h, a, b), lambda i: (i, 0, 0, 0))
    return pl.pallas_call(
        _ctx_mixer_kernel, out_shape=out_shape, grid=(n_ctx,),
        in_specs=[_smem_spec(), _smem_spec(), _smem_spec(),
                  pl.BlockSpec((SEQ, W_IN), lambda i: (i, 0))],
        out_specs=(pl.BlockSpec((SEQ, W_MIX), lambda i: (i, 0)),
                   st(H_RET, HEAD_DIM, HEAD_DIM), st(H_RET, HEAD_DIM, HEAD_DIM),
                   st(H_NA, SEQ, HEAD_DIM), st(H_NA, SEQ, HEAD_DIM),
                   st(KV_WIN, SEQ, HEAD_DIM), st(KV_WIN, SEQ, HEAD_DIM)),
        compiler_params=_cparams(("parallel",)), name="ctx_mixer",
    )(lgf, lgb, sink, proj)


def _ret_lat_kernel(lgf_ref, lgb_ref, q_ref, k_ref, v_ref, g_ref, s0f_ref, s0b_ref, o_in_ref, o_ref):
    hp = pl.program_id(1)
    y, _, _ = _ret_pair(q_ref[...], k_ref[...], v_ref[...], g_ref[...],
                        (lgf_ref[2 * hp], lgf_ref[2 * hp + 1]), (lgb_ref[2 * hp], lgb_ref[2 * hp + 1]),
                        _block_diag(s0f_ref[0], s0f_ref[1]), _block_diag(s0b_ref[0], s0b_ref[1]), DEC_SEQ)
    o_ref[...] = y.astype(o_ref.dtype)


def _ret_latent(proj, o, lgf, lgb, s0f, s0b, l, n_lat):
    rb0 = (proj.shape[0] - n_lat * DEC_SEQ) // DEC_SEQ
    cb = lambda c0: pl.BlockSpec((DEC_SEQ, LANES), lambda b, hp: (rb0 + b, c0 + hp))
    st = pl.BlockSpec((None, None, 2, HEAD_DIM, HEAD_DIM), lambda b, hp: (b, l, hp, 0, 0))
    return pl.pallas_call(
        _ret_lat_kernel, out_shape=jax.ShapeDtypeStruct(o.shape, o.dtype),
        grid=(n_lat, N_RET_PAIRS),
        in_specs=[_smem_spec(), _smem_spec(), cb(CB_RQ), cb(CB_RK), cb(CB_RV), cb(CB_RG), st, st,
                  pl.BlockSpec(memory_space=pl.ANY)],
        out_specs=pl.BlockSpec((DEC_SEQ, LANES), lambda b, hp: (rb0 + b, OB_RET + hp)),
        input_output_aliases={8: 0},
        compiler_params=_cparams(("parallel", "parallel")), name="ret_latent",
    )(lgf, lgb, proj, proj, proj, proj, s0f, s0b, o)


def _na_lat_kernel(q_ref, k_ref, v_ref, kc_ref, vc_ref, bias_ref, o_in_ref, o_ref):
    kc = jnp.concatenate([kc_ref[0], kc_ref[1]], axis=1).astype(BF16)
    vc = jnp.concatenate([vc_ref[0], vc_ref[1]], axis=1).astype(BF16)

    def body(r, carry):
        r0 = jnp.clip(r - NA_ROWS // 2, 0, LAT_ROWS - NA_ROWS)
        q0 = pl.multiple_of(r * GRID_W, GRID_W)
        k0 = pl.multiple_of(r0 * GRID_W, GRID_W)
        q = q_ref[pl.ds(q0, GRID_W), :] * SCALE
        kw = k_ref[pl.ds(k0, NA_WIN_KEYS), :].astype(BF16)
        vw = v_ref[pl.ds(k0, NA_WIN_KEYS), :].astype(BF16)
        o = _attn_pair(q, [kw, kc], [vw, vc], [(bias_ref[0, r], bias_ref[1, r]), None], None)
        o_ref[pl.ds(q0, GRID_W), :] = o.astype(o_ref.dtype)
        return carry

    lax.fori_loop(0, LAT_ROWS, body, 0)


def _na_bias_table(rpb):
    r = np.arange(LAT_ROWS)
    r0 = np.clip(r - NA_ROWS // 2, 0, LAT_ROWS - NA_ROWS)
    dr = r0[:, None] + np.arange(NA_ROWS)[None, :] - r[:, None] + (NA_ROWS - 1)
    cols = np.arange(GRID_W)
    c0 = np.clip(cols - NA_COLS // 2, 0, GRID_W - NA_COLS)
    col_ok = (cols[None, :] >= c0[:, None]) & (cols[None, :] < c0[:, None] + NA_COLS)
    dc = np.clip(cols[None, :] - cols[:, None], -(NA_COLS - 1), NA_COLS - 1) + (NA_COLS - 1)
    by_col = jnp.where(col_ok[None, None], rpb.astype(F32)[:, :, dc], NEG)
    tbl = jnp.take(by_col, jnp.asarray(dr.reshape(-1)), axis=1)
    tbl = tbl.reshape(H_NA, LAT_ROWS, NA_ROWS, GRID_W, GRID_W).transpose(0, 1, 3, 2, 4)
    return tbl.reshape(H_NA, LAT_ROWS, GRID_W, NA_WIN_KEYS)


def _na_latent(proj, o, kc, vc, bias, l, n_lat):
    rb0 = (proj.shape[0] - n_lat * DEC_SEQ) // DEC_SEQ
    cb = lambda c0: pl.BlockSpec((DEC_SEQ, LANES), lambda b, hp: (rb0 + b, c0 + hp))
    cache = pl.BlockSpec((None, None, 2, PAST_LEN, HEAD_DIM), lambda b, hp: (b, l, hp, 0, 0))
    return pl.pallas_call(
        _na_lat_kernel, out_shape=jax.ShapeDtypeStruct(o.shape, o.dtype),
        grid=(n_lat, N_NA_PAIRS),
        in_specs=[cb(CB_NQ), cb(CB_NK), cb(CB_NV), cache, cache,
                  pl.BlockSpec((2, LAT_ROWS, GRID_W, NA_WIN_KEYS), lambda b, hp: (hp, 0, 0, 0)),
                  pl.BlockSpec(memory_space=pl.ANY)],
        out_specs=pl.BlockSpec((DEC_SEQ, LANES), lambda b, hp: (rb0 + b, OB_NA + hp)),
        input_output_aliases={6: 0},
        compiler_params=_cparams(("parallel", "parallel")), name="na_latent",
    )(proj, proj, proj, kc, vc, bias, o)


def _rope_tables():
    t = jnp.arange(DEC_SEQ)
    d = np.arange(LANES) % HEAD_DIM
    quarter = HEAD_DIM // 4
    inv = ROPE_BASE ** (-jnp.arange(quarter, dtype=F32) / quarter)
    pos = jnp.where(jnp.asarray(d < HEAD_DIM // 2)[None, :], (t // GRID_W)[:, None], (t % GRID_W)[:, None])
    ang = pos.astype(F32) * inv[d % quarter][None, :]
    sign = jnp.asarray(np.where((d & quarter) == 0, -1.0, 1.0), F32)
    return jnp.cos(ang), jnp.sin(ang) * sign[None, :]


def _win_lat_kernel(sink_ref, q_ref, k_ref, v_ref, kc_ref, vc_ref, cos_ref, sin_ref, o_in_ref, o_ref,
                    qs_ref, kp_ref, vp_ref):
    hp = pl.program_id(1)
    quarter = HEAD_DIM // 4
    lane = lax.broadcasted_iota(jnp.int32, (DEC_SEQ, LANES), 1)
    first = (lane & quarter) == 0
    lo = lane < HEAD_DIM
    cos = cos_ref[...]
    sin = sin_ref[...]

    def rope(x):
        sw = jnp.where(first, pltpu.roll(x, LANES - quarter, 1), pltpu.roll(x, quarter, 1))
        return x * cos + sw * sin

    qs_ref[...] = rope(q_ref[...]) * SCALE
    k = rope(k_ref[...])
    v = v_ref[...]
    lo_orig = (2 * hp) // G_WIN == 0
    hi_orig = (2 * hp + 1) // G_WIN == 1
    orig = jnp.where(lo, lo_orig.astype(jnp.int32), hi_orig.astype(jnp.int32)) == 1
    zeros = jnp.zeros((WIN_BLOCK, LANES), BF16)
    kp_ref[:WIN_BLOCK] = zeros
    kp_ref[WIN_BLOCK + DEC_SEQ:] = zeros
    vp_ref[:WIN_BLOCK] = zeros
    vp_ref[WIN_BLOCK + DEC_SEQ:] = zeros
    kp_ref[WIN_BLOCK:WIN_BLOCK + DEC_SEQ] = jnp.where(orig, k, pltpu.roll(k, HEAD_DIM, 1)).astype(BF16)
    vp_ref[WIN_BLOCK:WIN_BLOCK + DEC_SEQ] = jnp.where(orig, v, pltpu.roll(v, HEAD_DIM, 1)).astype(BF16)
    kc = jnp.concatenate([jnp.where(lo_orig, kc_ref[0], kc_ref[1]),
                          jnp.where(hi_orig, kc_ref[1], kc_ref[0])], axis=1).astype(BF16)
    vc = jnp.concatenate([jnp.where(lo_orig, vc_ref[0], vc_ref[1]),
                          jnp.where(hi_orig, vc_ref[1], vc_ref[0])], axis=1).astype(BF16)
    sinks = (sink_ref[2 * hp], sink_ref[2 * hp + 1])
    n_band = 3 * WIN_BLOCK
    qi = lax.broadcasted_iota(jnp.int32, (WIN_BLOCK, n_band), 0)
    kj = lax.broadcasted_iota(jnp.int32, (WIN_BLOCK, n_band), 1)
    near = jnp.abs(qi + WIN_BLOCK - kj) <= WIN_HALF

    def body(n, carry):
        q0 = pl.multiple_of(n * WIN_BLOCK, WIN_BLOCK)
        kpos = (n - 1) * WIN_BLOCK + kj
        ok = jnp.logical_and(near, jnp.logical_and(kpos >= 0, kpos < DEC_SEQ))
        band = jnp.where(ok, 0.0, NEG)
        o = _attn_pair(qs_ref[pl.ds(q0, WIN_BLOCK), :],
                       [kp_ref[pl.ds(q0, n_band), :], kc], [vp_ref[pl.ds(q0, n_band), :], vc],
                       [(band, band), None], sinks)
        o_ref[pl.ds(q0, WIN_BLOCK), :] = o.astype(o_ref.dtype)
        return carry

    lax.fori_loop(0, DEC_SEQ // WIN_BLOCK, body, 0)


def _win_latent(proj, o, kc, vc, sink, cos, sin, l, n_lat):
    rb0 = (proj.shape[0] - n_lat * DEC_SEQ) // DEC_SEQ
    cache = pl.BlockSpec((None, None, KV_WIN, PAST_LEN, HEAD_DIM), lambda b, hp: (b, l, 0, 0, 0))
    tbl = pl.BlockSpec((DEC_SEQ, LANES), lambda b, hp: (0, 0))
    return pl.pallas_call(
        _win_lat_kernel, out_shape=jax.ShapeDtypeStruct(o.shape, o.dtype),
        grid=(n_lat, N_WIN_PAIRS),
        in_specs=[_smem_spec(),
                  pl.BlockSpec((DEC_SEQ, LANES), lambda b, hp: (rb0 + b, CB_WQ + hp)),
                  pl.BlockSpec((DEC_SEQ, LANES), lambda b, hp: (rb0 + b, CB_WK)),
                  pl.BlockSpec((DEC_SEQ, LANES), lambda b, hp: (rb0 + b, CB_WV)),
                  cache, cache, tbl, tbl, pl.BlockSpec(memory_space=pl.ANY)],
        out_specs=pl.BlockSpec((DEC_SEQ, LANES), lambda b, hp: (rb0 + b, OB_WIN + hp)),
        scratch_shapes=[pltpu.VMEM((DEC_SEQ, LANES), F32),
                        pltpu.VMEM((DEC_SEQ + 2 * WIN_BLOCK, LANES), BF16),
                        pltpu.VMEM((DEC_SEQ + 2 * WIN_BLOCK, LANES), BF16)],
        input_output_aliases={8: 0},
        compiler_params=_cparams(("parallel", "parallel")), name="win_latent",
    )(sink, proj, proj, proj, kc, vc, cos, sin, o)


def _mixers(proj, l, n_ctx, n_lat, state_ret_fwd, state_ret_bwd, cache_na_k, cache_na_v,
            cache_win_k, cache_win_v, ret_decay_fwd, ret_decay_bwd, na_rpb, win_sink):
    lgf = jax.nn.log_sigmoid(ret_decay_fwd[l].astype(F32))
    lgb = jax.nn.log_sigmoid(ret_decay_bwd[l].astype(F32))
    sink = win_sink[l].astype(F32)
    o, sf, sb, nk, nv, wk, wv = _ctx_mixer(proj, lgf, lgb, sink, n_ctx)
    o = _ret_latent(proj, o, lgf, lgb, state_ret_fwd, state_ret_bwd, l, n_lat)
    o = _na_latent(proj, o, cache_na_k, cache_na_v, _na_bias_table(na_rpb[l]), l, n_lat)
    cos, sin = _rope_tables()
    o = _win_latent(proj, o, cache_win_k, cache_win_v, sink, cos, sin, l, n_lat)
    return o, (sf, sb, nk, nv, wk, wv)


def kernel(x_prompt, x_sample, c, state_ret_fwd, state_ret_bwd, cache_na_k, cache_na_v, cache_win_k, cache_win_v, c_ctx, norm1_g, norm2_g, ada_w, ada_b, w_in, w_out, ret_decay_fwd, ret_decay_bwd, na_rpb, win_sink, ffn_w_in, ffn_w_out, moe_router, moe_w_in, moe_w_out, final_norm_g):
    x = jnp.concatenate([x_prompt.reshape(T_CTX, D_MODEL), x_sample.reshape(T_LAT, D_MODEL)], axis=0)
    cond = jnp.concatenate([c_ctx[None, :], c, jnp.zeros((N_GROUPS - 1 - DEC_BATCH, D_MODEL), F32)], axis=0)
    ctx_states = []
    for l in range(DEPTH):
        mod = _adaln(cond, ada_w, ada_b, l)
        h = _norm_mod(x, norm1_g, mod, l, 0, 1)
        proj = _matmul(h, w_in, l)
        o, ctx = _mixers(proj, l, BATCH, DEC_BATCH, state_ret_fwd, state_ret_bwd, cache_na_k, cache_na_v,
                         cache_win_k, cache_win_v, ret_decay_fwd, ret_decay_bwd, na_rpb, win_sink)
        ctx_states.append(ctx)
        x = _matmul(o, w_out, l, resid=x, mod=mod, k_gate=2)
        i = l // 2
        if l % 2 == 0:
            h = _norm_mod(x, norm2_g, mod, l, 3, 4)
            y = _dense_ffn(h, ffn_w_in, ffn_w_out, i)
        else:
            router = jnp.pad(moe_router[i], ((0, 0), (0, LANES - N_EXPERTS)))
            h, logits = _norm_mod(x, norm2_g, mod, l, 3, 4, router=router)
            y = _moe(h, logits, moe_w_in[i], moe_w_out[i])
        x = _resid(x, y, mod, 5)
    y = _final_norm(x, final_norm_g)
    y_prompt = y[:T_CTX].reshape(BATCH, SEQ, D_MODEL)
    y_sample = y[T_CTX:].reshape(DEC_BATCH, DEC_SEQ, D_MODEL)
    dt = x_prompt.dtype
    outs = [jnp.stack([s[k] for s in ctx_states], axis=1).astype(dt) for k in range(6)]
    return (y_prompt, y_sample, *outs)
```

```python
import functools

import jax
import jax.numpy as jnp
from jax import lax
import numpy as np
from jax.experimental import pallas as pl
from jax.experimental.pallas import tpu as pltpu

D_MODEL = 2048
BATCH = 32
SEQ = 256
DEPTH = 2
DEC_BATCH = 4
DEC_SEQ = 1024
PAST_LEN = 256

GRID_W = 64
HEAD_DIM = 64
H_RET = 12
H_NA = 10
H_WIN = 10
KV_WIN = 2
G_WIN = H_WIN // KV_WIN
W_RET = H_RET * HEAD_DIM
W_NA = H_NA * HEAD_DIM
W_WIN = H_WIN * HEAD_DIM
W_MIX = W_RET + W_NA + W_WIN
W_IN = 4 * W_RET + 3 * W_NA + W_WIN + 2 * KV_WIN * HEAD_DIM
RET_CHUNK = 128
NA_ROWS = 8
NA_COLS = 16
WIN_HALF = 128
WIN_BLOCK = 128
ROPE_BASE = 10000.0
D_FF = 5632
N_EXPERTS = 8
TOP_K = 2
D_FF_EXPERT = 7168
EPS = 1e-6
NEG = -1e30
SCALE = HEAD_DIM ** -0.5

T_CTX = BATCH * SEQ
T_LAT = DEC_BATCH * DEC_SEQ
T_ALL = T_CTX + T_LAT
N_GROUPS = 8
LANES = 128

F32 = jnp.float32
BF16 = jnp.bfloat16

VMEM_LIMIT = 56 * 1024 * 1024


def _group_of_tile(i, tm):
    return jnp.maximum((i * tm - T_CTX) // DEC_SEQ + 1, 0)


def _cparams(sem):
    return pltpu.CompilerParams(dimension_semantics=sem, vmem_limit_bytes=VMEM_LIMIT)


def _adaln_kernel(c_ref, w_ref, b_ref, o_ref):
    c = c_ref[...]
    s = (c * jax.nn.sigmoid(c)).astype(BF16)
    o_ref[...] = jnp.dot(s, w_ref[...].astype(BF16), preferred_element_type=F32) + b_ref[...]


def _adaln(cond, ada_w, ada_b, l):
    tn = 1024
    n = 6 * D_MODEL
    out = pl.pallas_call(
        _adaln_kernel,
        out_shape=jax.ShapeDtypeStruct((N_GROUPS, n), F32),
        grid=(n // tn,),
        in_specs=[
            pl.BlockSpec((N_GROUPS, D_MODEL), lambda j: (0, 0)),
            pl.BlockSpec((None, D_MODEL, tn), lambda j: (l, 0, j)),
            pl.BlockSpec((None, 1, tn), lambda j: (l, 0, j)),
        ],
        out_specs=pl.BlockSpec((N_GROUPS, tn), lambda j: (0, j)),
        compiler_params=_cparams(("arbitrary",)),
        name="adaln",
    )(cond, ada_w, ada_b.reshape(DEPTH, 1, n))
    return out.reshape(N_GROUPS, 6, 1, D_MODEL)


N_SUB = D_MODEL // LANES


def _store_rows3(ref3, val):
    for s in range(N_SUB):
        ref3[:, s, :] = val[:, s * LANES:(s + 1) * LANES].astype(ref3.dtype)


def _load_rows3(ref3, dtype=F32):
    return jnp.concatenate([ref3[:, s, :].astype(dtype) for s in range(N_SUB)], axis=1)


def _norm_mod_body(x_ref, g_ref, sh_ref, sc_ref):
    x = x_ref[...]
    ms = jnp.mean(x * x, axis=-1, keepdims=True)
    y = x * lax.rsqrt(ms + EPS) * g_ref[...]
    return y * (1.0 + sc_ref[...]) + sh_ref[...]


def _norm_mod_kernel(x_ref, g_ref, sh_ref, sc_ref, h_ref):
    h_ref[...] = _norm_mod_body(x_ref, g_ref, sh_ref, sc_ref).astype(h_ref.dtype)


def _norm_mod3_kernel(x_ref, g_ref, sh_ref, sc_ref, h3_ref):
    _store_rows3(h3_ref, _norm_mod_body(x_ref, g_ref, sh_ref, sc_ref))


R_E0, R_E1, R_RANK0, R_RANK1, R_W0, R_W1 = range(6)


def _norm_mod_router_kernel(x_ref, g_ref, sh_ref, sc_ref, r_ref, h3_ref, route_ref, cnt_ref, run_ref):
    i = pl.program_id(0)

    @pl.when(i == 0)
    def _():
        run_ref[...] = jnp.zeros_like(run_ref)

    h = _norm_mod_body(x_ref, g_ref, sh_ref, sc_ref)
    _store_rows3(h3_ref, h)
    tm = h.shape[0]
    lg = jnp.dot(h, r_ref[...], preferred_element_type=F32, precision=lax.Precision.HIGHEST)
    lane = lax.broadcasted_iota(jnp.int32, (tm, LANES), 1)
    lane_f = lane.astype(F32)
    lg = jnp.where(lane < N_EXPERTS, lg, -jnp.inf)
    v0 = jnp.max(lg, axis=-1, keepdims=True)
    e0 = jnp.min(jnp.where(lg == v0, lane_f, float(LANES)), axis=-1, keepdims=True)
    lg1 = jnp.where(lane_f == e0, -jnp.inf, lg)
    v1 = jnp.max(lg1, axis=-1, keepdims=True)
    e1 = jnp.min(jnp.where(lg1 == v1, lane_f, float(LANES)), axis=-1, keepdims=True)
    ex = jnp.exp(v1 - v0)
    w0 = 1.0 / (1.0 + ex)
    w1 = ex / (1.0 + ex)
    oh0 = jnp.where(lane_f == e0, 1.0, 0.0)
    oh1 = jnp.where(lane_f == e1, 1.0, 0.0)
    oh = oh0 + oh1
    earlier = (lax.broadcasted_iota(jnp.int32, (tm, tm), 0)
               > lax.broadcasted_iota(jnp.int32, (tm, tm), 1))
    before = jnp.dot(jnp.where(earlier, 1.0, 0.0).astype(BF16), oh.astype(BF16),
                     preferred_element_type=F32) + run_ref[0:1, :]
    rank0 = jnp.sum(oh0 * before, axis=-1, keepdims=True)
    rank1 = jnp.sum(oh1 * before, axis=-1, keepdims=True)
    rec = jnp.zeros((tm, LANES), F32)
    for k, val in ((R_E0, e0), (R_E1, e1), (R_RANK0, rank0), (R_RANK1, rank1), (R_W0, w0), (R_W1, w1)):
        rec = jnp.where(lane == k, val, rec)
    route_ref[...] = rec
    run_ref[0:1, :] = run_ref[0:1, :] + jnp.sum(oh, axis=0, keepdims=True)
    cnt_ref[...] = run_ref[...]


def _norm_mod(x, g, mod, l, k_shift, k_scale, *, rows3=False, router=None):
    tm = 512
    g3 = g.reshape(DEPTH, 1, D_MODEL)
    in_specs = [
        pl.BlockSpec((tm, D_MODEL), lambda i: (i, 0)),
        pl.BlockSpec((None, 1, D_MODEL), lambda i: (l, 0, 0)),
        pl.BlockSpec((None, None, 1, D_MODEL), lambda i: (_group_of_tile(i, tm), k_shift, 0, 0)),
        pl.BlockSpec((None, None, 1, D_MODEL), lambda i: (_group_of_tile(i, tm), k_scale, 0, 0)),
    ]
    h3_shape = jax.ShapeDtypeStruct((T_ALL, N_SUB, LANES), F32)
    h3_spec = pl.BlockSpec((tm, N_SUB, LANES), lambda i: (i, 0, 0))
    if router is not None:
        in_specs.append(pl.BlockSpec((D_MODEL, LANES), lambda i: (0, 0)))
        return pl.pallas_call(
            _norm_mod_router_kernel,
            out_shape=(h3_shape, jax.ShapeDtypeStruct((T_ALL, LANES), F32),
                       jax.ShapeDtypeStruct((8, LANES), F32)),
            grid=(T_ALL // tm,),
            in_specs=in_specs,
            out_specs=(h3_spec, pl.BlockSpec((tm, LANES), lambda i: (i, 0)),
                       pl.BlockSpec((8, LANES), lambda i: (0, 0))),
            scratch_shapes=[pltpu.VMEM((8, LANES), F32)],
            compiler_params=_cparams(("arbitrary",)), name="norm_mod_router",
        )(x, g3, mod, mod, router)
    if rows3:
        return pl.pallas_call(
            _norm_mod3_kernel, out_shape=h3_shape, grid=(T_ALL // tm,),
            in_specs=in_specs, out_specs=h3_spec,
            compiler_params=_cparams(("parallel",)), name="norm_mod_rows",
        )(x, g3, mod, mod)
    return pl.pallas_call(
        _norm_mod_kernel, out_shape=jax.ShapeDtypeStruct((T_ALL, D_MODEL), BF16), grid=(T_ALL // tm,),
        in_specs=in_specs, out_specs=pl.BlockSpec((tm, D_MODEL), lambda i: (i, 0)),
        compiler_params=_cparams(("parallel",)), name="norm_mod",
    )(x, g3, mod, mod)


def _mm_kernel(a_ref, w_ref, o_ref):
    o_ref[...] = jnp.dot(a_ref[...].astype(BF16), w_ref[...].astype(BF16),
                         preferred_element_type=F32)


def _mm_res_kernel(a_ref, w_ref, x_ref, gate_ref, o_ref):
    acc = jnp.dot(a_ref[...].astype(BF16), w_ref[...].astype(BF16),
                  preferred_element_type=F32)
    o_ref[...] = x_ref[...] + gate_ref[...] * acc


def _matmul(a, w, l, *, resid=None, mod=None, k_gate=None):
    tm, tn = 1024, 512
    t, k = a.shape
    n = w.shape[-1]
    grid = (t // tm, pl.cdiv(n, tn))
    in_specs = [
        pl.BlockSpec((tm, k), lambda i, j: (i, 0)),
        pl.BlockSpec((None, k, tn), lambda i, j: (l, 0, j)),
    ]
    args = [a, w]
    kern = _mm_kernel
    if resid is not None:
        in_specs += [
            pl.BlockSpec((tm, tn), lambda i, j: (i, j)),
            pl.BlockSpec((None, None, 1, tn), lambda i, j: (_group_of_tile(i, tm), k_gate, 0, j)),
        ]
        args += [resid, mod]
        kern = _mm_res_kernel
    return pl.pallas_call(
        kern, out_shape=jax.ShapeDtypeStruct((t, n), F32), grid=grid,
        in_specs=in_specs, out_specs=pl.BlockSpec((tm, tn), lambda i, j: (i, j)),
        compiler_params=_cparams(("parallel", "arbitrary")), name="proj",
    )(*args)


def _ffn_kernel(te_ref, nt_ref, hs_ref, wa_ref, wg_ref, wo_ref, o_ref, hb_ref, acc_ref, *, n_f):
    i = pl.program_id(0)
    j = pl.program_id(1)
    valid = i < nt_ref[0]

    @pl.when(jnp.logical_and(valid, j == 0))
    def _():
        for s in range(N_SUB):
            hb_ref[:, s * LANES:(s + 1) * LANES] = hs_ref[:, s, :].astype(BF16)

    @pl.when(valid)
    def _():
        h = hb_ref[...]
        a = jnp.dot(h, wa_ref[...].astype(BF16), preferred_element_type=F32)
        g = jnp.dot(h, wg_ref[...].astype(BF16), preferred_element_type=F32)
        act = (a * jax.nn.sigmoid(a) * g).astype(BF16)
        part = jnp.dot(act, wo_ref[...].astype(BF16), preferred_element_type=F32)

        @pl.when(j == 0)
        def _():
            acc_ref[...] = part

        @pl.when(j > 0)
        def _():
            acc_ref[...] += part

        @pl.when(j == n_f - 1)
        def _():
            _store_rows3(o_ref, acc_ref[...])

    @pl.when(jnp.logical_and(jnp.logical_not(valid), j == 0))
    def _():
        o_ref[...] = jnp.zeros_like(o_ref)


def _ffn(hs3, te, nt, w_in, w_out, d_ff, tm):
    tf = 256
    r = hs3.shape[0]
    n_f = d_ff // tf

    def jj(i, j, nt_ref):
        return jnp.where(i < nt_ref[0], j, n_f - 1)

    def ii(i, nt_ref):
        return jnp.minimum(i, nt_ref[0] - 1)

    single = pl.Buffered(1)
    grid_spec = pltpu.PrefetchScalarGridSpec(
        num_scalar_prefetch=2,
        grid=(r // tm, n_f),
        in_specs=[
            pl.BlockSpec((tm, N_SUB, LANES), lambda i, j, te_ref, nt_ref: (ii(i, nt_ref), 0, 0),
                         pipeline_mode=single),
            pl.BlockSpec((None, D_MODEL, tf), lambda i, j, te_ref, nt_ref: (te_ref[i], 0, jj(i, j, nt_ref))),
            pl.BlockSpec((None, D_MODEL, tf), lambda i, j, te_ref, nt_ref: (te_ref[i], 0, jj(i, j, nt_ref) + n_f)),
            pl.BlockSpec((None, tf, D_MODEL), lambda i, j, te_ref, nt_ref: (te_ref[i], jj(i, j, nt_ref), 0)),
        ],
        out_specs=pl.BlockSpec((tm, N_SUB, LANES), lambda i, j, te_ref, nt_ref: (i, 0, 0),
                               pipeline_mode=single),
        scratch_shapes=[pltpu.VMEM((tm, D_MODEL), BF16), pltpu.VMEM((tm, D_MODEL), F32)],
    )
    return pl.pallas_call(
        functools.partial(_ffn_kernel, n_f=n_f),
        out_shape=jax.ShapeDtypeStruct((r, N_SUB, LANES), F32),
        grid_spec=grid_spec,
        compiler_params=_cparams(("arbitrary", "arbitrary")), name="ffn",
    )(te, nt, hs3, w_in, w_in, w_out)


DISPATCH_CHUNK = 512


def _row_of(a, eid_ref, rank_ref, first_row_ref):
    return first_row_ref[eid_ref[a]] + rank_ref[a]


ZERO_ROWS = 256
N_PAD_RANGES = N_EXPERTS + 1


def _dispatch_kernel(eid_ref, rank_ref, first_row_ref, pad_lo_ref, pad_hi_ref, h3_ref, hs3_ref,
                     zero_ref, sem, zsem):
    step = pl.program_id(0)

    @pl.when(step == 0)
    def _():
        zero_ref[...] = jnp.zeros_like(zero_ref)
        for e in range(N_PAD_RANGES):
            lo = pad_lo_ref[e]
            hi = pad_hi_ref[e]
            mid = jnp.minimum((lo + ZERO_ROWS - 1) // ZERO_ROWS * ZERO_ROWS, hi)

            def row_copy(r):
                return pltpu.make_async_copy(zero_ref.at[0], hs3_ref.at[r], zsem)

            def block_copy(b):
                r0 = pl.multiple_of(b * ZERO_ROWS, ZERO_ROWS)
                return pltpu.make_async_copy(zero_ref, hs3_ref.at[pl.ds(r0, ZERO_ROWS)], zsem)

            def fire(mk):
                def body(k, c):
                    mk(k).start()
                    return c
                return body

            def drain(mk):
                def body(k, c):
                    mk(k).wait()
                    return c
                return body

            lax.fori_loop(lo, mid, fire(row_copy), 0)
            lax.fori_loop(mid // ZERO_ROWS, hi // ZERO_ROWS, fire(block_copy), 0)
            lax.fori_loop(lo, mid, drain(row_copy), 0)
            lax.fori_loop(mid // ZERO_ROWS, hi // ZERO_ROWS, drain(block_copy), 0)

    base = step * DISPATCH_CHUNK

    def copy(a):
        return pltpu.make_async_copy(h3_ref.at[a // TOP_K],
                                     hs3_ref.at[_row_of(a, eid_ref, rank_ref, first_row_ref)], sem)

    def start(k, c):
        copy(base + k).start()
        return c

    def wait(k, c):
        copy(base + k).wait()
        return c

    lax.fori_loop(0, DISPATCH_CHUNK, start, 0)
    lax.fori_loop(0, DISPATCH_CHUNK, wait, 0)


def _dispatch(h3, eid, rank, first_row, pad_lo, pad_hi, rows):
    n_assign = eid.shape[0]
    grid_spec = pltpu.PrefetchScalarGridSpec(
        num_scalar_prefetch=5,
        grid=(n_assign // DISPATCH_CHUNK,),
        in_specs=[pl.BlockSpec(memory_space=pl.ANY)],
        out_specs=pl.BlockSpec(memory_space=pl.ANY),
        scratch_shapes=[pltpu.VMEM((ZERO_ROWS, N_SUB, LANES), F32),
                        pltpu.SemaphoreType.DMA, pltpu.SemaphoreType.DMA],
    )
    return pl.pallas_call(
        _dispatch_kernel,
        out_shape=jax.ShapeDtypeStruct((rows, N_SUB, LANES), F32),
        grid_spec=grid_spec,
        compiler_params=_cparams(("arbitrary",)), name="moe_dispatch",
    )(eid, rank, first_row, pad_lo, pad_hi, h3)


def _resid3_kernel(x_ref, y3_ref, gate_ref, o_ref):
    o_ref[...] = x_ref[...] + gate_ref[...] * _load_rows3(y3_ref)


def _resid3(x, y3, mod, k_gate):
    tm = 512
    return pl.pallas_call(
        _resid3_kernel,
        out_shape=jax.ShapeDtypeStruct((T_ALL, D_MODEL), F32), grid=(T_ALL // tm,),
        in_specs=[
            pl.BlockSpec((tm, D_MODEL), lambda i: (i, 0)),
            pl.BlockSpec((tm, N_SUB, LANES), lambda i: (i, 0, 0)),
            pl.BlockSpec((None, None, 1, D_MODEL), lambda i: (_group_of_tile(i, tm), k_gate, 0, 0)),
        ],
        out_specs=pl.BlockSpec((tm, D_MODEL), lambda i: (i, 0)),
        compiler_params=_cparams(("parallel",)), name="resid",
    )(x, y3, mod)


COMBINE_TM = 256


def _combine_kernel(eid_ref, rank_ref, first_row_ref, x_ref, route_ref, gate_ref, ys3_ref, o_ref,
                    buf_ref, sem):
    i = pl.program_id(0)
    n = pl.num_programs(0)
    tm = COMBINE_TM

    def copy(tile, slot, k, t):
        a = (tile * tm + t) * TOP_K + k
        return pltpu.make_async_copy(ys3_ref.at[_row_of(a, eid_ref, rank_ref, first_row_ref)],
                                     buf_ref.at[slot, k, t], sem.at[slot])

    def start_tile(tile, slot):
        for k in range(TOP_K):
            def start(t, c):
                copy(tile, slot, k, t).start()
                return c
            lax.fori_loop(0, tm, start, 0)

    @pl.when(i == 0)
    def _():
        start_tile(0, 0)

    @pl.when(i + 1 < n)
    def _():
        start_tile(i + 1, (i + 1) % 2)

    slot = i % 2
    for k in range(TOP_K):
        def wait(t, c):
            copy(i, slot, k, t).wait()
            return c
        lax.fori_loop(0, tm, wait, 0)

    rec = route_ref[...]
    w0 = rec[:, R_W0:R_W0 + 1]
    w1 = rec[:, R_W1:R_W1 + 1]
    y = w0 * _load_rows3(buf_ref.at[slot, 0]) + w1 * _load_rows3(buf_ref.at[slot, 1])
    o_ref[...] = x_ref[...] + gate_ref[...] * y


def _combine(x, ys3, route, eid, rank, first_row, mod, k_gate):
    tm = COMBINE_TM
    grid_spec = pltpu.PrefetchScalarGridSpec(
        num_scalar_prefetch=3,
        grid=(T_ALL // tm,),
        in_specs=[
            pl.BlockSpec((tm, D_MODEL), lambda i, *_: (i, 0)),
            pl.BlockSpec((tm, LANES), lambda i, *_: (i, 0)),
            pl.BlockSpec((None, None, 1, D_MODEL), lambda i, *_: (_group_of_tile(i, tm), k_gate, 0, 0)),
            pl.BlockSpec(memory_space=pl.ANY),
        ],
        out_specs=pl.BlockSpec((tm, D_MODEL), lambda i, *_: (i, 0)),
        scratch_shapes=[pltpu.VMEM((2, TOP_K, tm, N_SUB, LANES), F32), pltpu.SemaphoreType.DMA((2,))],
    )
    return pl.pallas_call(
        _combine_kernel,
        out_shape=jax.ShapeDtypeStruct((T_ALL, D_MODEL), F32),
        grid_spec=grid_spec,
        compiler_params=_cparams(("arbitrary",)), name="moe_combine",
    )(eid, rank, first_row, x, route, mod, ys3)


def _final_norm_kernel(x_ref, g_ref, o_ref):
    x = x_ref[...]
    ms = jnp.mean(x * x, axis=-1, keepdims=True)
    o_ref[...] = x * lax.rsqrt(ms + EPS) * g_ref[...]


def _final_norm(x, g):
    tm = 512
    return pl.pallas_call(
        _final_norm_kernel,
        out_shape=jax.ShapeDtypeStruct((T_ALL, D_MODEL), F32), grid=(T_ALL // tm,),
        in_specs=[pl.BlockSpec((tm, D_MODEL), lambda i: (i, 0)),
                  pl.BlockSpec((1, D_MODEL), lambda i: (0, 0))],
        out_specs=pl.BlockSpec((tm, D_MODEL), lambda i: (i, 0)),
        compiler_params=_cparams(("parallel",)), name="final_norm",
    )(x, g.reshape(1, D_MODEL))


FFN_TM = 1024
MOE_TM = 1024
MOE_TILES = (TOP_K * T_ALL) // MOE_TM + N_EXPERTS


def _dense_ffn(x, h3, w_in, w_out, i, mod, k_gate):
    n_t = T_ALL // FFN_TM
    te = jnp.full((n_t,), i, jnp.int32)
    nt = jnp.full((1,), n_t, jnp.int32)
    return _resid3(x, _ffn(h3, te, nt, w_in, w_out, D_FF, FFN_TM), mod, k_gate)


def _moe(x, h3, route, counts, w_in, w_out, mod, k_gate):
    tm = MOE_TM
    eid = route[:, R_E0:R_E1 + 1].astype(jnp.int32).reshape(-1)
    rank = route[:, R_RANK0:R_RANK1 + 1].astype(jnp.int32).reshape(-1)
    counts = counts[0, :N_EXPERTS].astype(jnp.int32)
    ptiles = (counts + tm - 1) // tm
    tile_end = jnp.cumsum(ptiles)
    first_row = (tile_end - ptiles) * tm
    n_tiles = tile_end[-1]
    tile_ids = jnp.arange(MOE_TILES, dtype=jnp.int32)
    te = jnp.sum((tile_ids[:, None] >= tile_end[None, :]).astype(jnp.int32), axis=1)
    te = jnp.minimum(te, N_EXPERTS - 1)
    te_last = jnp.sum(jnp.where(tile_ids == n_tiles - 1, te, 0))
    te = jnp.where(tile_ids < n_tiles, te, te_last)
    rows = MOE_TILES * tm
    pad_lo = jnp.concatenate([first_row + counts, tile_end[-1:] * tm])
    pad_hi = jnp.concatenate([tile_end * tm, jnp.full((1,), rows, jnp.int32)])
    hs3 = _dispatch(h3, eid, rank, first_row, pad_lo, pad_hi, rows)
    ys3 = _ffn(hs3, te, n_tiles.reshape(1), w_in, w_out, D_FF_EXPERT, tm)
    return _combine(x, ys3, route, eid, rank, first_row, mod, k_gate)


CB_RQ, CB_RK, CB_RV, CB_RG = 0, 6, 12, 18
CB_NQ, CB_NK, CB_NV = 24, 29, 34
CB_WQ, CB_WK, CB_WV = 39, 44, 45
OB_RET, OB_NA, OB_WIN = 0, 6, 11
N_RET_PAIRS = H_RET // 2
N_NA_PAIRS = H_NA // 2
N_WIN_PAIRS = H_WIN // 2
NA_WIN_KEYS = NA_ROWS * GRID_W
LAT_ROWS = DEC_SEQ // GRID_W


def _lane_lo(shape):
    return lax.broadcasted_iota(jnp.int32, shape, len(shape) - 1) < HEAD_DIM


def _dot_nt(a, b):
    return lax.dot_general(a, b, (((1,), (1,)), ((), ())), preferred_element_type=F32)


def _dot(a, b):
    return jnp.dot(a, b, preferred_element_type=F32)


def _attn_pair(q, ks, vs, biases, sinks):
    lo = _lane_lo(q.shape)
    outs = []
    for half in (0, 1):
        qm = jnp.where(lo if half == 0 else jnp.logical_not(lo), q, 0.0).astype(BF16)
        ss = []
        for kb, bb in zip(ks, biases):
            s = _dot_nt(qm, kb)
            if bb is not None:
                s = s + bb[half]
            ss.append(s)
        m = jnp.max(ss[0], axis=-1, keepdims=True)
        for s in ss[1:]:
            m = jnp.maximum(m, jnp.max(s, axis=-1, keepdims=True))
        if sinks is not None:
            m = jnp.maximum(m, sinks[half])
        ps = [jnp.exp(s - m) for s in ss]
        den = jnp.sum(ps[0], axis=-1, keepdims=True)
        for p in ps[1:]:
            den = den + jnp.sum(p, axis=-1, keepdims=True)
        if sinks is not None:
            den = den + jnp.exp(sinks[half] - m)
        o = _dot(ps[0].astype(BF16), vs[0])
        for p, vb in zip(ps[1:], vs[1:]):
            o = o + _dot(p.astype(BF16), vb)
        outs.append(o / den)
    return jnp.where(lo, outs[0], outs[1])


def _block_diag(s0, s1):
    z = jnp.zeros_like(s0)
    return jnp.concatenate([jnp.concatenate([s0, z], axis=1), jnp.concatenate([z, s1], axis=1)], axis=0)


def _ret_pair(q, k, v, g, lgf, lgb, s0f, s0b, seq):
    c = RET_CHUNK
    n = seq // c
    lo1 = _lane_lo((1, LANES))
    lo = _lane_lo((c, LANES))
    lgf_v = jnp.where(lo1, lgf[0], lgf[1])
    lgb_v = jnp.where(lo1, lgb[0], lgb[1])
    pos = lax.broadcasted_iota(jnp.int32, (c, LANES), 0).astype(F32)
    qw_f = jnp.exp(lgf_v * (pos + 1.0))
    kw_f = jnp.exp(lgf_v * (c - 1.0 - pos))
    qw_b = jnp.exp(lgb_v * (c - pos))
    kw_b = jnp.exp(lgb_v * pos)
    gc_f = jnp.exp(lgf_v * float(c))
    gc_b = jnp.exp(lgb_v * float(c))
    diff = (lax.broadcasted_iota(jnp.int32, (c, c), 0)
            - lax.broadcasted_iota(jnp.int32, (c, c), 1)).astype(F32)
    decay = [jnp.where(diff >= 0, jnp.exp(lgf[h] * jnp.maximum(diff, 0.0)), 0.0)
             + jnp.where(diff <= 0, jnp.exp(lgb[h] * jnp.maximum(-diff, 0.0)), 0.0) for h in (0, 1)]
    bd = ((lax.broadcasted_iota(jnp.int32, (LANES, LANES), 0) < HEAD_DIM)
          == (lax.broadcasted_iota(jnp.int32, (LANES, LANES), 1) < HEAD_DIM))
    zero = jnp.zeros((LANES, LANES), F32)
    sf = zero if s0f is None else s0f
    sb = zero if s0b is None else s0b
    qs = [q[i * c:(i + 1) * c] for i in range(n)]
    ks = [k[i * c:(i + 1) * c] * SCALE for i in range(n)]
    vs = [v[i * c:(i + 1) * c].astype(BF16) for i in range(n)]
    outs = []
    for i in range(n):
        kb = ks[i].astype(BF16)
        p0 = (_dot_nt(jnp.where(lo, qs[i], 0.0).astype(BF16), kb) * decay[0]).astype(BF16)
        p1 = (_dot_nt(jnp.where(lo, 0.0, qs[i]).astype(BF16), kb) * decay[1]).astype(BF16)
        o = jnp.where(lo, _dot(p0, vs[i]), _dot(p1, vs[i]))
        o = o + _dot((qs[i] * qw_f).astype(BF16), sf.astype(BF16))
        sf = gc_f * sf + jnp.where(bd, _dot((ks[i] * kw_f).T.astype(BF16), vs[i]), 0.0)
        outs.append(o)
    for i in reversed(range(n)):
        outs[i] = outs[i] + _dot((qs[i] * qw_b).astype(BF16), sb.astype(BF16))
        sb = gc_b * sb + jnp.where(bd, _dot((ks[i] * kw_b).T.astype(BF16), vs[i]), 0.0)
    o = jnp.concatenate(outs, axis=0) if n > 1 else outs[0]
    lo_s = _lane_lo((seq, LANES))
    inv_d = 1.0 / HEAD_DIM
    mu = jnp.where(lo_s, jnp.sum(jnp.where(lo_s, o, 0.0), axis=-1, keepdims=True),
                   jnp.sum(jnp.where(lo_s, 0.0, o), axis=-1, keepdims=True)) * inv_d
    d = o - mu
    d2 = d * d
    var = jnp.where(lo_s, jnp.sum(jnp.where(lo_s, d2, 0.0), axis=-1, keepdims=True),
                    jnp.sum(jnp.where(lo_s, 0.0, d2), axis=-1, keepdims=True)) * inv_d
    y = d * lax.rsqrt(var + EPS) * (g * jax.nn.sigmoid(g))
    return y, sf, sb


def _ctx_mixer_kernel(lgf_ref, lgb_ref, sink_ref, p_ref, o_in_ref, o_ref, sf_ref, sb_ref,
                      nk_ref, nv_ref, wk_ref, wv_ref):
    def col(blk):
        return p_ref[:, blk * LANES:(blk + 1) * LANES]

    for hp in range(N_RET_PAIRS):
        y, sf, sb = _ret_pair(col(CB_RQ + hp), col(CB_RK + hp), col(CB_RV + hp), col(CB_RG + hp),
                              (lgf_ref[2 * hp], lgf_ref[2 * hp + 1]),
                              (lgb_ref[2 * hp], lgb_ref[2 * hp + 1]), None, None, SEQ)
        o_ref[:, (OB_RET + hp) * LANES:(OB_RET + hp + 1) * LANES] = y.astype(o_ref.dtype)
        sf_ref[2 * hp] = sf[:HEAD_DIM, :HEAD_DIM]
        sf_ref[2 * hp + 1] = sf[HEAD_DIM:, HEAD_DIM:]
        sb_ref[2 * hp] = sb[:HEAD_DIM, :HEAD_DIM]
        sb_ref[2 * hp + 1] = sb[HEAD_DIM:, HEAD_DIM:]

    for hp in range(N_NA_PAIRS):
        k = col(CB_NK + hp)
        v = col(CB_NV + hp)
        o = _attn_pair(col(CB_NQ + hp) * SCALE, [k.astype(BF16)], [v.astype(BF16)], [None], None)
        o_ref[:, (OB_NA + hp) * LANES:(OB_NA + hp + 1) * LANES] = o.astype(o_ref.dtype)
        nk_ref[2 * hp] = k[:, :HEAD_DIM]
        nk_ref[2 * hp + 1] = k[:, HEAD_DIM:]
        nv_ref[2 * hp] = v[:, :HEAD_DIM]
        nv_ref[2 * hp + 1] = v[:, HEAD_DIM:]

    k = col(CB_WK)
    v = col(CB_WV)
    for kv in range(KV_WIN):
        wk_ref[kv] = k[:, kv * HEAD_DIM:(kv + 1) * HEAD_DIM]
        wv_ref[kv] = v[:, kv * HEAD_DIM:(kv + 1) * HEAD_DIM]
    lo = _lane_lo(k.shape)
    k_sw = pltpu.roll(k, HEAD_DIM, 1)
    v_sw = pltpu.roll(v, HEAD_DIM, 1)
    for hp in range(N_WIN_PAIRS):
        kv_lo = (2 * hp) // G_WIN
        kv_hi = (2 * hp + 1) // G_WIN
        kk = jnp.where(lo, k if kv_lo == 0 else k_sw, k if kv_hi == 1 else k_sw).astype(BF16)
        vv = jnp.where(lo, v if kv_lo == 0 else v_sw, v if kv_hi == 1 else v_sw).astype(BF16)
        o = _attn_pair(col(CB_WQ + hp) * SCALE, [kk], [vv], [None],
                       (sink_ref[2 * hp], sink_ref[2 * hp + 1]))
        o_ref[:, (OB_WIN + hp) * LANES:(OB_WIN + hp + 1) * LANES] = o.astype(o_ref.dtype)


def _smem_spec():
    return pl.BlockSpec(memory_space=pltpu.SMEM)


def _ctx_mixer(proj, lgf, lgb, sink, n_ctx):
    t = proj.shape[0]
    o_init = jnp.zeros((t, W_MIX), BF16)
    out_shape = (
        jax.ShapeDtypeStruct((t, W_MIX), BF16),
        jax.ShapeDtypeStruct((n_ctx, H_RET, HEAD_DIM, HEAD_DIM), F32),
        jax.ShapeDtypeStruct((n_ctx, H_RET, HEAD_DIM, HEAD_DIM), F32),
        jax.ShapeDtypeStruct((n_ctx, H_NA, SEQ, HEAD_DIM), F32),
        jax.ShapeDtypeStruct((n_ctx, H_NA, SEQ, HEAD_DIM), F32),
        jax.ShapeDtypeStruct((n_ctx, KV_WIN, SEQ, HEAD_DIM), F32),
        jax.ShapeDtypeStruct((n_ctx, KV_WIN, SEQ, HEAD_DIM), F32),
    )
    st = lambda h, a, b: pl.BlockSpec((None, h, a, b), lambda i: (i, 0, 0, 0))
    return pl.pallas_call(
        _ctx_mixer_kernel, out_shape=out_shape, grid=(n_ctx,),
        in_specs=[_smem_spec(), _smem_spec(), _smem_spec(),
                  pl.BlockSpec((SEQ, W_IN), lambda i: (i, 0)), pl.BlockSpec(memory_space=pl.ANY)],
        out_specs=(pl.BlockSpec((SEQ, W_MIX), lambda i: (i, 0)),
                   st(H_RET, HEAD_DIM, HEAD_DIM), st(H_RET, HEAD_DIM, HEAD_DIM),
                   st(H_NA, SEQ, HEAD_DIM), st(H_NA, SEQ, HEAD_DIM),
                   st(KV_WIN, SEQ, HEAD_DIM), st(KV_WIN, SEQ, HEAD_DIM)),
        input_output_aliases={4: 0},
        compiler_params=_cparams(("parallel",)), name="ctx_mixer",
    )(lgf, lgb, sink, proj, o_init)


def _ret_lat_kernel(lgf_ref, lgb_ref, q_ref, k_ref, v_ref, g_ref, s0f_ref, s0b_ref, o_in_ref, o_ref):
    hp = pl.program_id(1)
    y, _, _ = _ret_pair(q_ref[...], k_ref[...], v_ref[...], g_ref[...],
                        (lgf_ref[2 * hp], lgf_ref[2 * hp + 1]), (lgb_ref[2 * hp], lgb_ref[2 * hp + 1]),
                        _block_diag(s0f_ref[0], s0f_ref[1]), _block_diag(s0b_ref[0], s0b_ref[1]), DEC_SEQ)
    o_ref[...] = y.astype(o_ref.dtype)


def _ret_latent(proj, o, lgf, lgb, s0f, s0b, l, n_lat):
    rb0 = (proj.shape[0] - n_lat * DEC_SEQ) // DEC_SEQ
    cb = lambda c0: pl.BlockSpec((DEC_SEQ, LANES), lambda b, hp: (rb0 + b, c0 + hp))
    st = pl.BlockSpec((None, None, 2, HEAD_DIM, HEAD_DIM), lambda b, hp: (b, l, hp, 0, 0))
    return pl.pallas_call(
        _ret_lat_kernel, out_shape=jax.ShapeDtypeStruct(o.shape, o.dtype),
        grid=(n_lat, N_RET_PAIRS),
        in_specs=[_smem_spec(), _smem_spec(), cb(CB_RQ), cb(CB_RK), cb(CB_RV), cb(CB_RG), st, st,
                  pl.BlockSpec(memory_space=pl.ANY)],
        out_specs=pl.BlockSpec((DEC_SEQ, LANES), lambda b, hp: (rb0 + b, OB_RET + hp)),
        input_output_aliases={8: 0},
        compiler_params=_cparams(("parallel", "parallel")), name="ret_latent",
    )(lgf, lgb, proj, proj, proj, proj, s0f, s0b, o)


def _na_lat_kernel(q_ref, k_ref, v_ref, kc_ref, vc_ref, bias_ref, o_in_ref, o_ref):
    kc = jnp.concatenate([kc_ref[0], kc_ref[1]], axis=1).astype(BF16)
    vc = jnp.concatenate([vc_ref[0], vc_ref[1]], axis=1).astype(BF16)

    def body(r, carry):
        r0 = jnp.clip(r - NA_ROWS // 2, 0, LAT_ROWS - NA_ROWS)
        q0 = pl.multiple_of(r * GRID_W, GRID_W)
        k0 = pl.multiple_of(r0 * GRID_W, GRID_W)
        q = q_ref[pl.ds(q0, GRID_W), :] * SCALE
        kw = k_ref[pl.ds(k0, NA_WIN_KEYS), :].astype(BF16)
        vw = v_ref[pl.ds(k0, NA_WIN_KEYS), :].astype(BF16)
        dr0 = r0 - r + (NA_ROWS - 1)
        bias = [jnp.concatenate([bias_ref[h, dr0 + 2 * m] for m in range(NA_ROWS // 2)], axis=1)
                for h in (0, 1)]
        o = _attn_pair(q, [kw, kc], [vw, vc], [bias, None], None)
        o_ref[pl.ds(q0, GRID_W), :] = o.astype(o_ref.dtype)
        return carry

    lax.fori_loop(0, LAT_ROWS, body, 0)


N_ROW_OFFS = 2 * NA_ROWS - 1


def _na_bias_table(rpb):
    cols = np.arange(GRID_W)
    c0 = np.clip(cols - NA_COLS // 2, 0, GRID_W - NA_COLS)
    col_ok = (cols[None, :] >= c0[:, None]) & (cols[None, :] < c0[:, None] + NA_COLS)
    dc = np.clip(cols[None, :] - cols[:, None], -(NA_COLS - 1), NA_COLS - 1) + (NA_COLS - 1)
    by_col = jnp.where(col_ok[None, None], rpb.astype(F32)[:, :, dc], NEG)
    return jnp.concatenate([by_col[:, :-1], by_col[:, 1:]], axis=-1)


def _na_latent(proj, o, kc, vc, bias, l, n_lat):
    rb0 = (proj.shape[0] - n_lat * DEC_SEQ) // DEC_SEQ
    cb = lambda c0: pl.BlockSpec((DEC_SEQ, LANES), lambda b, hp: (rb0 + b, c0 + hp))
    cache = pl.BlockSpec((None, None, 2, PAST_LEN, HEAD_DIM), lambda b, hp: (b, l, hp, 0, 0))
    return pl.pallas_call(
        _na_lat_kernel, out_shape=jax.ShapeDtypeStruct(o.shape, o.dtype),
        grid=(n_lat, N_NA_PAIRS),
        in_specs=[cb(CB_NQ), cb(CB_NK), cb(CB_NV), cache, cache,
                  pl.BlockSpec((2, N_ROW_OFFS - 1, GRID_W, LANES), lambda b, hp: (hp, 0, 0, 0)),
                  pl.BlockSpec(memory_space=pl.ANY)],
        out_specs=pl.BlockSpec((DEC_SEQ, LANES), lambda b, hp: (rb0 + b, OB_NA + hp)),
        input_output_aliases={6: 0},
        compiler_params=_cparams(("parallel", "parallel")), name="na_latent",
    )(proj, proj, proj, kc, vc, bias, o)


def _rope_tables():
    t = jnp.arange(DEC_SEQ)
    d = np.arange(LANES) % HEAD_DIM
    quarter = HEAD_DIM // 4
    inv = ROPE_BASE ** (-jnp.arange(quarter, dtype=F32) / quarter)
    pos = jnp.where(jnp.asarray(d < HEAD_DIM // 2)[None, :], (t // GRID_W)[:, None], (t % GRID_W)[:, None])
    ang = pos.astype(F32) * inv[d % quarter][None, :]
    sign = jnp.asarray(np.where((d & quarter) == 0, -1.0, 1.0), F32)
    return jnp.cos(ang), jnp.sin(ang) * sign[None, :]


def _win_lat_kernel(sink_ref, q_ref, k_ref, v_ref, kc_ref, vc_ref, cos_ref, sin_ref, o_in_ref, o_ref,
                    qs_ref, kp_ref, vp_ref):
    hp = pl.program_id(1)
    quarter = HEAD_DIM // 4
    lane = lax.broadcasted_iota(jnp.int32, (DEC_SEQ, LANES), 1)
    first = (lane & quarter) == 0
    lo = lane < HEAD_DIM
    cos = cos_ref[...]
    sin = sin_ref[...]

    def rope(x):
        sw = jnp.where(first, pltpu.roll(x, LANES - quarter, 1), pltpu.roll(x, quarter, 1))
        return x * cos + sw * sin

    qs_ref[...] = rope(q_ref[...]) * SCALE
    k = rope(k_ref[...])
    v = v_ref[...]
    lo_orig = (2 * hp) // G_WIN == 0
    hi_orig = (2 * hp + 1) // G_WIN == 1
    orig = jnp.where(lo, lo_orig.astype(jnp.int32), hi_orig.astype(jnp.int32)) == 1
    zeros = jnp.zeros((WIN_BLOCK, LANES), BF16)
    kp_ref[:WIN_BLOCK] = zeros
    kp_ref[WIN_BLOCK + DEC_SEQ:] = zeros
    vp_ref[:WIN_BLOCK] = zeros
    vp_ref[WIN_BLOCK + DEC_SEQ:] = zeros
    kp_ref[WIN_BLOCK:WIN_BLOCK + DEC_SEQ] = jnp.where(orig, k, pltpu.roll(k, HEAD_DIM, 1)).astype(BF16)
    vp_ref[WIN_BLOCK:WIN_BLOCK + DEC_SEQ] = jnp.where(orig, v, pltpu.roll(v, HEAD_DIM, 1)).astype(BF16)
    kc = jnp.concatenate([jnp.where(lo_orig, kc_ref[0], kc_ref[1]),
                          jnp.where(hi_orig, kc_ref[1], kc_ref[0])], axis=1).astype(BF16)
    vc = jnp.concatenate([jnp.where(lo_orig, vc_ref[0], vc_ref[1]),
                          jnp.where(hi_orig, vc_ref[1], vc_ref[0])], axis=1).astype(BF16)
    sinks = (sink_ref[2 * hp], sink_ref[2 * hp + 1])
    n_band = 3 * WIN_BLOCK
    qi = lax.broadcasted_iota(jnp.int32, (WIN_BLOCK, n_band), 0)
    kj = lax.broadcasted_iota(jnp.int32, (WIN_BLOCK, n_band), 1)
    near = jnp.abs(qi + WIN_BLOCK - kj) <= WIN_HALF

    def body(n, carry):
        q0 = pl.multiple_of(n * WIN_BLOCK, WIN_BLOCK)
        kpos = (n - 1) * WIN_BLOCK + kj
        ok = jnp.logical_and(near, jnp.logical_and(kpos >= 0, kpos < DEC_SEQ))
        band = jnp.where(ok, 0.0, NEG)
        o = _attn_pair(qs_ref[pl.ds(q0, WIN_BLOCK), :],
                       [kp_ref[pl.ds(q0, n_band), :], kc], [vp_ref[pl.ds(q0, n_band), :], vc],
                       [(band, band), None], sinks)
        o_ref[pl.ds(q0, WIN_BLOCK), :] = o.astype(o_ref.dtype)
        return carry

    lax.fori_loop(0, DEC_SEQ // WIN_BLOCK, body, 0)


def _win_latent(proj, o, kc, vc, sink, cos, sin, l, n_lat):
    rb0 = (proj.shape[0] - n_lat * DEC_SEQ) // DEC_SEQ
    cache = pl.BlockSpec((None, None, KV_WIN, PAST_LEN, HEAD_DIM), lambda b, hp: (b, l, 0, 0, 0))
    tbl = pl.BlockSpec((DEC_SEQ, LANES), lambda b, hp: (0, 0))
    return pl.pallas_call(
        _win_lat_kernel, out_shape=jax.ShapeDtypeStruct(o.shape, o.dtype),
        grid=(n_lat, N_WIN_PAIRS),
        in_specs=[_smem_spec(),
                  pl.BlockSpec((DEC_SEQ, LANES), lambda b, hp: (rb0 + b, CB_WQ + hp)),
                  pl.BlockSpec((DEC_SEQ, LANES), lambda b, hp: (rb0 + b, CB_WK)),
                  pl.BlockSpec((DEC_SEQ, LANES), lambda b, hp: (rb0 + b, CB_WV)),
                  cache, cache, tbl, tbl, pl.BlockSpec(memory_space=pl.ANY)],
        out_specs=pl.BlockSpec((DEC_SEQ, LANES), lambda b, hp: (rb0 + b, OB_WIN + hp)),
        scratch_shapes=[pltpu.VMEM((DEC_SEQ, LANES), F32),
                        pltpu.VMEM((DEC_SEQ + 2 * WIN_BLOCK, LANES), BF16),
                        pltpu.VMEM((DEC_SEQ + 2 * WIN_BLOCK, LANES), BF16)],
        input_output_aliases={8: 0},
        compiler_params=_cparams(("parallel", "parallel")), name="win_latent",
    )(sink, proj, proj, proj, kc, vc, cos, sin, o)


def _mixers(proj, l, n_ctx, n_lat, state_ret_fwd, state_ret_bwd, cache_na_k, cache_na_v,
            cache_win_k, cache_win_v, ret_decay_fwd, ret_decay_bwd, na_rpb, win_sink):
    lgf = jax.nn.log_sigmoid(ret_decay_fwd[l].astype(F32))
    lgb = jax.nn.log_sigmoid(ret_decay_bwd[l].astype(F32))
    sink = win_sink[l].astype(F32)
    o, sf, sb, nk, nv, wk, wv = _ctx_mixer(proj, lgf, lgb, sink, n_ctx)
    o = _ret_latent(proj, o, lgf, lgb, state_ret_fwd, state_ret_bwd, l, n_lat)
    o = _na_latent(proj, o, cache_na_k, cache_na_v, _na_bias_table(na_rpb[l]), l, n_lat)
    cos, sin = _rope_tables()
    o = _win_latent(proj, o, cache_win_k, cache_win_v, sink, cos, sin, l, n_lat)
    return o, (sf, sb, nk, nv, wk, wv)


def kernel(x_prompt, x_sample, c, state_ret_fwd, state_ret_bwd, cache_na_k, cache_na_v, cache_win_k, cache_win_v, c_ctx, norm1_g, norm2_g, ada_w, ada_b, w_in, w_out, ret_decay_fwd, ret_decay_bwd, na_rpb, win_sink, ffn_w_in, ffn_w_out, moe_router, moe_w_in, moe_w_out, final_norm_g):
    x = jnp.concatenate([x_prompt.reshape(T_CTX, D_MODEL), x_sample.reshape(T_LAT, D_MODEL)], axis=0)
    cond = jnp.concatenate([c_ctx[None, :], c, jnp.zeros((N_GROUPS - 1 - DEC_BATCH, D_MODEL), F32)], axis=0)
    ctx_states = []
    for l in range(DEPTH):
        mod = _adaln(cond, ada_w, ada_b, l)
        h = _norm_mod(x, norm1_g, mod, l, 0, 1)
        proj = _matmul(h, w_in, l)
        o, ctx = _mixers(proj, l, BATCH, DEC_BATCH, state_ret_fwd, state_ret_bwd, cache_na_k, cache_na_v,
                         cache_win_k, cache_win_v, ret_decay_fwd, ret_decay_bwd, na_rpb, win_sink)
        ctx_states.append(ctx)
        x = _matmul(o, w_out, l, resid=x, mod=mod, k_gate=2)
        i = l // 2
        if l % 2 == 0:
            h3 = _norm_mod(x, norm2_g, mod, l, 3, 4, rows3=True)
            x = _dense_ffn(x, h3, ffn_w_in, ffn_w_out, i, mod, 5)
        else:
            router = jnp.pad(moe_router[i], ((0, 0), (0, LANES - N_EXPERTS)))
            h3, route, counts = _norm_mod(x, norm2_g, mod, l, 3, 4, router=router)
            x = _moe(x, h3, route, counts, moe_w_in[i], moe_w_out[i], mod, 5)
    y = _final_norm(x, final_norm_g)
    y_prompt = y[:T_CTX].reshape(BATCH, SEQ, D_MODEL)
    y_sample = y[T_CTX:].reshape(DEC_BATCH, DEC_SEQ, D_MODEL)
    dt = x_prompt.dtype
    outs = [jnp.stack([s[k] for s in ctx_states], axis=1).astype(dt) for k in range(6)]
    return (y_prompt, y_sample, *outs)
```

```python
import functools

import jax
import jax.numpy as jnp
from jax import lax
import numpy as np
from jax.experimental import pallas as pl
from jax.experimental.pallas import tpu as pltpu

D_MODEL = 2048
BATCH = 32
SEQ = 256
DEPTH = 2
DEC_BATCH = 4
DEC_SEQ = 1024
PAST_LEN = 256

GRID_W = 64
HEAD_DIM = 64
H_RET = 12
H_NA = 10
H_WIN = 10
KV_WIN = 2
G_WIN = H_WIN // KV_WIN
W_RET = H_RET * HEAD_DIM
W_NA = H_NA * HEAD_DIM
W_WIN = H_WIN * HEAD_DIM
W_MIX = W_RET + W_NA + W_WIN
W_IN = 4 * W_RET + 3 * W_NA + W_WIN + 2 * KV_WIN * HEAD_DIM
RET_CHUNK = 128
NA_ROWS = 8
NA_COLS = 16
WIN_HALF = 128
WIN_BLOCK = 128
ROPE_BASE = 10000.0
D_FF = 5632
N_EXPERTS = 8
TOP_K = 2
D_FF_EXPERT = 7168
EPS = 1e-6
NEG = -1e30
SCALE = HEAD_DIM ** -0.5

T_CTX = BATCH * SEQ
T_LAT = DEC_BATCH * DEC_SEQ
T_ALL = T_CTX + T_LAT
N_GROUPS = 8
LANES = 128

F32 = jnp.float32
BF16 = jnp.bfloat16

VMEM_LIMIT = 56 * 1024 * 1024


def _group_of_tile(i, tm):
    return jnp.maximum((i * tm - T_CTX) // DEC_SEQ + 1, 0)


def _cparams(sem):
    return pltpu.CompilerParams(dimension_semantics=sem, vmem_limit_bytes=VMEM_LIMIT)


def _adaln_kernel(c_ref, w_ref, b_ref, o_ref):
    c = c_ref[...]
    s = (c * jax.nn.sigmoid(c)).astype(BF16)
    o_ref[...] = jnp.dot(s, w_ref[...].astype(BF16), preferred_element_type=F32) + b_ref[...]


def _adaln(cond, ada_w, ada_b, l):
    tn = 1024
    n = 6 * D_MODEL
    out = pl.pallas_call(
        _adaln_kernel,
        out_shape=jax.ShapeDtypeStruct((N_GROUPS, n), F32),
        grid=(n // tn,),
        in_specs=[
            pl.BlockSpec((N_GROUPS, D_MODEL), lambda j: (0, 0)),
            pl.BlockSpec((None, D_MODEL, tn), lambda j: (l, 0, j)),
            pl.BlockSpec((None, 1, tn), lambda j: (l, 0, j)),
        ],
        out_specs=pl.BlockSpec((N_GROUPS, tn), lambda j: (0, j)),
        compiler_params=_cparams(("arbitrary",)),
        name="adaln",
    )(cond, ada_w, ada_b.reshape(DEPTH, 1, n))
    return out.reshape(N_GROUPS, 6, 1, D_MODEL)


N_SUB = D_MODEL // LANES


def _store_token_major(ref, val):
    rows = val.shape[0]
    for s in range(N_SUB):
        ref[pl.ds(s, rows, stride=N_SUB), :] = val[:, s * LANES:(s + 1) * LANES].astype(ref.dtype)


def _load_token_major(ref, rows, dtype=F32):
    return jnp.concatenate([ref[pl.ds(s, rows, stride=N_SUB), :].astype(dtype) for s in range(N_SUB)],
                           axis=1)


def _token_slab(ref, t):
    return ref.at[pl.ds(pl.multiple_of(t * N_SUB, N_SUB), N_SUB)]


def _norm_mod_body(x_ref, g_ref, sh_ref, sc_ref):
    x = x_ref[...]
    ms = jnp.mean(x * x, axis=-1, keepdims=True)
    y = x * lax.rsqrt(ms + EPS) * g_ref[...]
    return y * (1.0 + sc_ref[...]) + sh_ref[...]


def _norm_mod_kernel(x_ref, g_ref, sh_ref, sc_ref, h_ref):
    h_ref[...] = _norm_mod_body(x_ref, g_ref, sh_ref, sc_ref).astype(h_ref.dtype)


R_E0, R_E1, R_RANK0, R_RANK1, R_W0, R_W1 = range(6)


def _norm_mod_router_kernel(x_ref, g_ref, sh_ref, sc_ref, r_ref, h2_ref, route_ref, cnt_ref, run_ref):
    i = pl.program_id(0)

    @pl.when(i == 0)
    def _():
        run_ref[...] = jnp.zeros_like(run_ref)

    h = _norm_mod_body(x_ref, g_ref, sh_ref, sc_ref)
    _store_token_major(h2_ref, h)
    tm = h.shape[0]
    lg = jnp.dot(h, r_ref[...], preferred_element_type=F32, precision=lax.Precision.HIGHEST)
    lane = lax.broadcasted_iota(jnp.int32, (tm, LANES), 1)
    lane_f = lane.astype(F32)
    lg = jnp.where(lane < N_EXPERTS, lg, -jnp.inf)
    v0 = jnp.max(lg, axis=-1, keepdims=True)
    e0 = jnp.min(jnp.where(lg == v0, lane_f, float(LANES)), axis=-1, keepdims=True)
    lg1 = jnp.where(lane_f == e0, -jnp.inf, lg)
    v1 = jnp.max(lg1, axis=-1, keepdims=True)
    e1 = jnp.min(jnp.where(lg1 == v1, lane_f, float(LANES)), axis=-1, keepdims=True)
    ex = jnp.exp(v1 - v0)
    w0 = 1.0 / (1.0 + ex)
    w1 = ex / (1.0 + ex)
    oh0 = jnp.where(lane_f == e0, 1.0, 0.0)
    oh1 = jnp.where(lane_f == e1, 1.0, 0.0)
    oh = oh0 + oh1
    earlier = (lax.broadcasted_iota(jnp.int32, (tm, tm), 0)
               > lax.broadcasted_iota(jnp.int32, (tm, tm), 1))
    before = jnp.dot(jnp.where(earlier, 1.0, 0.0).astype(BF16), oh.astype(BF16),
                     preferred_element_type=F32) + run_ref[0:1, :]
    rank0 = jnp.sum(oh0 * before, axis=-1, keepdims=True)
    rank1 = jnp.sum(oh1 * before, axis=-1, keepdims=True)
    rec = jnp.zeros((tm, LANES), F32)
    for k, val in ((R_E0, e0), (R_E1, e1), (R_RANK0, rank0), (R_RANK1, rank1), (R_W0, w0), (R_W1, w1)):
        rec = jnp.where(lane == k, val, rec)
    route_ref[...] = rec
    run_ref[0:1, :] = run_ref[0:1, :] + jnp.sum(oh, axis=0, keepdims=True)
    cnt_ref[...] = run_ref[...]


def _norm_mod(x, g, mod, l, k_shift, k_scale, *, router=None):
    tm = 512
    g3 = g.reshape(DEPTH, 1, D_MODEL)
    in_specs = [
        pl.BlockSpec((tm, D_MODEL), lambda i: (i, 0)),
        pl.BlockSpec((None, 1, D_MODEL), lambda i: (l, 0, 0)),
        pl.BlockSpec((None, None, 1, D_MODEL), lambda i: (_group_of_tile(i, tm), k_shift, 0, 0)),
        pl.BlockSpec((None, None, 1, D_MODEL), lambda i: (_group_of_tile(i, tm), k_scale, 0, 0)),
    ]
    if router is not None:
        in_specs.append(pl.BlockSpec((D_MODEL, LANES), lambda i: (0, 0)))
        return pl.pallas_call(
            _norm_mod_router_kernel,
            out_shape=(jax.ShapeDtypeStruct((T_ALL * N_SUB, LANES), F32),
                       jax.ShapeDtypeStruct((T_ALL, LANES), F32),
                       jax.ShapeDtypeStruct((8, LANES), F32)),
            grid=(T_ALL // tm,),
            in_specs=in_specs,
            out_specs=(pl.BlockSpec((tm * N_SUB, LANES), lambda i: (i, 0)),
                       pl.BlockSpec((tm, LANES), lambda i: (i, 0)),
                       pl.BlockSpec((8, LANES), lambda i: (0, 0))),
            scratch_shapes=[pltpu.VMEM((8, LANES), F32)],
            compiler_params=_cparams(("arbitrary",)), name="norm_mod_router",
        )(x, g3, mod, mod, router)
    return pl.pallas_call(
        _norm_mod_kernel, out_shape=jax.ShapeDtypeStruct((T_ALL, D_MODEL), BF16), grid=(T_ALL // tm,),
        in_specs=in_specs, out_specs=pl.BlockSpec((tm, D_MODEL), lambda i: (i, 0)),
        compiler_params=_cparams(("parallel",)), name="norm_mod",
    )(x, g3, mod, mod)


def _mm_kernel(a_ref, w_ref, o_ref):
    o_ref[...] = jnp.dot(a_ref[...].astype(BF16), w_ref[...].astype(BF16),
                         preferred_element_type=F32)


def _mm_res_kernel(a_ref, w_ref, x_ref, gate_ref, o_ref):
    acc = jnp.dot(a_ref[...].astype(BF16), w_ref[...].astype(BF16),
                  preferred_element_type=F32)
    o_ref[...] = x_ref[...] + gate_ref[...] * acc


def _matmul(a, w, l, *, resid=None, mod=None, k_gate=None):
    tm, tn = 1024, 512
    t, k = a.shape
    n = w.shape[-1]
    grid = (t // tm, pl.cdiv(n, tn))
    in_specs = [
        pl.BlockSpec((tm, k), lambda i, j: (i, 0)),
        pl.BlockSpec((None, k, tn), lambda i, j: (l, 0, j)),
    ]
    args = [a, w]
    kern = _mm_kernel
    if resid is not None:
        in_specs += [
            pl.BlockSpec((tm, tn), lambda i, j: (i, j)),
            pl.BlockSpec((None, None, 1, tn), lambda i, j: (_group_of_tile(i, tm), k_gate, 0, j)),
        ]
        args += [resid, mod]
        kern = _mm_res_kernel
    return pl.pallas_call(
        kern, out_shape=jax.ShapeDtypeStruct((t, n), F32), grid=grid,
        in_specs=in_specs, out_specs=pl.BlockSpec((tm, tn), lambda i, j: (i, j)),
        compiler_params=_cparams(("parallel", "arbitrary")), name="proj",
    )(*args)


FFN_TF = 256


def _ffn_step(h, wa_ref, wg_ref, wo_ref, acc_ref, j):
    a = jnp.dot(h, wa_ref[...].astype(BF16), preferred_element_type=F32)
    g = jnp.dot(h, wg_ref[...].astype(BF16), preferred_element_type=F32)
    act = (a * jax.nn.sigmoid(a) * g).astype(BF16)
    part = jnp.dot(act, wo_ref[...].astype(BF16), preferred_element_type=F32)

    @pl.when(j == 0)
    def _():
        acc_ref[...] = part

    @pl.when(j > 0)
    def _():
        acc_ref[...] += part


def _ffn_dense_kernel(h_ref, wa_ref, wg_ref, wo_ref, o_ref):
    _ffn_step(h_ref[...], wa_ref, wg_ref, wo_ref, o_ref, pl.program_id(1))


def _ffn_dense(h, w_in, w_out, e, d_ff, tm):
    tf = FFN_TF
    n_f = d_ff // tf
    return pl.pallas_call(
        _ffn_dense_kernel,
        out_shape=jax.ShapeDtypeStruct((h.shape[0], D_MODEL), F32),
        grid=(h.shape[0] // tm, n_f),
        in_specs=[
            pl.BlockSpec((tm, D_MODEL), lambda i, j: (i, 0)),
            pl.BlockSpec((None, D_MODEL, tf), lambda i, j: (e, 0, j)),
            pl.BlockSpec((None, D_MODEL, tf), lambda i, j: (e, 0, j + n_f)),
            pl.BlockSpec((None, tf, D_MODEL), lambda i, j: (e, j, 0)),
        ],
        out_specs=pl.BlockSpec((tm, D_MODEL), lambda i, j: (i, 0)),
        compiler_params=_cparams(("parallel", "arbitrary")), name="ffn_dense",
    )(h, w_in, w_in, w_out)


def _ffn_expert_kernel(te_ref, nt_ref, tok_ref, h2_ref, wa_ref, wg_ref, wo_ref, o_ref,
                       gbuf_ref, hb_ref, acc_ref, sem, *, n_f, tm):
    i = pl.program_id(0)
    j = pl.program_id(1)
    valid = i < nt_ref[0]

    def start_gather(tile):
        def body(r, c):
            pltpu.make_async_copy(_token_slab(h2_ref, tok_ref[tile * tm + r]),
                                  _token_slab(gbuf_ref, r), sem).start()
            return c
        lax.fori_loop(0, tm, body, 0)

    @pl.when(jnp.logical_and(valid, j == 0))
    def _():
        @pl.when(i == 0)
        def _():
            start_gather(0)

        pltpu.make_async_copy(h2_ref.at[pl.ds(0, tm * N_SUB)], gbuf_ref, sem).wait()
        for s in range(N_SUB):
            hb_ref[:, s * LANES:(s + 1) * LANES] = gbuf_ref[pl.ds(s, tm, stride=N_SUB), :].astype(BF16)

        @pl.when(i + 1 < nt_ref[0])
        def _():
            start_gather(i + 1)

    @pl.when(valid)
    def _():
        _ffn_step(hb_ref[...], wa_ref, wg_ref, wo_ref, acc_ref, j)

        @pl.when(j == n_f - 1)
        def _():
            _store_token_major(o_ref, acc_ref[...])

    @pl.when(jnp.logical_and(jnp.logical_not(valid), j == 0))
    def _():
        o_ref[...] = jnp.zeros_like(o_ref)


def _ffn_expert(h2, row_token, te, nt, w_in, w_out, d_ff, tm):
    tf = FFN_TF
    r = row_token.shape[0]
    n_f = d_ff // tf

    def jj(i, j, nt_ref):
        return jnp.where(i < nt_ref[0], j, n_f - 1)

    grid_spec = pltpu.PrefetchScalarGridSpec(
        num_scalar_prefetch=3,
        grid=(r // tm, n_f),
        in_specs=[
            pl.BlockSpec(memory_space=pl.ANY),
            pl.BlockSpec((None, D_MODEL, tf), lambda i, j, te_ref, nt_ref, tok_ref: (te_ref[i], 0, jj(i, j, nt_ref))),
            pl.BlockSpec((None, D_MODEL, tf), lambda i, j, te_ref, nt_ref, tok_ref: (te_ref[i], 0, jj(i, j, nt_ref) + n_f)),
            pl.BlockSpec((None, tf, D_MODEL), lambda i, j, te_ref, nt_ref, tok_ref: (te_ref[i], jj(i, j, nt_ref), 0)),
        ],
        out_specs=pl.BlockSpec((tm * N_SUB, LANES), lambda i, j, te_ref, nt_ref, tok_ref: (i, 0),
                               pipeline_mode=pl.Buffered(1)),
        scratch_shapes=[pltpu.VMEM((tm * N_SUB, LANES), F32), pltpu.VMEM((tm, D_MODEL), BF16),
                        pltpu.VMEM((tm, D_MODEL), F32), pltpu.SemaphoreType.DMA],
    )
    return pl.pallas_call(
        functools.partial(_ffn_expert_kernel, n_f=n_f, tm=tm),
        out_shape=jax.ShapeDtypeStruct((r * N_SUB, LANES), F32),
        grid_spec=grid_spec,
        compiler_params=_cparams(("arbitrary", "arbitrary")), name="ffn_expert",
    )(te, nt, row_token, h2, w_in, w_in, w_out)


def _resid_kernel(x_ref, y_ref, gate_ref, o_ref):
    o_ref[...] = x_ref[...] + gate_ref[...] * y_ref[...]


def _resid(x, y, mod, k_gate):
    tm = 512
    return pl.pallas_call(
        _resid_kernel,
        out_shape=jax.ShapeDtypeStruct((T_ALL, D_MODEL), F32), grid=(T_ALL // tm,),
        in_specs=[
            pl.BlockSpec((tm, D_MODEL), lambda i: (i, 0)),
            pl.BlockSpec((tm, D_MODEL), lambda i: (i, 0)),
            pl.BlockSpec((None, None, 1, D_MODEL), lambda i: (_group_of_tile(i, tm), k_gate, 0, 0)),
        ],
        out_specs=pl.BlockSpec((tm, D_MODEL), lambda i: (i, 0)),
        compiler_params=_cparams(("parallel",)), name="resid",
    )(x, y, mod)


COMBINE_TM = 256


def _combine_kernel(eid_ref, rank_ref, first_row_ref, x_ref, route_ref, gate_ref, ys2_ref, o_ref,
                    buf_ref, sem):
    i = pl.program_id(0)
    n = pl.num_programs(0)
    tm = COMBINE_TM

    def start_tile(tile, slot):
        for k in range(TOP_K):
            def start(t, c):
                a = (tile * tm + t) * TOP_K + k
                row = first_row_ref[eid_ref[a]] + rank_ref[a]
                pltpu.make_async_copy(_token_slab(ys2_ref, row), _token_slab(buf_ref.at[slot, k], t),
                                      sem.at[slot]).start()
                return c
            lax.fori_loop(0, tm, start, 0)

    @pl.when(i == 0)
    def _():
        start_tile(0, 0)

    @pl.when(i + 1 < n)
    def _():
        start_tile(i + 1, (i + 1) % 2)

    slot = i % 2
    for k in range(TOP_K):
        pltpu.make_async_copy(ys2_ref.at[pl.ds(0, tm * N_SUB)], buf_ref.at[slot, k], sem.at[slot]).wait()

    rec = route_ref[...]
    w0 = rec[:, R_W0:R_W0 + 1]
    w1 = rec[:, R_W1:R_W1 + 1]
    y = (w0 * _load_token_major(buf_ref.at[slot, 0], tm) + w1 * _load_token_major(buf_ref.at[slot, 1], tm))
    o_ref[...] = x_ref[...] + gate_ref[...] * y


def _combine(x, ys2, route, eid, rank, first_row, mod, k_gate):
    tm = COMBINE_TM
    grid_spec = pltpu.PrefetchScalarGridSpec(
        num_scalar_prefetch=3,
        grid=(T_ALL // tm,),
        in_specs=[
            pl.BlockSpec((tm, D_MODEL), lambda i, *_: (i, 0)),
            pl.BlockSpec((tm, LANES), lambda i, *_: (i, 0)),
            pl.BlockSpec((None, None, 1, D_MODEL), lambda i, *_: (_group_of_tile(i, tm), k_gate, 0, 0)),
            pl.BlockSpec(memory_space=pl.ANY),
        ],
        out_specs=pl.BlockSpec((tm, D_MODEL), lambda i, *_: (i, 0)),
        scratch_shapes=[pltpu.VMEM((2, TOP_K, tm * N_SUB, LANES), F32), pltpu.SemaphoreType.DMA((2,))],
    )
    return pl.pallas_call(
        _combine_kernel,
        out_shape=jax.ShapeDtypeStruct((T_ALL, D_MODEL), F32),
        grid_spec=grid_spec,
        compiler_params=_cparams(("arbitrary",)), name="moe_combine",
    )(eid, rank, first_row, x, route, mod, ys2)


def _final_norm_kernel(x_ref, g_ref, o_ref):
    x = x_ref[...]
    ms = jnp.mean(x * x, axis=-1, keepdims=True)
    o_ref[...] = x * lax.rsqrt(ms + EPS) * g_ref[...]


def _final_norm(x, g):
    tm = 512
    return pl.pallas_call(
        _final_norm_kernel,
        out_shape=jax.ShapeDtypeStruct((T_ALL, D_MODEL), F32), grid=(T_ALL // tm,),
        in_specs=[pl.BlockSpec((tm, D_MODEL), lambda i: (i, 0)),
                  pl.BlockSpec((1, D_MODEL), lambda i: (0, 0))],
        out_specs=pl.BlockSpec((tm, D_MODEL), lambda i: (i, 0)),
        compiler_params=_cparams(("parallel",)), name="final_norm",
    )(x, g.reshape(1, D_MODEL))


FFN_TM = 1024
MOE_TM = 1024
MOE_TILES = (TOP_K * T_ALL) // MOE_TM + N_EXPERTS


def _moe(x, h2, route, counts, w_in, w_out, mod, k_gate):
    tm = MOE_TM
    eid = route[:, R_E0:R_E1 + 1].astype(jnp.int32).reshape(-1)
    rank = route[:, R_RANK0:R_RANK1 + 1].astype(jnp.int32).reshape(-1)
    counts = counts[0, :N_EXPERTS].astype(jnp.int32)
    ptiles = (counts + tm - 1) // tm
    tile_end = jnp.cumsum(ptiles)
    first_row = (tile_end - ptiles) * tm
    n_tiles = tile_end[-1]
    tile_ids = jnp.arange(MOE_TILES, dtype=jnp.int32)
    te = jnp.sum((tile_ids[:, None] >= tile_end[None, :]).astype(jnp.int32), axis=1)
    te = jnp.minimum(te, N_EXPERTS - 1)
    te_last = jnp.sum(jnp.where(tile_ids == n_tiles - 1, te, 0))
    te = jnp.where(tile_ids < n_tiles, te, te_last)
    experts = jnp.arange(N_EXPERTS, dtype=jnp.int32)
    pos = jnp.sum(jnp.where(eid[:, None] == experts[None, :], first_row[None, :], 0), axis=1) + rank
    tok = jnp.arange(TOP_K * T_ALL, dtype=jnp.int32) // TOP_K
    row_token = jnp.zeros((MOE_TILES * tm,), jnp.int32).at[pos].set(tok)
    ys2 = _ffn_expert(h2, row_token, te, n_tiles.reshape(1), w_in, w_out, D_FF_EXPERT, tm)
    return _combine(x, ys2, route, eid, rank, first_row, mod, k_gate)


CB_RQ, CB_RK, CB_RV, CB_RG = 0, 6, 12, 18
CB_NQ, CB_NK, CB_NV = 24, 29, 34
CB_WQ, CB_WK, CB_WV = 39, 44, 45
OB_RET, OB_NA, OB_WIN = 0, 6, 11
N_RET_PAIRS = H_RET // 2
N_NA_PAIRS = H_NA // 2
N_WIN_PAIRS = H_WIN // 2
NA_WIN_KEYS = NA_ROWS * GRID_W
LAT_ROWS = DEC_SEQ // GRID_W


def _lane_lo(shape):
    return lax.broadcasted_iota(jnp.int32, shape, len(shape) - 1) < HEAD_DIM


def _dot_nt(a, b):
    return lax.dot_general(a, b, (((1,), (1,)), ((), ())), preferred_element_type=F32)


def _dot(a, b):
    return jnp.dot(a, b, preferred_element_type=F32)


def _attn_pair(q, ks, vs, biases, sinks):
    lo = _lane_lo(q.shape)
    outs = []
    for half in (0, 1):
        qm = jnp.where(lo if half == 0 else jnp.logical_not(lo), q, 0.0).astype(BF16)
        ss = []
        for kb, bb in zip(ks, biases):
            s = _dot_nt(qm, kb)
            if bb is not None:
                s = s + bb[half]
            ss.append(s)
        m = jnp.max(ss[0], axis=-1, keepdims=True)
        for s in ss[1:]:
            m = jnp.maximum(m, jnp.max(s, axis=-1, keepdims=True))
        if sinks is not None:
            m = jnp.maximum(m, sinks[half])
        ps = [jnp.exp(s - m) for s in ss]
        den = jnp.sum(ps[0], axis=-1, keepdims=True)
        for p in ps[1:]:
            den = den + jnp.sum(p, axis=-1, keepdims=True)
        if sinks is not None:
            den = den + jnp.exp(sinks[half] - m)
        o = _dot(ps[0].astype(BF16), vs[0])
        for p, vb in zip(ps[1:], vs[1:]):
            o = o + _dot(p.astype(BF16), vb)
        outs.append(o / den)
    return jnp.where(lo, outs[0], outs[1])


def _block_diag(s0, s1):
    z = jnp.zeros_like(s0)
    return jnp.concatenate([jnp.concatenate([s0, z], axis=1), jnp.concatenate([z, s1], axis=1)], axis=0)


def _ret_pair(q, k, v, g, lgf, lgb, s0f, s0b, seq):
    c = RET_CHUNK
    n = seq // c
    lo1 = _lane_lo((1, LANES))
    lo = _lane_lo((c, LANES))
    lgf_v = jnp.where(lo1, lgf[0], lgf[1])
    lgb_v = jnp.where(lo1, lgb[0], lgb[1])
    pos = lax.broadcasted_iota(jnp.int32, (c, LANES), 0).astype(F32)
    qw_f = jnp.exp(lgf_v * (pos + 1.0))
    kw_f = jnp.exp(lgf_v * (c - 1.0 - pos))
    qw_b = jnp.exp(lgb_v * (c - pos))
    kw_b = jnp.exp(lgb_v * pos)
    gc_f = jnp.exp(lgf_v * float(c))
    gc_b = jnp.exp(lgb_v * float(c))
    diff = (lax.broadcasted_iota(jnp.int32, (c, c), 0)
            - lax.broadcasted_iota(jnp.int32, (c, c), 1)).astype(F32)
    decay = [jnp.where(diff >= 0, jnp.exp(lgf[h] * jnp.maximum(diff, 0.0)), 0.0)
             + jnp.where(diff <= 0, jnp.exp(lgb[h] * jnp.maximum(-diff, 0.0)), 0.0) for h in (0, 1)]
    bd = ((lax.broadcasted_iota(jnp.int32, (LANES, LANES), 0) < HEAD_DIM)
          == (lax.broadcasted_iota(jnp.int32, (LANES, LANES), 1) < HEAD_DIM))
    zero = jnp.zeros((LANES, LANES), F32)
    sf = zero if s0f is None else s0f
    sb = zero if s0b is None else s0b
    qs = [q[i * c:(i + 1) * c] for i in range(n)]
    ks = [k[i * c:(i + 1) * c] * SCALE for i in range(n)]
    vs = [v[i * c:(i + 1) * c].astype(BF16) for i in range(n)]
    outs = []
    for i in range(n):
        kb = ks[i].astype(BF16)
        p0 = (_dot_nt(jnp.where(lo, qs[i], 0.0).astype(BF16), kb) * decay[0]).astype(BF16)
        p1 = (_dot_nt(jnp.where(lo, 0.0, qs[i]).astype(BF16), kb) * decay[1]).astype(BF16)
        o = jnp.where(lo, _dot(p0, vs[i]), _dot(p1, vs[i]))
        o = o + _dot((qs[i] * qw_f).astype(BF16), sf.astype(BF16))
        sf = gc_f * sf + jnp.where(bd, _dot((ks[i] * kw_f).T.astype(BF16), vs[i]), 0.0)
        outs.append(o)
    for i in reversed(range(n)):
        outs[i] = outs[i] + _dot((qs[i] * qw_b).astype(BF16), sb.astype(BF16))
        sb = gc_b * sb + jnp.where(bd, _dot((ks[i] * kw_b).T.astype(BF16), vs[i]), 0.0)
    o = jnp.concatenate(outs, axis=0) if n > 1 else outs[0]
    lo_s = _lane_lo((seq, LANES))
    inv_d = 1.0 / HEAD_DIM
    mu = jnp.where(lo_s, jnp.sum(jnp.where(lo_s, o, 0.0), axis=-1, keepdims=True),
                   jnp.sum(jnp.where(lo_s, 0.0, o), axis=-1, keepdims=True)) * inv_d
    d = o - mu
    d2 = d * d
    var = jnp.where(lo_s, jnp.sum(jnp.where(lo_s, d2, 0.0), axis=-1, keepdims=True),
                    jnp.sum(jnp.where(lo_s, 0.0, d2), axis=-1, keepdims=True)) * inv_d
    y = d * lax.rsqrt(var + EPS) * (g * jax.nn.sigmoid(g))
    return y, sf, sb


def _ctx_mixer_kernel(lgf_ref, lgb_ref, sink_ref, p_ref, o_in_ref, o_ref, sf_ref, sb_ref,
                      nk_ref, nv_ref, wk_ref, wv_ref):
    def col(blk):
        return p_ref[:, blk * LANES:(blk + 1) * LANES]

    for hp in range(N_RET_PAIRS):
        y, sf, sb = _ret_pair(col(CB_RQ + hp), col(CB_RK + hp), col(CB_RV + hp), col(CB_RG + hp),
                              (lgf_ref[2 * hp], lgf_ref[2 * hp + 1]),
                              (lgb_ref[2 * hp], lgb_ref[2 * hp + 1]), None, None, SEQ)
        o_ref[:, (OB_RET + hp) * LANES:(OB_RET + hp + 1) * LANES] = y.astype(o_ref.dtype)
        sf_ref[2 * hp] = sf[:HEAD_DIM, :HEAD_DIM]
        sf_ref[2 * hp + 1] = sf[HEAD_DIM:, HEAD_DIM:]
        sb_ref[2 * hp] = sb[:HEAD_DIM, :HEAD_DIM]
        sb_ref[2 * hp + 1] = sb[HEAD_DIM:, HEAD_DIM:]

    for hp in range(N_NA_PAIRS):
        k = col(CB_NK + hp)
        v = col(CB_NV + hp)
        o = _attn_pair(col(CB_NQ + hp) * SCALE, [k.astype(BF16)], [v.astype(BF16)], [None], None)
        o_ref[:, (OB_NA + hp) * LANES:(OB_NA + hp + 1) * LANES] = o.astype(o_ref.dtype)
        nk_ref[2 * hp] = k[:, :HEAD_DIM]
        nk_ref[2 * hp + 1] = k[:, HEAD_DIM:]
        nv_ref[2 * hp] = v[:, :HEAD_DIM]
        nv_ref[2 * hp + 1] = v[:, HEAD_DIM:]

    k = col(CB_WK)
    v = col(CB_WV)
    for kv in range(KV_WIN):
        wk_ref[kv] = k[:, kv * HEAD_DIM:(kv + 1) * HEAD_DIM]
        wv_ref[kv] = v[:, kv * HEAD_DIM:(kv + 1) * HEAD_DIM]
    lo = _lane_lo(k.shape)
    k_sw = pltpu.roll(k, HEAD_DIM, 1)
    v_sw = pltpu.roll(v, HEAD_DIM, 1)
    for hp in range(N_WIN_PAIRS):
        kv_lo = (2 * hp) // G_WIN
        kv_hi = (2 * hp + 1) // G_WIN
        kk = jnp.where(lo, k if kv_lo == 0 else k_sw, k if kv_hi == 1 else k_sw).astype(BF16)
        vv = jnp.where(lo, v if kv_lo == 0 else v_sw, v if kv_hi == 1 else v_sw).astype(BF16)
        o = _attn_pair(col(CB_WQ + hp) * SCALE, [kk], [vv], [None],
                       (sink_ref[2 * hp], sink_ref[2 * hp + 1]))
        o_ref[:, (OB_WIN + hp) * LANES:(OB_WIN + hp + 1) * LANES] = o.astype(o_ref.dtype)


def _smem_spec():
    return pl.BlockSpec(memory_space=pltpu.SMEM)


def _ctx_mixer(proj, lgf, lgb, sink, n_ctx):
    t = proj.shape[0]
    o_init = jnp.zeros((t, W_MIX), BF16)
    out_shape = (
        jax.ShapeDtypeStruct((t, W_MIX), BF16),
        jax.ShapeDtypeStruct((n_ctx, H_RET, HEAD_DIM, HEAD_DIM), F32),
        jax.ShapeDtypeStruct((n_ctx, H_RET, HEAD_DIM, HEAD_DIM), F32),
        jax.ShapeDtypeStruct((n_ctx, H_NA, SEQ, HEAD_DIM), F32),
        jax.ShapeDtypeStruct((n_ctx, H_NA, SEQ, HEAD_DIM), F32),
        jax.ShapeDtypeStruct((n_ctx, KV_WIN, SEQ, HEAD_DIM), F32),
        jax.ShapeDtypeStruct((n_ctx, KV_WIN, SEQ, HEAD_DIM), F32),
    )
    st = lambda h, a, b: pl.BlockSpec((None, h, a, b), lambda i: (i, 0, 0, 0))
    return pl.pallas_call(
        _ctx_mixer_kernel, out_shape=out_shape, grid=(n_ctx,),
        in_specs=[_smem_spec(), _smem_spec(), _smem_spec(),
                  pl.BlockSpec((SEQ, W_IN), lambda i: (i, 0)), pl.BlockSpec(memory_space=pl.ANY)],
        out_specs=(pl.BlockSpec((SEQ, W_MIX), lambda i: (i, 0)),
                   st(H_RET, HEAD_DIM, HEAD_DIM), st(H_RET, HEAD_DIM, HEAD_DIM),
                   st(H_NA, SEQ, HEAD_DIM), st(H_NA, SEQ, HEAD_DIM),
                   st(KV_WIN, SEQ, HEAD_DIM), st(KV_WIN, SEQ, HEAD_DIM)),
        input_output_aliases={4: 0},
        compiler_params=_cparams(("parallel",)), name="ctx_mixer",
    )(lgf, lgb, sink, proj, o_init)


def _ret_lat_kernel(lgf_ref, lgb_ref, q_ref, k_ref, v_ref, g_ref, s0f_ref, s0b_ref, o_in_ref, o_ref):
    hp = pl.program_id(1)
    y, _, _ = _ret_pair(q_ref[...], k_ref[...], v_ref[...], g_ref[...],
                        (lgf_ref[2 * hp], lgf_ref[2 * hp + 1]), (lgb_ref[2 * hp], lgb_ref[2 * hp + 1]),
                        _block_diag(s0f_ref[0], s0f_ref[1]), _block_diag(s0b_ref[0], s0b_ref[1]), DEC_SEQ)
    o_ref[...] = y.astype(o_ref.dtype)


def _ret_latent(proj, o, lgf, lgb, s0f, s0b, l, n_lat):
    rb0 = (proj.shape[0] - n_lat * DEC_SEQ) // DEC_SEQ
    cb = lambda c0: pl.BlockSpec((DEC_SEQ, LANES), lambda b, hp: (rb0 + b, c0 + hp))
    st = pl.BlockSpec((None, None, 2, HEAD_DIM, HEAD_DIM), lambda b, hp: (b, l, hp, 0, 0))
    return pl.pallas_call(
        _ret_lat_kernel, out_shape=jax.ShapeDtypeStruct(o.shape, o.dtype),
        grid=(n_lat, N_RET_PAIRS),
        in_specs=[_smem_spec(), _smem_spec(), cb(CB_RQ), cb(CB_RK), cb(CB_RV), cb(CB_RG), st, st,
                  pl.BlockSpec(memory_space=pl.ANY)],
        out_specs=pl.BlockSpec((DEC_SEQ, LANES), lambda b, hp: (rb0 + b, OB_RET + hp)),
        input_output_aliases={8: 0},
        compiler_params=_cparams(("parallel", "parallel")), name="ret_latent",
    )(lgf, lgb, proj, proj, proj, proj, s0f, s0b, o)


def _na_lat_kernel(q_ref, k_ref, v_ref, kc_ref, vc_ref, bias_ref, o_in_ref, o_ref):
    kc = jnp.concatenate([kc_ref[0], kc_ref[1]], axis=1).astype(BF16)
    vc = jnp.concatenate([vc_ref[0], vc_ref[1]], axis=1).astype(BF16)

    def body(r, carry):
        r0 = jnp.clip(r - NA_ROWS // 2, 0, LAT_ROWS - NA_ROWS)
        q0 = pl.multiple_of(r * GRID_W, GRID_W)
        k0 = pl.multiple_of(r0 * GRID_W, GRID_W)
        q = q_ref[pl.ds(q0, GRID_W), :] * SCALE
        kw = k_ref[pl.ds(k0, NA_WIN_KEYS), :].astype(BF16)
        vw = v_ref[pl.ds(k0, NA_WIN_KEYS), :].astype(BF16)
        dr0 = r0 - r + (NA_ROWS - 1)
        bias = [jnp.concatenate([bias_ref[h, dr0 + 2 * m] for m in range(NA_ROWS // 2)], axis=1)
                for h in (0, 1)]
        o = _attn_pair(q, [kw, kc], [vw, vc], [bias, None], None)
        o_ref[pl.ds(q0, GRID_W), :] = o.astype(o_ref.dtype)
        return carry

    lax.fori_loop(0, LAT_ROWS, body, 0)


N_ROW_OFFS = 2 * NA_ROWS - 1


def _na_bias_table(rpb):
    cols = np.arange(GRID_W)
    c0 = np.clip(cols - NA_COLS // 2, 0, GRID_W - NA_COLS)
    col_ok = (cols[None, :] >= c0[:, None]) & (cols[None, :] < c0[:, None] + NA_COLS)
    dc = np.clip(cols[None, :] - cols[:, None], -(NA_COLS - 1), NA_COLS - 1) + (NA_COLS - 1)
    by_col = jnp.where(col_ok[None, None], rpb.astype(F32)[:, :, dc], NEG)
    return jnp.concatenate([by_col[:, :-1], by_col[:, 1:]], axis=-1)


def _na_latent(proj, o, kc, vc, bias, l, n_lat):
    rb0 = (proj.shape[0] - n_lat * DEC_SEQ) // DEC_SEQ
    cb = lambda c0: pl.BlockSpec((DEC_SEQ, LANES), lambda b, hp: (rb0 + b, c0 + hp))
    cache = pl.BlockSpec((None, None, 2, PAST_LEN, HEAD_DIM), lambda b, hp: (b, l, hp, 0, 0))
    return pl.pallas_call(
        _na_lat_kernel, out_shape=jax.ShapeDtypeStruct(o.shape, o.dtype),
        grid=(n_lat, N_NA_PAIRS),
        in_specs=[cb(CB_NQ), cb(CB_NK), cb(CB_NV), cache, cache,
                  pl.BlockSpec((2, N_ROW_OFFS - 1, GRID_W, LANES), lambda b, hp: (hp, 0, 0, 0)),
                  pl.BlockSpec(memory_space=pl.ANY)],
        out_specs=pl.BlockSpec((DEC_SEQ, LANES), lambda b, hp: (rb0 + b, OB_NA + hp)),
        input_output_aliases={6: 0},
        compiler_params=_cparams(("parallel", "parallel")), name="na_latent",
    )(proj, proj, proj, kc, vc, bias, o)


def _rope_tables():
    t = jnp.arange(DEC_SEQ)
    d = np.arange(LANES) % HEAD_DIM
    quarter = HEAD_DIM // 4
    inv = ROPE_BASE ** (-jnp.arange(quarter, dtype=F32) / quarter)
    pos = jnp.where(jnp.asarray(d < HEAD_DIM // 2)[None, :], (t // GRID_W)[:, None], (t % GRID_W)[:, None])
    ang = pos.astype(F32) * inv[d % quarter][None, :]
    sign = jnp.asarray(np.where((d & quarter) == 0, -1.0, 1.0), F32)
    return jnp.cos(ang), jnp.sin(ang) * sign[None, :]


def _win_lat_kernel(sink_ref, q_ref, k_ref, v_ref, kc_ref, vc_ref, cos_ref, sin_ref, o_in_ref, o_ref,
                    qs_ref, kp_ref, vp_ref):
    hp = pl.program_id(1)
    quarter = HEAD_DIM // 4
    lane = lax.broadcasted_iota(jnp.int32, (DEC_SEQ, LANES), 1)
    first = (lane & quarter) == 0
    lo = lane < HEAD_DIM
    cos = cos_ref[...]
    sin = sin_ref[...]

    def rope(x):
        sw = jnp.where(first, pltpu.roll(x, LANES - quarter, 1), pltpu.roll(x, quarter, 1))
        return x * cos + sw * sin

    qs_ref[...] = rope(q_ref[...]) * SCALE
    k = rope(k_ref[...])
    v = v_ref[...]
    lo_orig = (2 * hp) // G_WIN == 0
    hi_orig = (2 * hp + 1) // G_WIN == 1
    orig = jnp.where(lo, lo_orig.astype(jnp.int32), hi_orig.astype(jnp.int32)) == 1
    zeros = jnp.zeros((WIN_BLOCK, LANES), BF16)
    kp_ref[:WIN_BLOCK] = zeros
    kp_ref[WIN_BLOCK + DEC_SEQ:] = zeros
    vp_ref[:WIN_BLOCK] = zeros
    vp_ref[WIN_BLOCK + DEC_SEQ:] = zeros
    kp_ref[WIN_BLOCK:WIN_BLOCK + DEC_SEQ] = jnp.where(orig, k, pltpu.roll(k, HEAD_DIM, 1)).astype(BF16)
    vp_ref[WIN_BLOCK:WIN_BLOCK + DEC_SEQ] = jnp.where(orig, v, pltpu.roll(v, HEAD_DIM, 1)).astype(BF16)
    kc = jnp.concatenate([jnp.where(lo_orig, kc_ref[0], kc_ref[1]),
                          jnp.where(hi_orig, kc_ref[1], kc_ref[0])], axis=1).astype(BF16)
    vc = jnp.concatenate([jnp.where(lo_orig, vc_ref[0], vc_ref[1]),
                          jnp.where(hi_orig, vc_ref[1], vc_ref[0])], axis=1).astype(BF16)
    sinks = (sink_ref[2 * hp], sink_ref[2 * hp + 1])
    n_band = 3 * WIN_BLOCK
    qi = lax.broadcasted_iota(jnp.int32, (WIN_BLOCK, n_band), 0)
    kj = lax.broadcasted_iota(jnp.int32, (WIN_BLOCK, n_band), 1)
    near = jnp.abs(qi + WIN_BLOCK - kj) <= WIN_HALF

    def body(n, carry):
        q0 = pl.multiple_of(n * WIN_BLOCK, WIN_BLOCK)
        kpos = (n - 1) * WIN_BLOCK + kj
        ok = jnp.logical_and(near, jnp.logical_and(kpos >= 0, kpos < DEC_SEQ))
        band = jnp.where(ok, 0.0, NEG)
        o = _attn_pair(qs_ref[pl.ds(q0, WIN_BLOCK), :],
                       [kp_ref[pl.ds(q0, n_band), :], kc], [vp_ref[pl.ds(q0, n_band), :], vc],
                       [(band, band), None], sinks)
        o_ref[pl.ds(q0, WIN_BLOCK), :] = o.astype(o_ref.dtype)
        return carry

    lax.fori_loop(0, DEC_SEQ // WIN_BLOCK, body, 0)


def _win_latent(proj, o, kc, vc, sink, cos, sin, l, n_lat):
    rb0 = (proj.shape[0] - n_lat * DEC_SEQ) // DEC_SEQ
    cache = pl.BlockSpec((None, None, KV_WIN, PAST_LEN, HEAD_DIM), lambda b, hp: (b, l, 0, 0, 0))
    tbl = pl.BlockSpec((DEC_SEQ, LANES), lambda b, hp: (0, 0))
    return pl.pallas_call(
        _win_lat_kernel, out_shape=jax.ShapeDtypeStruct(o.shape, o.dtype),
        grid=(n_lat, N_WIN_PAIRS),
        in_specs=[_smem_spec(),
                  pl.BlockSpec((DEC_SEQ, LANES), lambda b, hp: (rb0 + b, CB_WQ + hp)),
                  pl.BlockSpec((DEC_SEQ, LANES), lambda b, hp: (rb0 + b, CB_WK)),
                  pl.BlockSpec((DEC_SEQ, LANES), lambda b, hp: (rb0 + b, CB_WV)),
                  cache, cache, tbl, tbl, pl.BlockSpec(memory_space=pl.ANY)],
        out_specs=pl.BlockSpec((DEC_SEQ, LANES), lambda b, hp: (rb0 + b, OB_WIN + hp)),
        scratch_shapes=[pltpu.VMEM((DEC_SEQ, LANES), F32),
                        pltpu.VMEM((DEC_SEQ + 2 * WIN_BLOCK, LANES), BF16),
                        pltpu.VMEM((DEC_SEQ + 2 * WIN_BLOCK, LANES), BF16)],
        input_output_aliases={8: 0},
        compiler_params=_cparams(("parallel", "parallel")), name="win_latent",
    )(sink, proj, proj, proj, kc, vc, cos, sin, o)


def _mixers(proj, l, n_ctx, n_lat, state_ret_fwd, state_ret_bwd, cache_na_k, cache_na_v,
            cache_win_k, cache_win_v, ret_decay_fwd, ret_decay_bwd, na_rpb, win_sink):
    lgf = jax.nn.log_sigmoid(ret_decay_fwd[l].astype(F32))
    lgb = jax.nn.log_sigmoid(ret_decay_bwd[l].astype(F32))
    sink = win_sink[l].astype(F32)
    o, sf, sb, nk, nv, wk, wv = _ctx_mixer(proj, lgf, lgb, sink, n_ctx)
    o = _ret_latent(proj, o, lgf, lgb, state_ret_fwd, state_ret_bwd, l, n_lat)
    o = _na_latent(proj, o, cache_na_k, cache_na_v, _na_bias_table(na_rpb[l]), l, n_lat)
    cos, sin = _rope_tables()
    o = _win_latent(proj, o, cache_win_k, cache_win_v, sink, cos, sin, l, n_lat)
    return o, (sf, sb, nk, nv, wk, wv)


def kernel(x_prompt, x_sample, c, state_ret_fwd, state_ret_bwd, cache_na_k, cache_na_v, cache_win_k, cache_win_v, c_ctx, norm1_g, norm2_g, ada_w, ada_b, w_in, w_out, ret_decay_fwd, ret_decay_bwd, na_rpb, win_sink, ffn_w_in, ffn_w_out, moe_router, moe_w_in, moe_w_out, final_norm_g):
    x = jnp.concatenate([x_prompt.reshape(T_CTX, D_MODEL), x_sample.reshape(T_LAT, D_MODEL)], axis=0)
    cond = jnp.concatenate([c_ctx[None, :], c, jnp.zeros((N_GROUPS - 1 - DEC_BATCH, D_MODEL), F32)], axis=0)
    ctx_states = []
    for l in range(DEPTH):
        mod = _adaln(cond, ada_w, ada_b, l)
        h = _norm_mod(x, norm1_g, mod, l, 0, 1)
        proj = _matmul(h, w_in, l)
        o, ctx = _mixers(proj, l, BATCH, DEC_BATCH, state_ret_fwd, state_ret_bwd, cache_na_k, cache_na_v,
                         cache_win_k, cache_win_v, ret_decay_fwd, ret_decay_bwd, na_rpb, win_sink)
        ctx_states.append(ctx)
        x = _matmul(o, w_out, l, resid=x, mod=mod, k_gate=2)
        i = l // 2
        if l % 2 == 0:
            h = _norm_mod(x, norm2_g, mod, l, 3, 4)
            x = _resid(x, _ffn_dense(h, ffn_w_in, ffn_w_out, i, D_FF, FFN_TM), mod, 5)
        else:
            router = jnp.pad(moe_router[i], ((0, 0), (0, LANES - N_EXPERTS)))
            h2, route, counts = _norm_mod(x, norm2_g, mod, l, 3, 4, router=router)
            x = _moe(x, h2, route, counts, moe_w_in[i], moe_w_out[i], mod, 5)
    y = _final_norm(x, final_norm_g)
    y_prompt = y[:T_CTX].reshape(BATCH, SEQ, D_MODEL)
    y_sample = y[T_CTX:].reshape(DEC_BATCH, DEC_SEQ, D_MODEL)
    dt = x_prompt.dtype
    outs = [jnp.stack([s[k] for s in ctx_states], axis=1).astype(dt) for k in range(6)]
    return (y_prompt, y_sample, *outs)
```

```python
import functools

import jax
import jax.numpy as jnp
from jax import lax
import numpy as np
from jax.experimental import pallas as pl
from jax.experimental.pallas import tpu as pltpu

D_MODEL = 2048
BATCH = 32
SEQ = 256
DEPTH = 2
DEC_BATCH = 4
DEC_SEQ = 1024
PAST_LEN = 256

GRID_W = 64
HEAD_DIM = 64
H_RET = 12
H_NA = 10
H_WIN = 10
KV_WIN = 2
G_WIN = H_WIN // KV_WIN
W_RET = H_RET * HEAD_DIM
W_NA = H_NA * HEAD_DIM
W_WIN = H_WIN * HEAD_DIM
W_MIX = W_RET + W_NA + W_WIN
W_IN = 4 * W_RET + 3 * W_NA + W_WIN + 2 * KV_WIN * HEAD_DIM
RET_CHUNK = 128
NA_ROWS = 8
NA_COLS = 16
WIN_HALF = 128
WIN_BLOCK = 128
ROPE_BASE = 10000.0
D_FF = 5632
N_EXPERTS = 8
TOP_K = 2
D_FF_EXPERT = 7168
EPS = 1e-6
NEG = -1e30
SCALE = HEAD_DIM ** -0.5

T_CTX = BATCH * SEQ
T_LAT = DEC_BATCH * DEC_SEQ
T_ALL = T_CTX + T_LAT
N_GROUPS = 8
LANES = 128

F32 = jnp.float32
BF16 = jnp.bfloat16

VMEM_LIMIT = 56 * 1024 * 1024


def _group_of_tile(i, tm):
    return jnp.maximum((i * tm - T_CTX) // DEC_SEQ + 1, 0)


def _cparams(sem):
    return pltpu.CompilerParams(dimension_semantics=sem, vmem_limit_bytes=VMEM_LIMIT)


def _adaln_kernel(c_ref, w_ref, b_ref, o_ref):
    c = c_ref[...]
    s = (c * jax.nn.sigmoid(c)).astype(BF16)
    o_ref[...] = jnp.dot(s, w_ref[...].astype(BF16), preferred_element_type=F32) + b_ref[...]


def _adaln(cond, ada_w, ada_b, l):
    tn = 1024
    n = 6 * D_MODEL
    out = pl.pallas_call(
        _adaln_kernel,
        out_shape=jax.ShapeDtypeStruct((N_GROUPS, n), F32),
        grid=(n // tn,),
        in_specs=[
            pl.BlockSpec((N_GROUPS, D_MODEL), lambda j: (0, 0)),
            pl.BlockSpec((None, D_MODEL, tn), lambda j: (l, 0, j)),
            pl.BlockSpec((None, 1, tn), lambda j: (l, 0, j)),
        ],
        out_specs=pl.BlockSpec((N_GROUPS, tn), lambda j: (0, j)),
        compiler_params=_cparams(("arbitrary",)),
        name="adaln",
    )(cond, ada_w, ada_b.reshape(DEPTH, 1, n))
    return out.reshape(N_GROUPS, 6, 1, D_MODEL)


N_SUB = D_MODEL // LANES


def _store_token_major(ref, val):
    rows = val.shape[0]
    for s in range(N_SUB):
        ref[pl.ds(s, rows, stride=N_SUB), :] = val[:, s * LANES:(s + 1) * LANES].astype(ref.dtype)


def _load_token_major(ref, rows, dtype=F32):
    return jnp.concatenate([ref[pl.ds(s, rows, stride=N_SUB), :].astype(dtype) for s in range(N_SUB)],
                           axis=1)


def _token_slab(ref, t):
    return ref.at[pl.ds(pl.multiple_of(t * N_SUB, N_SUB), N_SUB)]


def _norm_mod_body(x_ref, g_ref, sh_ref, sc_ref):
    x = x_ref[...]
    ms = jnp.mean(x * x, axis=-1, keepdims=True)
    y = x * lax.rsqrt(ms + EPS) * g_ref[...]
    return y * (1.0 + sc_ref[...]) + sh_ref[...]


def _norm_mod_kernel(x_ref, g_ref, sh_ref, sc_ref, h_ref):
    h_ref[...] = _norm_mod_body(x_ref, g_ref, sh_ref, sc_ref).astype(h_ref.dtype)


R_E0, R_E1, R_RANK0, R_RANK1, R_W0, R_W1 = range(6)


def _norm_mod_router_kernel(x_ref, g_ref, sh_ref, sc_ref, r_ref, h2_ref, route_ref, cnt_ref, run_ref):
    i = pl.program_id(0)

    @pl.when(i == 0)
    def _():
        run_ref[...] = jnp.zeros_like(run_ref)

    h = _norm_mod_body(x_ref, g_ref, sh_ref, sc_ref)
    _store_token_major(h2_ref, h)
    tm = h.shape[0]
    lg = jnp.dot(h, r_ref[...], preferred_element_type=F32, precision=lax.Precision.HIGHEST)
    lane = lax.broadcasted_iota(jnp.int32, (tm, LANES), 1)
    lane_f = lane.astype(F32)
    lg = jnp.where(lane < N_EXPERTS, lg, -jnp.inf)
    v0 = jnp.max(lg, axis=-1, keepdims=True)
    e0 = jnp.min(jnp.where(lg == v0, lane_f, float(LANES)), axis=-1, keepdims=True)
    lg1 = jnp.where(lane_f == e0, -jnp.inf, lg)
    v1 = jnp.max(lg1, axis=-1, keepdims=True)
    e1 = jnp.min(jnp.where(lg1 == v1, lane_f, float(LANES)), axis=-1, keepdims=True)
    ex = jnp.exp(v1 - v0)
    w0 = 1.0 / (1.0 + ex)
    w1 = ex / (1.0 + ex)
    oh0 = jnp.where(lane_f == e0, 1.0, 0.0)
    oh1 = jnp.where(lane_f == e1, 1.0, 0.0)
    oh = oh0 + oh1
    earlier = (lax.broadcasted_iota(jnp.int32, (tm, tm), 0)
               > lax.broadcasted_iota(jnp.int32, (tm, tm), 1))
    before = jnp.dot(jnp.where(earlier, 1.0, 0.0).astype(BF16), oh.astype(BF16),
                     preferred_element_type=F32) + run_ref[0:1, :]
    rank0 = jnp.sum(oh0 * before, axis=-1, keepdims=True)
    rank1 = jnp.sum(oh1 * before, axis=-1, keepdims=True)
    rec = jnp.zeros((tm, LANES), F32)
    for k, val in ((R_E0, e0), (R_E1, e1), (R_RANK0, rank0), (R_RANK1, rank1), (R_W0, w0), (R_W1, w1)):
        rec = jnp.where(lane == k, val, rec)
    route_ref[...] = rec
    run_ref[0:1, :] = run_ref[0:1, :] + jnp.sum(oh, axis=0, keepdims=True)
    cnt_ref[...] = run_ref[...]


def _norm_mod(x, g, mod, l, k_shift, k_scale, *, router=None):
    tm = 512
    g3 = g.reshape(DEPTH, 1, D_MODEL)
    in_specs = [
        pl.BlockSpec((tm, D_MODEL), lambda i: (i, 0)),
        pl.BlockSpec((None, 1, D_MODEL), lambda i: (l, 0, 0)),
        pl.BlockSpec((None, None, 1, D_MODEL), lambda i: (_group_of_tile(i, tm), k_shift, 0, 0)),
        pl.BlockSpec((None, None, 1, D_MODEL), lambda i: (_group_of_tile(i, tm), k_scale, 0, 0)),
    ]
    if router is not None:
        in_specs.append(pl.BlockSpec((D_MODEL, LANES), lambda i: (0, 0)))
        return pl.pallas_call(
            _norm_mod_router_kernel,
            out_shape=(jax.ShapeDtypeStruct((T_ALL * N_SUB, LANES), F32),
                       jax.ShapeDtypeStruct((T_ALL, LANES), F32),
                       jax.ShapeDtypeStruct((8, LANES), F32)),
            grid=(T_ALL // tm,),
            in_specs=in_specs,
            out_specs=(pl.BlockSpec((tm * N_SUB, LANES), lambda i: (i, 0)),
                       pl.BlockSpec((tm, LANES), lambda i: (i, 0)),
                       pl.BlockSpec((8, LANES), lambda i: (0, 0))),
            scratch_shapes=[pltpu.VMEM((8, LANES), F32)],
            compiler_params=_cparams(("arbitrary",)), name="norm_mod_router",
        )(x, g3, mod, mod, router)
    return pl.pallas_call(
        _norm_mod_kernel, out_shape=jax.ShapeDtypeStruct((T_ALL, D_MODEL), BF16), grid=(T_ALL // tm,),
        in_specs=in_specs, out_specs=pl.BlockSpec((tm, D_MODEL), lambda i: (i, 0)),
        compiler_params=_cparams(("parallel",)), name="norm_mod",
    )(x, g3, mod, mod)


def _mm_kernel(a_ref, w_ref, o_ref):
    o_ref[...] = jnp.dot(a_ref[...].astype(BF16), w_ref[...].astype(BF16),
                         preferred_element_type=F32)


def _mm_res_kernel(a_ref, w_ref, x_ref, gate_ref, o_ref):
    acc = jnp.dot(a_ref[...].astype(BF16), w_ref[...].astype(BF16),
                  preferred_element_type=F32)
    o_ref[...] = x_ref[...] + gate_ref[...] * acc


def _matmul(a, w, l, *, resid=None, mod=None, k_gate=None):
    tm, tn = (2048 if resid is None else DEC_SEQ), 512
    t, k = a.shape
    n = w.shape[-1]
    grid = (t // tm, pl.cdiv(n, tn))
    in_specs = [
        pl.BlockSpec((tm, k), lambda i, j: (i, 0)),
        pl.BlockSpec((None, k, tn), lambda i, j: (l, 0, j)),
    ]
    args = [a, w]
    kern = _mm_kernel
    if resid is not None:
        in_specs += [
            pl.BlockSpec((tm, tn), lambda i, j: (i, j)),
            pl.BlockSpec((None, None, 1, tn), lambda i, j: (_group_of_tile(i, tm), k_gate, 0, j)),
        ]
        args += [resid, mod]
        kern = _mm_res_kernel
    return pl.pallas_call(
        kern, out_shape=jax.ShapeDtypeStruct((t, n), F32), grid=grid,
        in_specs=in_specs, out_specs=pl.BlockSpec((tm, tn), lambda i, j: (i, j)),
        compiler_params=_cparams(("parallel", "arbitrary")), name="proj",
    )(*args)


FFN_TF = 256


def _ffn_step(h, wa_ref, wg_ref, wo_ref, acc_ref, j):
    a = jnp.dot(h, wa_ref[...].astype(BF16), preferred_element_type=F32)
    g = jnp.dot(h, wg_ref[...].astype(BF16), preferred_element_type=F32)
    act = (a * jax.nn.sigmoid(a) * g).astype(BF16)

    @pl.when(j == 0)
    def _():
        acc_ref[...] = jnp.zeros_like(acc_ref)

    acc_ref[...] += jnp.dot(act, wo_ref[...].astype(BF16), preferred_element_type=F32)


def _ffn_dense_kernel(h_ref, wa_ref, wg_ref, wo_ref, o_ref):
    _ffn_step(h_ref[...], wa_ref, wg_ref, wo_ref, o_ref, pl.program_id(1))


def _ffn_dense(h, w_in, w_out, e, d_ff, tm):
    tf = FFN_TF
    n_f = d_ff // tf
    return pl.pallas_call(
        _ffn_dense_kernel,
        out_shape=jax.ShapeDtypeStruct((h.shape[0], D_MODEL), F32),
        grid=(h.shape[0] // tm, n_f),
        in_specs=[
            pl.BlockSpec((tm, D_MODEL), lambda i, j: (i, 0)),
            pl.BlockSpec((None, D_MODEL, tf), lambda i, j: (e, 0, j)),
            pl.BlockSpec((None, D_MODEL, tf), lambda i, j: (e, 0, j + n_f)),
            pl.BlockSpec((None, tf, D_MODEL), lambda i, j: (e, j, 0)),
        ],
        out_specs=pl.BlockSpec((tm, D_MODEL), lambda i, j: (i, 0)),
        compiler_params=_cparams(("parallel", "arbitrary")), name="ffn_dense",
    )(h, w_in, w_in, w_out)


def _ffn_expert_kernel(te_ref, nt_ref, tok_ref, h2_ref, wa_ref, wg_ref, wo_ref, o_ref,
                       gbuf_ref, hb_ref, acc_ref, sem, *, n_f, tm):
    i = pl.program_id(0)
    j = pl.program_id(1)
    valid = i < nt_ref[0]

    def start_gather(tile):
        def body(r, c):
            pltpu.make_async_copy(_token_slab(h2_ref, tok_ref[tile * tm + r]),
                                  _token_slab(gbuf_ref, r), sem).start()
            return c
        lax.fori_loop(0, tm, body, 0)

    @pl.when(jnp.logical_and(valid, j == 0))
    def _():
        @pl.when(i == 0)
        def _():
            start_gather(0)

        pltpu.make_async_copy(h2_ref.at[pl.ds(0, tm * N_SUB)], gbuf_ref, sem).wait()
        for s in range(N_SUB):
            hb_ref[:, s * LANES:(s + 1) * LANES] = gbuf_ref[pl.ds(s, tm, stride=N_SUB), :].astype(BF16)

        @pl.when(i + 1 < nt_ref[0])
        def _():
            start_gather(i + 1)

    @pl.when(valid)
    def _():
        _ffn_step(hb_ref[...], wa_ref, wg_ref, wo_ref, acc_ref, j)

        @pl.when(j == n_f - 1)
        def _():
            _store_token_major(o_ref, acc_ref[...])

    @pl.when(jnp.logical_and(jnp.logical_not(valid), j == 0))
    def _():
        o_ref[...] = jnp.zeros_like(o_ref)


def _ffn_expert(h2, row_token, te, nt, w_in, w_out, d_ff, tm):
    tf = FFN_TF
    r = row_token.shape[0]
    n_f = d_ff // tf

    def jj(i, j, nt_ref):
        return jnp.where(i < nt_ref[0], j, n_f - 1)

    grid_spec = pltpu.PrefetchScalarGridSpec(
        num_scalar_prefetch=3,
        grid=(r // tm, n_f),
        in_specs=[
            pl.BlockSpec(memory_space=pl.ANY),
            pl.BlockSpec((None, D_MODEL, tf), lambda i, j, te_ref, nt_ref, tok_ref: (te_ref[i], 0, jj(i, j, nt_ref))),
            pl.BlockSpec((None, D_MODEL, tf), lambda i, j, te_ref, nt_ref, tok_ref: (te_ref[i], 0, jj(i, j, nt_ref) + n_f)),
            pl.BlockSpec((None, tf, D_MODEL), lambda i, j, te_ref, nt_ref, tok_ref: (te_ref[i], jj(i, j, nt_ref), 0)),
        ],
        out_specs=pl.BlockSpec((tm * N_SUB, LANES), lambda i, j, te_ref, nt_ref, tok_ref: (i, 0),
                               pipeline_mode=pl.Buffered(1)),
        scratch_shapes=[pltpu.VMEM((tm * N_SUB, LANES), F32), pltpu.VMEM((tm, D_MODEL), BF16),
                        pltpu.VMEM((tm, D_MODEL), F32), pltpu.SemaphoreType.DMA],
    )
    return pl.pallas_call(
        functools.partial(_ffn_expert_kernel, n_f=n_f, tm=tm),
        out_shape=jax.ShapeDtypeStruct((r * N_SUB, LANES), F32),
        grid_spec=grid_spec,
        compiler_params=_cparams(("arbitrary", "arbitrary")), name="ffn_expert",
    )(te, nt, row_token, h2, w_in, w_in, w_out)


def _resid_kernel(x_ref, y_ref, gate_ref, o_ref):
    o_ref[...] = x_ref[...] + gate_ref[...] * y_ref[...]


def _resid(x, y, mod, k_gate):
    tm = 512
    return pl.pallas_call(
        _resid_kernel,
        out_shape=jax.ShapeDtypeStruct((T_ALL, D_MODEL), F32), grid=(T_ALL // tm,),
        in_specs=[
            pl.BlockSpec((tm, D_MODEL), lambda i: (i, 0)),
            pl.BlockSpec((tm, D_MODEL), lambda i: (i, 0)),
            pl.BlockSpec((None, None, 1, D_MODEL), lambda i: (_group_of_tile(i, tm), k_gate, 0, 0)),
        ],
        out_specs=pl.BlockSpec((tm, D_MODEL), lambda i: (i, 0)),
        compiler_params=_cparams(("parallel",)), name="resid",
    )(x, y, mod)


COMBINE_TM = 256


def _combine_kernel(eid_ref, rank_ref, first_row_ref, x_ref, route_ref, gate_ref, ys2_ref, o_ref,
                    buf_ref, sem):
    i = pl.program_id(0)
    n = pl.num_programs(0)
    tm = COMBINE_TM

    def start_tile(tile, slot):
        for k in range(TOP_K):
            def start(t, c):
                a = (tile * tm + t) * TOP_K + k
                row = first_row_ref[eid_ref[a]] + rank_ref[a]
                pltpu.make_async_copy(_token_slab(ys2_ref, row), _token_slab(buf_ref.at[slot, k], t),
                                      sem.at[slot]).start()
                return c
            lax.fori_loop(0, tm, start, 0)

    @pl.when(i == 0)
    def _():
        start_tile(0, 0)

    @pl.when(i + 1 < n)
    def _():
        start_tile(i + 1, (i + 1) % 2)

    slot = i % 2
    for k in range(TOP_K):
        pltpu.make_async_copy(ys2_ref.at[pl.ds(0, tm * N_SUB)], buf_ref.at[slot, k], sem.at[slot]).wait()

    rec = route_ref[...]
    w0 = rec[:, R_W0:R_W0 + 1]
    w1 = rec[:, R_W1:R_W1 + 1]
    y = (w0 * _load_token_major(buf_ref.at[slot, 0], tm) + w1 * _load_token_major(buf_ref.at[slot, 1], tm))
    o_ref[...] = x_ref[...] + gate_ref[...] * y


def _combine(x, ys2, route, eid, rank, first_row, mod, k_gate):
    tm = COMBINE_TM
    grid_spec = pltpu.PrefetchScalarGridSpec(
        num_scalar_prefetch=3,
        grid=(T_ALL // tm,),
        in_specs=[
            pl.BlockSpec((tm, D_MODEL), lambda i, *_: (i, 0)),
            pl.BlockSpec((tm, LANES), lambda i, *_: (i, 0)),
            pl.BlockSpec((None, None, 1, D_MODEL), lambda i, *_: (_group_of_tile(i, tm), k_gate, 0, 0)),
            pl.BlockSpec(memory_space=pl.ANY),
        ],
        out_specs=pl.BlockSpec((tm, D_MODEL), lambda i, *_: (i, 0)),
        scratch_shapes=[pltpu.VMEM((2, TOP_K, tm * N_SUB, LANES), F32), pltpu.SemaphoreType.DMA((2,))],
    )
    return pl.pallas_call(
        _combine_kernel,
        out_shape=jax.ShapeDtypeStruct((T_ALL, D_MODEL), F32),
        grid_spec=grid_spec,
        compiler_params=_cparams(("arbitrary",)), name="moe_combine",
    )(eid, rank, first_row, x, route, mod, ys2)


def _final_norm_kernel(x_ref, g_ref, o_ref):
    x = x_ref[...]
    ms = jnp.mean(x * x, axis=-1, keepdims=True)
    o_ref[...] = x * lax.rsqrt(ms + EPS) * g_ref[...]


def _final_norm(x, g, row0, rows):
    tm = 512
    return pl.pallas_call(
        _final_norm_kernel,
        out_shape=jax.ShapeDtypeStruct((rows, D_MODEL), F32), grid=(rows // tm,),
        in_specs=[pl.BlockSpec((tm, D_MODEL), lambda i: (row0 // tm + i, 0)),
                  pl.BlockSpec((1, D_MODEL), lambda i: (0, 0))],
        out_specs=pl.BlockSpec((tm, D_MODEL), lambda i: (i, 0)),
        compiler_params=_cparams(("parallel",)), name="final_norm",
    )(x, g.reshape(1, D_MODEL))


FFN_TM = 1024
MOE_TM = 1024
MOE_TILES = (TOP_K * T_ALL) // MOE_TM + N_EXPERTS


def _moe(x, h2, route, counts, w_in, w_out, mod, k_gate):
    tm = MOE_TM
    eid = route[:, R_E0:R_E1 + 1].astype(jnp.int32).reshape(-1)
    rank = route[:, R_RANK0:R_RANK1 + 1].astype(jnp.int32).reshape(-1)
    counts = counts[0, :N_EXPERTS].astype(jnp.int32)
    ptiles = (counts + tm - 1) // tm
    tile_end = jnp.cumsum(ptiles)
    first_row = (tile_end - ptiles) * tm
    n_tiles = tile_end[-1]
    tile_ids = jnp.arange(MOE_TILES, dtype=jnp.int32)
    te = jnp.sum((tile_ids[:, None] >= tile_end[None, :]).astype(jnp.int32), axis=1)
    te = jnp.minimum(te, N_EXPERTS - 1)
    te_last = jnp.sum(jnp.where(tile_ids == n_tiles - 1, te, 0))
    te = jnp.where(tile_ids < n_tiles, te, te_last)
    experts = jnp.arange(N_EXPERTS, dtype=jnp.int32)
    pos = jnp.sum(jnp.where(eid[:, None] == experts[None, :], first_row[None, :], 0), axis=1) + rank
    tok = jnp.arange(TOP_K * T_ALL, dtype=jnp.int32) // TOP_K
    row_token = jnp.zeros((MOE_TILES * tm,), jnp.int32).at[pos].set(tok)
    ys2 = _ffn_expert(h2, row_token, te, n_tiles.reshape(1), w_in, w_out, D_FF_EXPERT, tm)
    return _combine(x, ys2, route, eid, rank, first_row, mod, k_gate)


CB_RQ, CB_RK, CB_RV, CB_RG = 0, 6, 12, 18
CB_NQ, CB_NK, CB_NV = 24, 29, 34
CB_WQ, CB_WK, CB_WV = 39, 44, 45
OB_RET, OB_NA, OB_WIN = 0, 6, 11
N_RET_PAIRS = H_RET // 2
N_NA_PAIRS = H_NA // 2
N_WIN_PAIRS = H_WIN // 2
NA_WIN_KEYS = NA_ROWS * GRID_W
LAT_ROWS = DEC_SEQ // GRID_W


def _lane_lo(shape):
    return lax.broadcasted_iota(jnp.int32, shape, len(shape) - 1) < HEAD_DIM


def _dot_nt(a, b):
    return lax.dot_general(a, b, (((1,), (1,)), ((), ())), preferred_element_type=F32)


def _dot(a, b):
    return jnp.dot(a, b, preferred_element_type=F32)


def _attn_pair(q, ks, vs, biases, sinks):
    lo = _lane_lo(q.shape)
    outs = []
    for half in (0, 1):
        qm = jnp.where(lo if half == 0 else jnp.logical_not(lo), q, 0.0).astype(BF16)
        ss = []
        for kb, bb in zip(ks, biases):
            s = _dot_nt(qm, kb)
            if bb is not None:
                s = s + bb[half]
            ss.append(s)
        m = jnp.max(ss[0], axis=-1, keepdims=True)
        for s in ss[1:]:
            m = jnp.maximum(m, jnp.max(s, axis=-1, keepdims=True))
        if sinks is not None:
            m = jnp.maximum(m, sinks[half])
        ps = [jnp.exp(s - m) for s in ss]
        den = jnp.sum(ps[0], axis=-1, keepdims=True)
        for p in ps[1:]:
            den = den + jnp.sum(p, axis=-1, keepdims=True)
        if sinks is not None:
            den = den + jnp.exp(sinks[half] - m)
        o = _dot(ps[0].astype(BF16), vs[0])
        for p, vb in zip(ps[1:], vs[1:]):
            o = o + _dot(p.astype(BF16), vb)
        outs.append(o / den)
    return jnp.where(lo, outs[0], outs[1])


def _block_diag(s0, s1):
    z = jnp.zeros_like(s0)
    return jnp.concatenate([jnp.concatenate([s0, z], axis=1), jnp.concatenate([z, s1], axis=1)], axis=0)


def _ret_pair(q, k, v, g, lgf, lgb, s0f, s0b, seq):
    c = RET_CHUNK
    n = seq // c
    lo1 = _lane_lo((1, LANES))
    lo = _lane_lo((c, LANES))
    lgf_v = jnp.where(lo1, lgf[0], lgf[1])
    lgb_v = jnp.where(lo1, lgb[0], lgb[1])
    pos = lax.broadcasted_iota(jnp.int32, (c, LANES), 0).astype(F32)
    qw_f = jnp.exp(lgf_v * (pos + 1.0))
    kw_f = jnp.exp(lgf_v * (c - 1.0 - pos))
    qw_b = jnp.exp(lgb_v * (c - pos))
    kw_b = jnp.exp(lgb_v * pos)
    gc_f = jnp.exp(lgf_v * float(c))
    gc_b = jnp.exp(lgb_v * float(c))
    diff = (lax.broadcasted_iota(jnp.int32, (c, c), 0)
            - lax.broadcasted_iota(jnp.int32, (c, c), 1)).astype(F32)
    decay = [jnp.where(diff >= 0, jnp.exp(lgf[h] * jnp.maximum(diff, 0.0)), 0.0)
             + jnp.where(diff <= 0, jnp.exp(lgb[h] * jnp.maximum(-diff, 0.0)), 0.0) for h in (0, 1)]
    bd = ((lax.broadcasted_iota(jnp.int32, (LANES, LANES), 0) < HEAD_DIM)
          == (lax.broadcasted_iota(jnp.int32, (LANES, LANES), 1) < HEAD_DIM))
    zero = jnp.zeros((LANES, LANES), F32)
    sf = zero if s0f is None else s0f
    sb = zero if s0b is None else s0b
    qs = [q[i * c:(i + 1) * c] for i in range(n)]
    ks = [k[i * c:(i + 1) * c] * SCALE for i in range(n)]
    vs = [v[i * c:(i + 1) * c].astype(BF16) for i in range(n)]
    outs = []
    for i in range(n):
        kb = ks[i].astype(BF16)
        p0 = (_dot_nt(jnp.where(lo, qs[i], 0.0).astype(BF16), kb) * decay[0]).astype(BF16)
        p1 = (_dot_nt(jnp.where(lo, 0.0, qs[i]).astype(BF16), kb) * decay[1]).astype(BF16)
        o = jnp.where(lo, _dot(p0, vs[i]), _dot(p1, vs[i]))
        o = o + _dot((qs[i] * qw_f).astype(BF16), sf.astype(BF16))
        sf = gc_f * sf + jnp.where(bd, _dot((ks[i] * kw_f).T.astype(BF16), vs[i]), 0.0)
        outs.append(o)
    for i in reversed(range(n)):
        outs[i] = outs[i] + _dot((qs[i] * qw_b).astype(BF16), sb.astype(BF16))
        sb = gc_b * sb + jnp.where(bd, _dot((ks[i] * kw_b).T.astype(BF16), vs[i]), 0.0)
    o = jnp.concatenate(outs, axis=0) if n > 1 else outs[0]
    lo_s = _lane_lo((seq, LANES))
    inv_d = 1.0 / HEAD_DIM
    mu = jnp.where(lo_s, jnp.sum(jnp.where(lo_s, o, 0.0), axis=-1, keepdims=True),
                   jnp.sum(jnp.where(lo_s, 0.0, o), axis=-1, keepdims=True)) * inv_d
    d = o - mu
    d2 = d * d
    var = jnp.where(lo_s, jnp.sum(jnp.where(lo_s, d2, 0.0), axis=-1, keepdims=True),
                    jnp.sum(jnp.where(lo_s, 0.0, d2), axis=-1, keepdims=True)) * inv_d
    y = d * lax.rsqrt(var + EPS) * (g * jax.nn.sigmoid(g))
    return y, sf, sb


N_STATES = 6


def _ctx_mixer_kernel(lgf_ref, lgb_ref, sink_ref, p_ref, o_in_ref, *refs, n_prev):
    prev_refs = refs[:N_STATES] if n_prev else ()
    o_ref, sf_ref, sb_ref, nk_ref, nv_ref, wk_ref, wv_ref = refs[len(prev_refs):]
    for prev, cur in zip(prev_refs, (sf_ref, sb_ref, nk_ref, nv_ref, wk_ref, wv_ref)):
        cur[:n_prev] = prev[...]
    l = n_prev

    def col(blk):
        return p_ref[:, blk * LANES:(blk + 1) * LANES]

    for hp in range(N_RET_PAIRS):
        y, sf, sb = _ret_pair(col(CB_RQ + hp), col(CB_RK + hp), col(CB_RV + hp), col(CB_RG + hp),
                              (lgf_ref[2 * hp], lgf_ref[2 * hp + 1]),
                              (lgb_ref[2 * hp], lgb_ref[2 * hp + 1]), None, None, SEQ)
        o_ref[:, (OB_RET + hp) * LANES:(OB_RET + hp + 1) * LANES] = y.astype(o_ref.dtype)
        sf_ref[l, 2 * hp] = sf[:HEAD_DIM, :HEAD_DIM]
        sf_ref[l, 2 * hp + 1] = sf[HEAD_DIM:, HEAD_DIM:]
        sb_ref[l, 2 * hp] = sb[:HEAD_DIM, :HEAD_DIM]
        sb_ref[l, 2 * hp + 1] = sb[HEAD_DIM:, HEAD_DIM:]

    for hp in range(N_NA_PAIRS):
        k = col(CB_NK + hp)
        v = col(CB_NV + hp)
        o = _attn_pair(col(CB_NQ + hp) * SCALE, [k.astype(BF16)], [v.astype(BF16)], [None], None)
        o_ref[:, (OB_NA + hp) * LANES:(OB_NA + hp + 1) * LANES] = o.astype(o_ref.dtype)
        nk_ref[l, 2 * hp] = k[:, :HEAD_DIM]
        nk_ref[l, 2 * hp + 1] = k[:, HEAD_DIM:]
        nv_ref[l, 2 * hp] = v[:, :HEAD_DIM]
        nv_ref[l, 2 * hp + 1] = v[:, HEAD_DIM:]

    k = col(CB_WK)
    v = col(CB_WV)
    for kv in range(KV_WIN):
        wk_ref[l, kv] = k[:, kv * HEAD_DIM:(kv + 1) * HEAD_DIM]
        wv_ref[l, kv] = v[:, kv * HEAD_DIM:(kv + 1) * HEAD_DIM]
    lo = _lane_lo(k.shape)
    k_sw = pltpu.roll(k, HEAD_DIM, 1)
    v_sw = pltpu.roll(v, HEAD_DIM, 1)
    for hp in range(N_WIN_PAIRS):
        kv_lo = (2 * hp) // G_WIN
        kv_hi = (2 * hp + 1) // G_WIN
        kk = jnp.where(lo, k if kv_lo == 0 else k_sw, k if kv_hi == 1 else k_sw).astype(BF16)
        vv = jnp.where(lo, v if kv_lo == 0 else v_sw, v if kv_hi == 1 else v_sw).astype(BF16)
        o = _attn_pair(col(CB_WQ + hp) * SCALE, [kk], [vv], [None],
                       (sink_ref[2 * hp], sink_ref[2 * hp + 1]))
        o_ref[:, (OB_WIN + hp) * LANES:(OB_WIN + hp + 1) * LANES] = o.astype(o_ref.dtype)


def _smem_spec():
    return pl.BlockSpec(memory_space=pltpu.SMEM)


def _ctx_mixer(proj, lgf, lgb, sink, n_ctx, prev_states):
    t = proj.shape[0]
    n_prev = prev_states[0].shape[1] if prev_states else 0
    o_init = jnp.zeros((t, W_MIX), BF16)
    dims = [(H_RET, HEAD_DIM, HEAD_DIM)] * 2 + [(H_NA, SEQ, HEAD_DIM)] * 2 + [(KV_WIN, SEQ, HEAD_DIM)] * 2
    st = lambda n, d: pl.BlockSpec((None, n) + d, lambda i: (i, 0, 0, 0, 0))
    return pl.pallas_call(
        functools.partial(_ctx_mixer_kernel, n_prev=n_prev),
        out_shape=(jax.ShapeDtypeStruct((t, W_MIX), BF16),
                   *[jax.ShapeDtypeStruct((n_ctx, n_prev + 1) + d, F32) for d in dims]),
        grid=(n_ctx,),
        in_specs=[_smem_spec(), _smem_spec(), _smem_spec(),
                  pl.BlockSpec((SEQ, W_IN), lambda i: (i, 0)), pl.BlockSpec(memory_space=pl.ANY),
                  *[st(n_prev, d) for d in dims[:len(prev_states)]]],
        out_specs=(pl.BlockSpec((SEQ, W_MIX), lambda i: (i, 0)), *[st(n_prev + 1, d) for d in dims]),
        input_output_aliases={4: 0},
        compiler_params=_cparams(("parallel",)), name="ctx_mixer",
    )(lgf, lgb, sink, proj, o_init, *prev_states)


def _ret_lat_kernel(lgf_ref, lgb_ref, q_ref, k_ref, v_ref, g_ref, s0f_ref, s0b_ref, o_in_ref, o_ref):
    hp = pl.program_id(1)
    y, _, _ = _ret_pair(q_ref[...], k_ref[...], v_ref[...], g_ref[...],
                        (lgf_ref[2 * hp], lgf_ref[2 * hp + 1]), (lgb_ref[2 * hp], lgb_ref[2 * hp + 1]),
                        _block_diag(s0f_ref[0], s0f_ref[1]), _block_diag(s0b_ref[0], s0b_ref[1]), DEC_SEQ)
    o_ref[...] = y.astype(o_ref.dtype)


def _ret_latent(proj, o, lgf, lgb, s0f, s0b, l, n_lat):
    rb0 = (proj.shape[0] - n_lat * DEC_SEQ) // DEC_SEQ
    cb = lambda c0: pl.BlockSpec((DEC_SEQ, LANES), lambda b, hp: (rb0 + b, c0 + hp))
    st = pl.BlockSpec((None, None, 2, HEAD_DIM, HEAD_DIM), lambda b, hp: (b, l, hp, 0, 0))
    return pl.pallas_call(
        _ret_lat_kernel, out_shape=jax.ShapeDtypeStruct(o.shape, o.dtype),
        grid=(n_lat, N_RET_PAIRS),
        in_specs=[_smem_spec(), _smem_spec(), cb(CB_RQ), cb(CB_RK), cb(CB_RV), cb(CB_RG), st, st,
                  pl.BlockSpec(memory_space=pl.ANY)],
        out_specs=pl.BlockSpec((DEC_SEQ, LANES), lambda b, hp: (rb0 + b, OB_RET + hp)),
        input_output_aliases={8: 0},
        compiler_params=_cparams(("parallel", "parallel")), name="ret_latent",
    )(lgf, lgb, proj, proj, proj, proj, s0f, s0b, o)


def _na_lat_kernel(q_ref, k_ref, v_ref, kc_ref, vc_ref, bias_ref, o_in_ref, o_ref):
    kc = jnp.concatenate([kc_ref[0], kc_ref[1]], axis=1).astype(BF16)
    vc = jnp.concatenate([vc_ref[0], vc_ref[1]], axis=1).astype(BF16)

    def body(r, carry):
        r0 = jnp.clip(r - NA_ROWS // 2, 0, LAT_ROWS - NA_ROWS)
        q0 = pl.multiple_of(r * GRID_W, GRID_W)
        k0 = pl.multiple_of(r0 * GRID_W, GRID_W)
        q = q_ref[pl.ds(q0, GRID_W), :] * SCALE
        kw = k_ref[pl.ds(k0, NA_WIN_KEYS), :].astype(BF16)
        vw = v_ref[pl.ds(k0, NA_WIN_KEYS), :].astype(BF16)
        dr0 = r0 - r + (NA_ROWS - 1)
        bias = [jnp.concatenate([bias_ref[h, dr0 + 2 * m] for m in range(NA_ROWS // 2)], axis=1)
                for h in (0, 1)]
        o = _attn_pair(q, [kw, kc], [vw, vc], [bias, None], None)
        o_ref[pl.ds(q0, GRID_W), :] = o.astype(o_ref.dtype)
        return carry

    lax.fori_loop(0, LAT_ROWS, body, 0)


N_ROW_OFFS = 2 * NA_ROWS - 1


def _na_bias_table(rpb):
    cols = np.arange(GRID_W)
    c0 = np.clip(cols - NA_COLS // 2, 0, GRID_W - NA_COLS)
    col_ok = (cols[None, :] >= c0[:, None]) & (cols[None, :] < c0[:, None] + NA_COLS)
    dc = np.clip(cols[None, :] - cols[:, None], -(NA_COLS - 1), NA_COLS - 1) + (NA_COLS - 1)
    by_col = jnp.where(col_ok[None, None], rpb.astype(F32)[:, :, dc], NEG)
    return jnp.concatenate([by_col[:, :-1], by_col[:, 1:]], axis=-1)


def _na_latent(proj, o, kc, vc, bias, l, n_lat):
    rb0 = (proj.shape[0] - n_lat * DEC_SEQ) // DEC_SEQ
    cb = lambda c0: pl.BlockSpec((DEC_SEQ, LANES), lambda b, hp: (rb0 + b, c0 + hp))
    cache = pl.BlockSpec((None, None, 2, PAST_LEN, HEAD_DIM), lambda b, hp: (b, l, hp, 0, 0))
    return pl.pallas_call(
        _na_lat_kernel, out_shape=jax.ShapeDtypeStruct(o.shape, o.dtype),
        grid=(n_lat, N_NA_PAIRS),
        in_specs=[cb(CB_NQ), cb(CB_NK), cb(CB_NV), cache, cache,
                  pl.BlockSpec((2, N_ROW_OFFS - 1, GRID_W, LANES), lambda b, hp: (hp, 0, 0, 0)),
                  pl.BlockSpec(memory_space=pl.ANY)],
        out_specs=pl.BlockSpec((DEC_SEQ, LANES), lambda b, hp: (rb0 + b, OB_NA + hp)),
        input_output_aliases={6: 0},
        compiler_params=_cparams(("parallel", "parallel")), name="na_latent",
    )(proj, proj, proj, kc, vc, bias, o)


def _rope_tables():
    t = jnp.arange(DEC_SEQ)
    d = np.arange(LANES) % HEAD_DIM
    quarter = HEAD_DIM // 4
    inv = ROPE_BASE ** (-jnp.arange(quarter, dtype=F32) / quarter)
    pos = jnp.where(jnp.asarray(d < HEAD_DIM // 2)[None, :], (t // GRID_W)[:, None], (t % GRID_W)[:, None])
    ang = pos.astype(F32) * inv[d % quarter][None, :]
    sign = jnp.asarray(np.where((d & quarter) == 0, -1.0, 1.0), F32)
    return jnp.cos(ang), jnp.sin(ang) * sign[None, :]


def _win_lat_kernel(sink_ref, q_ref, k_ref, v_ref, kc_ref, vc_ref, cos_ref, sin_ref, o_in_ref, o_ref,
                    qs_ref, kp_ref, vp_ref):
    hp = pl.program_id(1)
    quarter = HEAD_DIM // 4
    lane = lax.broadcasted_iota(jnp.int32, (DEC_SEQ, LANES), 1)
    first = (lane & quarter) == 0
    lo = lane < HEAD_DIM
    cos = cos_ref[...]
    sin = sin_ref[...]

    def rope(x):
        sw = jnp.where(first, pltpu.roll(x, LANES - quarter, 1), pltpu.roll(x, quarter, 1))
        return x * cos + sw * sin

    qs_ref[...] = rope(q_ref[...]) * SCALE
    k = rope(k_ref[...])
    v = v_ref[...]
    lo_orig = (2 * hp) // G_WIN == 0
    hi_orig = (2 * hp + 1) // G_WIN == 1
    orig = jnp.where(lo, lo_orig.astype(jnp.int32), hi_orig.astype(jnp.int32)) == 1
    zeros = jnp.zeros((WIN_BLOCK, LANES), BF16)
    kp_ref[:WIN_BLOCK] = zeros
    kp_ref[WIN_BLOCK + DEC_SEQ:] = zeros
    vp_ref[:WIN_BLOCK] = zeros
    vp_ref[WIN_BLOCK + DEC_SEQ:] = zeros
    kp_ref[WIN_BLOCK:WIN_BLOCK + DEC_SEQ] = jnp.where(orig, k, pltpu.roll(k, HEAD_DIM, 1)).astype(BF16)
    vp_ref[WIN_BLOCK:WIN_BLOCK + DEC_SEQ] = jnp.where(orig, v, pltpu.roll(v, HEAD_DIM, 1)).astype(BF16)
    kc = jnp.concatenate([jnp.where(lo_orig, kc_ref[0], kc_ref[1]),
                          jnp.where(hi_orig, kc_ref[1], kc_ref[0])], axis=1).astype(BF16)
    vc = jnp.concatenate([jnp.where(lo_orig, vc_ref[0], vc_ref[1]),
                          jnp.where(hi_orig, vc_ref[1], vc_ref[0])], axis=1).astype(BF16)
    sinks = (sink_ref[2 * hp], sink_ref[2 * hp + 1])
    n_band = 3 * WIN_BLOCK
    qi = lax.broadcasted_iota(jnp.int32, (WIN_BLOCK, n_band), 0)
    kj = lax.broadcasted_iota(jnp.int32, (WIN_BLOCK, n_band), 1)
    near = jnp.abs(qi + WIN_BLOCK - kj) <= WIN_HALF

    def body(n, carry):
        q0 = pl.multiple_of(n * WIN_BLOCK, WIN_BLOCK)
        kpos = (n - 1) * WIN_BLOCK + kj
        ok = jnp.logical_and(near, jnp.logical_and(kpos >= 0, kpos < DEC_SEQ))
        band = jnp.where(ok, 0.0, NEG)
        o = _attn_pair(qs_ref[pl.ds(q0, WIN_BLOCK), :],
                       [kp_ref[pl.ds(q0, n_band), :], kc], [vp_ref[pl.ds(q0, n_band), :], vc],
                       [(band, band), None], sinks)
        o_ref[pl.ds(q0, WIN_BLOCK), :] = o.astype(o_ref.dtype)
        return carry

    lax.fori_loop(0, DEC_SEQ // WIN_BLOCK, body, 0)


def _win_latent(proj, o, kc, vc, sink, cos, sin, l, n_lat):
    rb0 = (proj.shape[0] - n_lat * DEC_SEQ) // DEC_SEQ
    cache = pl.BlockSpec((None, None, KV_WIN, PAST_LEN, HEAD_DIM), lambda b, hp: (b, l, 0, 0, 0))
    tbl = pl.BlockSpec((DEC_SEQ, LANES), lambda b, hp: (0, 0))
    return pl.pallas_call(
        _win_lat_kernel, out_shape=jax.ShapeDtypeStruct(o.shape, o.dtype),
        grid=(n_lat, N_WIN_PAIRS),
        in_specs=[_smem_spec(),
                  pl.BlockSpec((DEC_SEQ, LANES), lambda b, hp: (rb0 + b, CB_WQ + hp)),
                  pl.BlockSpec((DEC_SEQ, LANES), lambda b, hp: (rb0 + b, CB_WK)),
                  pl.BlockSpec((DEC_SEQ, LANES), lambda b, hp: (rb0 + b, CB_WV)),
                  cache, cache, tbl, tbl, pl.BlockSpec(memory_space=pl.ANY)],
        out_specs=pl.BlockSpec((DEC_SEQ, LANES), lambda b, hp: (rb0 + b, OB_WIN + hp)),
        scratch_shapes=[pltpu.VMEM((DEC_SEQ, LANES), F32),
                        pltpu.VMEM((DEC_SEQ + 2 * WIN_BLOCK, LANES), BF16),
                        pltpu.VMEM((DEC_SEQ + 2 * WIN_BLOCK, LANES), BF16)],
        input_output_aliases={8: 0},
        compiler_params=_cparams(("parallel", "parallel")), name="win_latent",
    )(sink, proj, proj, proj, kc, vc, cos, sin, o)


def _mixers(proj, l, n_ctx, n_lat, prev_states, state_ret_fwd, state_ret_bwd, cache_na_k, cache_na_v,
            cache_win_k, cache_win_v, ret_decay_fwd, ret_decay_bwd, na_rpb, win_sink):
    lgf = jax.nn.log_sigmoid(ret_decay_fwd[l].astype(F32))
    lgb = jax.nn.log_sigmoid(ret_decay_bwd[l].astype(F32))
    sink = win_sink[l].astype(F32)
    o, *states = _ctx_mixer(proj, lgf, lgb, sink, n_ctx, prev_states)
    o = _ret_latent(proj, o, lgf, lgb, state_ret_fwd, state_ret_bwd, l, n_lat)
    o = _na_latent(proj, o, cache_na_k, cache_na_v, _na_bias_table(na_rpb[l]), l, n_lat)
    cos, sin = _rope_tables()
    o = _win_latent(proj, o, cache_win_k, cache_win_v, sink, cos, sin, l, n_lat)
    return o, tuple(states)


def kernel(x_prompt, x_sample, c, state_ret_fwd, state_ret_bwd, cache_na_k, cache_na_v, cache_win_k, cache_win_v, c_ctx, norm1_g, norm2_g, ada_w, ada_b, w_in, w_out, ret_decay_fwd, ret_decay_bwd, na_rpb, win_sink, ffn_w_in, ffn_w_out, moe_router, moe_w_in, moe_w_out, final_norm_g):
    x = jnp.concatenate([x_prompt.reshape(T_CTX, D_MODEL), x_sample.reshape(T_LAT, D_MODEL)], axis=0)
    cond = jnp.concatenate([c_ctx[None, :], c, jnp.zeros((N_GROUPS - 1 - DEC_BATCH, D_MODEL), F32)], axis=0)
    states = ()
    for l in range(DEPTH):
        mod = _adaln(cond, ada_w, ada_b, l)
        h = _norm_mod(x, norm1_g, mod, l, 0, 1)
        proj = _matmul(h, w_in, l)
        o, states = _mixers(proj, l, BATCH, DEC_BATCH, states, state_ret_fwd, state_ret_bwd, cache_na_k, cache_na_v,
                         cache_win_k, cache_win_v, ret_decay_fwd, ret_decay_bwd, na_rpb, win_sink)
        x = _matmul(o, w_out, l, resid=x, mod=mod, k_gate=2)
        i = l // 2
        if l % 2 == 0:
            h = _norm_mod(x, norm2_g, mod, l, 3, 4)
            x = _resid(x, _ffn_dense(h, ffn_w_in, ffn_w_out, i, D_FF, FFN_TM), mod, 5)
        else:
            router = jnp.pad(moe_router[i], ((0, 0), (0, LANES - N_EXPERTS)))
            h2, route, counts = _norm_mod(x, norm2_g, mod, l, 3, 4, router=router)
            x = _moe(x, h2, route, counts, moe_w_in[i], moe_w_out[i], mod, 5)
    y_prompt = _final_norm(x, final_norm_g, 0, T_CTX).reshape(BATCH, SEQ, D_MODEL)
    y_sample = _final_norm(x, final_norm_g, T_CTX, T_LAT).reshape(DEC_BATCH, DEC_SEQ, D_MODEL)
    return (y_prompt, y_sample, *states)
```

```python
import functools

import jax
import jax.numpy as jnp
from jax import lax
import numpy as np
from jax.experimental import pallas as pl
from jax.experimental.pallas import tpu as pltpu

D_MODEL = 2048
BATCH = 32
SEQ = 256
DEPTH = 2
DEC_BATCH = 4
DEC_SEQ = 1024
PAST_LEN = 256

GRID_W = 64
HEAD_DIM = 64
H_RET = 12
H_NA = 10
H_WIN = 10
KV_WIN = 2
G_WIN = H_WIN // KV_WIN
W_RET = H_RET * HEAD_DIM
W_NA = H_NA * HEAD_DIM
W_WIN = H_WIN * HEAD_DIM
W_MIX = W_RET + W_NA + W_WIN
W_IN = 4 * W_RET + 3 * W_NA + W_WIN + 2 * KV_WIN * HEAD_DIM
RET_CHUNK = 128
NA_ROWS = 8
NA_COLS = 16
WIN_HALF = 128
WIN_BLOCK = 128
ROPE_BASE = 10000.0
D_FF = 5632
N_EXPERTS = 8
TOP_K = 2
D_FF_EXPERT = 7168
EPS = 1e-6
NEG = -1e30
SCALE = HEAD_DIM ** -0.5

T_CTX = BATCH * SEQ
T_LAT = DEC_BATCH * DEC_SEQ
T_ALL = T_CTX + T_LAT
N_GROUPS = 8
LANES = 128

F32 = jnp.float32
BF16 = jnp.bfloat16

VMEM_LIMIT = 56 * 1024 * 1024


def _group_of_tile(i, tm):
    return jnp.maximum((i * tm - T_CTX) // DEC_SEQ + 1, 0)


def _cparams(sem):
    return pltpu.CompilerParams(dimension_semantics=sem, vmem_limit_bytes=VMEM_LIMIT)


def _adaln_kernel(c_ref, w_ref, b_ref, o_ref):
    c = c_ref[...]
    s = (c * jax.nn.sigmoid(c)).astype(BF16)
    o_ref[...] = jnp.dot(s, w_ref[...].astype(BF16), preferred_element_type=F32) + b_ref[...]


def _adaln(cond, ada_w, ada_b, l):
    tn = 1024
    n = 6 * D_MODEL
    out = pl.pallas_call(
        _adaln_kernel,
        out_shape=jax.ShapeDtypeStruct((N_GROUPS, n), F32),
        grid=(n // tn,),
        in_specs=[
            pl.BlockSpec((N_GROUPS, D_MODEL), lambda j: (0, 0)),
            pl.BlockSpec((None, D_MODEL, tn), lambda j: (l, 0, j)),
            pl.BlockSpec((None, 1, tn), lambda j: (l, 0, j)),
        ],
        out_specs=pl.BlockSpec((N_GROUPS, tn), lambda j: (0, j)),
        compiler_params=_cparams(("arbitrary",)),
        name="adaln",
    )(cond, ada_w, ada_b.reshape(DEPTH, 1, n))
    return out.reshape(N_GROUPS, 6, 1, D_MODEL)


N_SUB = D_MODEL // LANES


def _store_token_major(ref, val):
    rows = val.shape[0]
    for s in range(N_SUB):
        ref[pl.ds(s, rows, stride=N_SUB), :] = val[:, s * LANES:(s + 1) * LANES].astype(ref.dtype)


def _load_token_major(ref, rows, dtype=F32):
    return jnp.concatenate([ref[pl.ds(s, rows, stride=N_SUB), :].astype(dtype) for s in range(N_SUB)],
                           axis=1)


def _token_slab(ref, t):
    return ref.at[pl.ds(pl.multiple_of(t * N_SUB, N_SUB), N_SUB)]


def _norm_mod_body(x_ref, g_ref, sh_ref, sc_ref):
    x = x_ref[...]
    ms = jnp.mean(x * x, axis=-1, keepdims=True)
    y = x * lax.rsqrt(ms + EPS) * g_ref[...]
    return y * (1.0 + sc_ref[...]) + sh_ref[...]


def _norm_mod_kernel(x_ref, g_ref, sh_ref, sc_ref, h_ref):
    h_ref[...] = _norm_mod_body(x_ref, g_ref, sh_ref, sc_ref).astype(h_ref.dtype)


R_E0, R_E1, R_RANK0, R_RANK1, R_W0, R_W1 = range(6)


def _norm_mod_router_kernel(x_ref, g_ref, sh_ref, sc_ref, r_ref, h2_ref, route_ref, cnt_ref, run_ref):
    i = pl.program_id(0)

    @pl.when(i == 0)
    def _():
        run_ref[...] = jnp.zeros_like(run_ref)

    h = _norm_mod_body(x_ref, g_ref, sh_ref, sc_ref)
    _store_token_major(h2_ref, h)
    tm = h.shape[0]
    lg = jnp.dot(h, r_ref[...], preferred_element_type=F32, precision=lax.Precision.HIGHEST)
    lane = lax.broadcasted_iota(jnp.int32, (tm, LANES), 1)
    lane_f = lane.astype(F32)
    lg = jnp.where(lane < N_EXPERTS, lg, -jnp.inf)
    v0 = jnp.max(lg, axis=-1, keepdims=True)
    e0 = jnp.min(jnp.where(lg == v0, lane_f, float(LANES)), axis=-1, keepdims=True)
    lg1 = jnp.where(lane_f == e0, -jnp.inf, lg)
    v1 = jnp.max(lg1, axis=-1, keepdims=True)
    e1 = jnp.min(jnp.where(lg1 == v1, lane_f, float(LANES)), axis=-1, keepdims=True)
    ex = jnp.exp(v1 - v0)
    w0 = 1.0 / (1.0 + ex)
    w1 = ex / (1.0 + ex)
    oh0 = jnp.where(lane_f == e0, 1.0, 0.0)
    oh1 = jnp.where(lane_f == e1, 1.0, 0.0)
    oh = oh0 + oh1
    earlier = (lax.broadcasted_iota(jnp.int32, (tm, tm), 0)
               > lax.broadcasted_iota(jnp.int32, (tm, tm), 1))
    before = jnp.dot(jnp.where(earlier, 1.0, 0.0).astype(BF16), oh.astype(BF16),
                     preferred_element_type=F32) + run_ref[0:1, :]
    rank0 = jnp.sum(oh0 * before, axis=-1, keepdims=True)
    rank1 = jnp.sum(oh1 * before, axis=-1, keepdims=True)
    rec = jnp.zeros((tm, LANES), F32)
    for k, val in ((R_E0, e0), (R_E1, e1), (R_RANK0, rank0), (R_RANK1, rank1), (R_W0, w0), (R_W1, w1)):
        rec = jnp.where(lane == k, val, rec)
    route_ref[...] = rec
    run_ref[0:1, :] = run_ref[0:1, :] + jnp.sum(oh, axis=0, keepdims=True)
    cnt_ref[...] = run_ref[...]


def _norm_mod(x, g, mod, l, k_shift, k_scale, *, router=None):
    tm = 512
    g3 = g.reshape(DEPTH, 1, D_MODEL)
    in_specs = [
        pl.BlockSpec((tm, D_MODEL), lambda i: (i, 0)),
        pl.BlockSpec((None, 1, D_MODEL), lambda i: (l, 0, 0)),
        pl.BlockSpec((None, None, 1, D_MODEL), lambda i: (_group_of_tile(i, tm), k_shift, 0, 0)),
        pl.BlockSpec((None, None, 1, D_MODEL), lambda i: (_group_of_tile(i, tm), k_scale, 0, 0)),
    ]
    if router is not None:
        in_specs.append(pl.BlockSpec((D_MODEL, LANES), lambda i: (0, 0)))
        return pl.pallas_call(
            _norm_mod_router_kernel,
            out_shape=(jax.ShapeDtypeStruct((T_ALL * N_SUB, LANES), F32),
                       jax.ShapeDtypeStruct((T_ALL, LANES), F32),
                       jax.ShapeDtypeStruct((8, LANES), F32)),
            grid=(T_ALL // tm,),
            in_specs=in_specs,
            out_specs=(pl.BlockSpec((tm * N_SUB, LANES), lambda i: (i, 0)),
                       pl.BlockSpec((tm, LANES), lambda i: (i, 0)),
                       pl.BlockSpec((8, LANES), lambda i: (0, 0))),
            scratch_shapes=[pltpu.VMEM((8, LANES), F32)],
            compiler_params=_cparams(("arbitrary",)), name="norm_mod_router",
        )(x, g3, mod, mod, router)
    return pl.pallas_call(
        _norm_mod_kernel, out_shape=jax.ShapeDtypeStruct((T_ALL, D_MODEL), BF16), grid=(T_ALL // tm,),
        in_specs=in_specs, out_specs=pl.BlockSpec((tm, D_MODEL), lambda i: (i, 0)),
        compiler_params=_cparams(("parallel",)), name="norm_mod",
    )(x, g3, mod, mod)


def _mm_kernel(a_ref, w_ref, o_ref):
    o_ref[...] = jnp.dot(a_ref[...].astype(BF16), w_ref[...].astype(BF16),
                         preferred_element_type=F32)


def _mm_res_kernel(a_ref, w_ref, x_ref, gate_ref, o_ref):
    acc = jnp.dot(a_ref[...].astype(BF16), w_ref[...].astype(BF16),
                  preferred_element_type=F32)
    o_ref[...] = x_ref[...] + gate_ref[...] * acc


def _matmul(a, w, l, *, resid=None, mod=None, k_gate=None):
    tm, tn = (2048 if resid is None else DEC_SEQ), 512
    t, k = a.shape
    n = w.shape[-1]
    grid = (t // tm, pl.cdiv(n, tn))
    in_specs = [
        pl.BlockSpec((tm, k), lambda i, j: (i, 0)),
        pl.BlockSpec((None, k, tn), lambda i, j: (l, 0, j)),
    ]
    args = [a, w]
    kern = _mm_kernel
    if resid is not None:
        in_specs += [
            pl.BlockSpec((tm, tn), lambda i, j: (i, j)),
            pl.BlockSpec((None, None, 1, tn), lambda i, j: (_group_of_tile(i, tm), k_gate, 0, j)),
        ]
        args += [resid, mod]
        kern = _mm_res_kernel
    return pl.pallas_call(
        kern, out_shape=jax.ShapeDtypeStruct((t, n), F32), grid=grid,
        in_specs=in_specs, out_specs=pl.BlockSpec((tm, tn), lambda i, j: (i, j)),
        compiler_params=_cparams(("parallel", "arbitrary")), name="proj",
    )(*args)


FFN_TF = 256


def _ffn_step(h_ref, wa_ref, wg_ref, wo_ref, acc_ref, j, n_rows=None):
    @pl.when(j == 0)
    def _():
        acc_ref[...] = jnp.zeros_like(acc_ref)

    def rows(r0, n):
        h = h_ref[r0:r0 + n, :]
        a = jnp.dot(h, wa_ref[...].astype(BF16), preferred_element_type=F32)
        g = jnp.dot(h, wg_ref[...].astype(BF16), preferred_element_type=F32)
        act = (a * jax.nn.sigmoid(a) * g).astype(BF16)
        acc_ref[r0:r0 + n, :] += jnp.dot(act, wo_ref[...].astype(BF16), preferred_element_type=F32)

    tm = h_ref.shape[0]
    if n_rows is None:
        rows(0, tm)
    else:
        rows(0, tm // 2)

        @pl.when(n_rows > tm // 2)
        def _():
            rows(tm // 2, tm // 2)


def _ffn_dense_kernel(h_ref, wa_ref, wg_ref, wo_ref, o_ref):
    _ffn_step(h_ref, wa_ref, wg_ref, wo_ref, o_ref, pl.program_id(1))


def _ffn_dense(h, w_in, w_out, e, d_ff, tm):
    tf = FFN_TF
    n_f = d_ff // tf
    return pl.pallas_call(
        _ffn_dense_kernel,
        out_shape=jax.ShapeDtypeStruct((h.shape[0], D_MODEL), F32),
        grid=(h.shape[0] // tm, n_f),
        in_specs=[
            pl.BlockSpec((tm, D_MODEL), lambda i, j: (i, 0)),
            pl.BlockSpec((None, D_MODEL, tf), lambda i, j: (e, 0, j)),
            pl.BlockSpec((None, D_MODEL, tf), lambda i, j: (e, 0, j + n_f)),
            pl.BlockSpec((None, tf, D_MODEL), lambda i, j: (e, j, 0)),
        ],
        out_specs=pl.BlockSpec((tm, D_MODEL), lambda i, j: (i, 0)),
        compiler_params=_cparams(("parallel", "arbitrary")), name="ffn_dense",
    )(h, w_in, w_in, w_out)


N_DMA_PRIORITIES = 2


def _ffn_expert_kernel(te_ref, nt_ref, nrows_ref, tok_ref, h2_ref, wa_ref, wg_ref, wo_ref, o_ref,
                       gbuf_ref, hb_ref, acc_ref, sem, *, n_f, tm):
    i = pl.program_id(0)
    j = pl.program_id(1)
    valid = i < nt_ref[0]

    def start_gather(tile):
        def body(p, c):
            for u in range(N_DMA_PRIORITIES):
                r = p * N_DMA_PRIORITIES + u
                pltpu.make_async_copy(_token_slab(h2_ref, tok_ref[tile * tm + r]),
                                      _token_slab(gbuf_ref, r), sem).start(priority=u)
            return c
        lax.fori_loop(0, tm // N_DMA_PRIORITIES, body, 0, unroll=4)

    @pl.when(jnp.logical_and(valid, j == 0))
    def _():
        @pl.when(i == 0)
        def _():
            start_gather(0)

        pltpu.make_async_copy(h2_ref.at[pl.ds(0, tm * N_SUB)], gbuf_ref, sem).wait()
        for s in range(N_SUB):
            hb_ref[:, s * LANES:(s + 1) * LANES] = gbuf_ref[pl.ds(s, tm, stride=N_SUB), :].astype(BF16)

        @pl.when(i + 1 < nt_ref[0])
        def _():
            start_gather(i + 1)

    @pl.when(valid)
    def _():
        _ffn_step(hb_ref, wa_ref, wg_ref, wo_ref, acc_ref, j, nrows_ref[i])

        @pl.when(j == n_f - 1)
        def _():
            _store_token_major(o_ref, acc_ref[...])

    @pl.when(jnp.logical_and(jnp.logical_not(valid), j == 0))
    def _():
        o_ref[...] = jnp.zeros_like(o_ref)


def _ffn_expert(h2, row_token, te, nt, tile_rows, w_in, w_out, d_ff, tm):
    tf = FFN_TF
    r = row_token.shape[0]
    n_f = d_ff // tf

    def jj(i, j, nt_ref):
        return jnp.where(i < nt_ref[0], j, n_f - 1)

    grid_spec = pltpu.PrefetchScalarGridSpec(
        num_scalar_prefetch=4,
        grid=(r // tm, n_f),
        in_specs=[
            pl.BlockSpec(memory_space=pl.ANY),
            pl.BlockSpec((None, D_MODEL, tf), lambda i, j, te_ref, nt_ref, *_: (te_ref[i], 0, jj(i, j, nt_ref))),
            pl.BlockSpec((None, D_MODEL, tf), lambda i, j, te_ref, nt_ref, *_: (te_ref[i], 0, jj(i, j, nt_ref) + n_f)),
            pl.BlockSpec((None, tf, D_MODEL), lambda i, j, te_ref, nt_ref, *_: (te_ref[i], jj(i, j, nt_ref), 0)),
        ],
        out_specs=pl.BlockSpec((tm * N_SUB, LANES), lambda i, j, *_: (i, 0),
                               pipeline_mode=pl.Buffered(1)),
        scratch_shapes=[pltpu.VMEM((tm * N_SUB, LANES), F32), pltpu.VMEM((tm, D_MODEL), BF16),
                        pltpu.VMEM((tm, D_MODEL), F32), pltpu.SemaphoreType.DMA],
    )
    return pl.pallas_call(
        functools.partial(_ffn_expert_kernel, n_f=n_f, tm=tm),
        out_shape=jax.ShapeDtypeStruct((r * N_SUB, LANES), F32),
        grid_spec=grid_spec,
        compiler_params=_cparams(("arbitrary", "arbitrary")), name="ffn_expert",
    )(te, nt, tile_rows, row_token, h2, w_in, w_in, w_out)


def _resid_kernel(x_ref, y_ref, gate_ref, o_ref):
    o_ref[...] = x_ref[...] + gate_ref[...] * y_ref[...]


def _resid(x, y, mod, k_gate):
    tm = 512
    return pl.pallas_call(
        _resid_kernel,
        out_shape=jax.ShapeDtypeStruct((T_ALL, D_MODEL), F32), grid=(T_ALL // tm,),
        in_specs=[
            pl.BlockSpec((tm, D_MODEL), lambda i: (i, 0)),
            pl.BlockSpec((tm, D_MODEL), lambda i: (i, 0)),
            pl.BlockSpec((None, None, 1, D_MODEL), lambda i: (_group_of_tile(i, tm), k_gate, 0, 0)),
        ],
        out_specs=pl.BlockSpec((tm, D_MODEL), lambda i: (i, 0)),
        compiler_params=_cparams(("parallel",)), name="resid",
    )(x, y, mod)


COMBINE_TM = 256


def _combine_kernel(eid_ref, rank_ref, first_row_ref, x_ref, route_ref, gate_ref, ys2_ref, o_ref,
                    buf_ref, sem):
    i = pl.program_id(0)
    n = pl.num_programs(0)
    tm = COMBINE_TM

    def start_tile(tile, slot):
        def start(t, c):
            for k in range(TOP_K):
                a = (tile * tm + t) * TOP_K + k
                row = first_row_ref[eid_ref[a]] + rank_ref[a]
                pltpu.make_async_copy(_token_slab(ys2_ref, row), _token_slab(buf_ref.at[slot, k], t),
                                      sem.at[slot]).start(priority=k % N_DMA_PRIORITIES)
            return c
        lax.fori_loop(0, tm, start, 0, unroll=4)

    @pl.when(i == 0)
    def _():
        start_tile(0, 0)

    @pl.when(i + 1 < n)
    def _():
        start_tile(i + 1, (i + 1) % 2)

    slot = i % 2
    for k in range(TOP_K):
        pltpu.make_async_copy(ys2_ref.at[pl.ds(0, tm * N_SUB)], buf_ref.at[slot, k], sem.at[slot]).wait()

    rec = route_ref[...]
    w0 = rec[:, R_W0:R_W0 + 1]
    w1 = rec[:, R_W1:R_W1 + 1]
    y = (w0 * _load_token_major(buf_ref.at[slot, 0], tm) + w1 * _load_token_major(buf_ref.at[slot, 1], tm))
    o_ref[...] = x_ref[...] + gate_ref[...] * y


def _combine(x, ys2, route, eid, rank, first_row, mod, k_gate):
    tm = COMBINE_TM
    grid_spec = pltpu.PrefetchScalarGridSpec(
        num_scalar_prefetch=3,
        grid=(T_ALL // tm,),
        in_specs=[
            pl.BlockSpec((tm, D_MODEL), lambda i, *_: (i, 0)),
            pl.BlockSpec((tm, LANES), lambda i, *_: (i, 0)),
            pl.BlockSpec((None, None, 1, D_MODEL), lambda i, *_: (_group_of_tile(i, tm), k_gate, 0, 0)),
            pl.BlockSpec(memory_space=pl.ANY),
        ],
        out_specs=pl.BlockSpec((tm, D_MODEL), lambda i, *_: (i, 0)),
        scratch_shapes=[pltpu.VMEM((2, TOP_K, tm * N_SUB, LANES), F32), pltpu.SemaphoreType.DMA((2,))],
    )
    return pl.pallas_call(
        _combine_kernel,
        out_shape=jax.ShapeDtypeStruct((T_ALL, D_MODEL), F32),
        grid_spec=grid_spec,
        compiler_params=_cparams(("arbitrary",)), name="moe_combine",
    )(eid, rank, first_row, x, route, mod, ys2)


def _final_norm_kernel(x_ref, g_ref, o_ref):
    x = x_ref[...]
    ms = jnp.mean(x * x, axis=-1, keepdims=True)
    o_ref[...] = x * lax.rsqrt(ms + EPS) * g_ref[...]


def _final_norm(x, g, row0, rows):
    tm = 512
    return pl.pallas_call(
        _final_norm_kernel,
        out_shape=jax.ShapeDtypeStruct((rows, D_MODEL), F32), grid=(rows // tm,),
        in_specs=[pl.BlockSpec((tm, D_MODEL), lambda i: (row0 // tm + i, 0)),
                  pl.BlockSpec((1, D_MODEL), lambda i: (0, 0))],
        out_specs=pl.BlockSpec((tm, D_MODEL), lambda i: (i, 0)),
        compiler_params=_cparams(("parallel",)), name="final_norm",
    )(x, g.reshape(1, D_MODEL))


FFN_TM = 1024
MOE_TM = 1024
MOE_TILES = (TOP_K * T_ALL) // MOE_TM + N_EXPERTS


def _moe(x, h2, route, counts, w_in, w_out, mod, k_gate):
    tm = MOE_TM
    eid = route[:, R_E0:R_E1 + 1].astype(jnp.int32).reshape(-1)
    rank = route[:, R_RANK0:R_RANK1 + 1].astype(jnp.int32).reshape(-1)
    counts = counts[0, :N_EXPERTS].astype(jnp.int32)
    ptiles = (counts + tm - 1) // tm
    tile_end = jnp.cumsum(ptiles)
    first_row = (tile_end - ptiles) * tm
    n_tiles = tile_end[-1]
    tile_ids = jnp.arange(MOE_TILES, dtype=jnp.int32)
    te = jnp.sum((tile_ids[:, None] >= tile_end[None, :]).astype(jnp.int32), axis=1)
    te = jnp.minimum(te, N_EXPERTS - 1)
    te_last = jnp.sum(jnp.where(tile_ids == n_tiles - 1, te, 0))
    te = jnp.where(tile_ids < n_tiles, te, te_last)
    experts = jnp.arange(N_EXPERTS, dtype=jnp.int32)
    row_end = jnp.sum(jnp.where(te[:, None] == experts[None, :], (first_row + counts)[None, :], 0), axis=1)
    tile_rows = jnp.clip(row_end - tile_ids * tm, 1, tm)
    pos = jnp.sum(jnp.where(eid[:, None] == experts[None, :], first_row[None, :], 0), axis=1) + rank
    tok = jnp.arange(TOP_K * T_ALL, dtype=jnp.int32) // TOP_K
    row_token = jnp.zeros((MOE_TILES * tm,), jnp.int32).at[pos].set(tok)
    ys2 = _ffn_expert(h2, row_token, te, n_tiles.reshape(1), tile_rows, w_in, w_out, D_FF_EXPERT, tm)
    return _combine(x, ys2, route, eid, rank, first_row, mod, k_gate)


CB_RQ, CB_RK, CB_RV, CB_RG = 0, 6, 12, 18
CB_NQ, CB_NK, CB_NV = 24, 29, 34
CB_WQ, CB_WK, CB_WV = 39, 44, 45
OB_RET, OB_NA, OB_WIN = 0, 6, 11
N_RET_PAIRS = H_RET // 2
N_NA_PAIRS = H_NA // 2
N_WIN_PAIRS = H_WIN // 2
NA_WIN_KEYS = NA_ROWS * GRID_W
LAT_ROWS = DEC_SEQ // GRID_W


def _lane_lo(shape):
    return lax.broadcasted_iota(jnp.int32, shape, len(shape) - 1) < HEAD_DIM


def _dot_nt(a, b):
    return lax.dot_general(a, b, (((1,), (1,)), ((), ())), preferred_element_type=F32)


def _dot(a, b):
    return jnp.dot(a, b, preferred_element_type=F32)


def _attn_pair(q, ks, vs, biases, sinks):
    lo = _lane_lo(q.shape)
    outs = []
    for half in (0, 1):
        qm = jnp.where(lo if half == 0 else jnp.logical_not(lo), q, 0.0).astype(BF16)
        ss = []
        for kb, bb in zip(ks, biases):
            s = _dot_nt(qm, kb)
            if bb is not None:
                s = s + bb[half]
            ss.append(s)
        m = jnp.max(ss[0], axis=-1, keepdims=True)
        for s in ss[1:]:
            m = jnp.maximum(m, jnp.max(s, axis=-1, keepdims=True))
        if sinks is not None:
            m = jnp.maximum(m, sinks[half])
        ps = [jnp.exp(s - m) for s in ss]
        den = jnp.sum(ps[0], axis=-1, keepdims=True)
        for p in ps[1:]:
            den = den + jnp.sum(p, axis=-1, keepdims=True)
        if sinks is not None:
            den = den + jnp.exp(sinks[half] - m)
        o = _dot(ps[0].astype(BF16), vs[0])
        for p, vb in zip(ps[1:], vs[1:]):
            o = o + _dot(p.astype(BF16), vb)
        outs.append(o / den)
    return jnp.where(lo, outs[0], outs[1])


def _block_diag(s0, s1):
    z = jnp.zeros_like(s0)
    return jnp.concatenate([jnp.concatenate([s0, z], axis=1), jnp.concatenate([z, s1], axis=1)], axis=0)


def _ret_pair(q, k, v, g, lgf, lgb, s0f, s0b, seq):
    c = RET_CHUNK
    n = seq // c
    lo1 = _lane_lo((1, LANES))
    lo = _lane_lo((c, LANES))
    lgf_v = jnp.where(lo1, lgf[0], lgf[1])
    lgb_v = jnp.where(lo1, lgb[0], lgb[1])
    pos = lax.broadcasted_iota(jnp.int32, (c, LANES), 0).astype(F32)
    qw_f = jnp.exp(lgf_v * (pos + 1.0))
    kw_f = jnp.exp(lgf_v * (c - 1.0 - pos))
    qw_b = jnp.exp(lgb_v * (c - pos))
    kw_b = jnp.exp(lgb_v * pos)
    gc_f = jnp.exp(lgf_v * float(c))
    gc_b = jnp.exp(lgb_v * float(c))
    diff = (lax.broadcasted_iota(jnp.int32, (c, c), 0)
            - lax.broadcasted_iota(jnp.int32, (c, c), 1)).astype(F32)
    decay = [jnp.where(diff >= 0, jnp.exp(lgf[h] * jnp.maximum(diff, 0.0)), 0.0)
             + jnp.where(diff <= 0, jnp.exp(lgb[h] * jnp.maximum(-diff, 0.0)), 0.0) for h in (0, 1)]
    bd = ((lax.broadcasted_iota(jnp.int32, (LANES, LANES), 0) < HEAD_DIM)
          == (lax.broadcasted_iota(jnp.int32, (LANES, LANES), 1) < HEAD_DIM))
    zero = jnp.zeros((LANES, LANES), F32)
    sf = zero if s0f is None else s0f
    sb = zero if s0b is None else s0b
    qs = [q[i * c:(i + 1) * c] for i in range(n)]
    ks = [k[i * c:(i + 1) * c] * SCALE for i in range(n)]
    vs = [v[i * c:(i + 1) * c].astype(BF16) for i in range(n)]
    outs = []
    for i in range(n):
        kb = ks[i].astype(BF16)
        p0 = (_dot_nt(jnp.where(lo, qs[i], 0.0).astype(BF16), kb) * decay[0]).astype(BF16)
        p1 = (_dot_nt(jnp.where(lo, 0.0, qs[i]).astype(BF16), kb) * decay[1]).astype(BF16)
        o = jnp.where(lo, _dot(p0, vs[i]), _dot(p1, vs[i]))
        o = o + _dot((qs[i] * qw_f).astype(BF16), sf.astype(BF16))
        sf = gc_f * sf + jnp.where(bd, _dot((ks[i] * kw_f).T.astype(BF16), vs[i]), 0.0)
        outs.append(o)
    for i in reversed(range(n)):
        outs[i] = outs[i] + _dot((qs[i] * qw_b).astype(BF16), sb.astype(BF16))
        sb = gc_b * sb + jnp.where(bd, _dot((ks[i] * kw_b).T.astype(BF16), vs[i]), 0.0)
    o = jnp.concatenate(outs, axis=0) if n > 1 else outs[0]
    lo_s = _lane_lo((seq, LANES))
    inv_d = 1.0 / HEAD_DIM
    mu = jnp.where(lo_s, jnp.sum(jnp.where(lo_s, o, 0.0), axis=-1, keepdims=True),
                   jnp.sum(jnp.where(lo_s, 0.0, o), axis=-1, keepdims=True)) * inv_d
    d = o - mu
    d2 = d * d
    var = jnp.where(lo_s, jnp.sum(jnp.where(lo_s, d2, 0.0), axis=-1, keepdims=True),
                    jnp.sum(jnp.where(lo_s, 0.0, d2), axis=-1, keepdims=True)) * inv_d
    y = d * lax.rsqrt(var + EPS) * (g * jax.nn.sigmoid(g))
    return y, sf, sb


N_STATES = 6


def _ctx_mixer_kernel(lgf_ref, lgb_ref, sink_ref, p_ref, o_in_ref, *refs, n_prev):
    prev_refs = refs[:N_STATES] if n_prev else ()
    o_ref, sf_ref, sb_ref, nk_ref, nv_ref, wk_ref, wv_ref = refs[len(prev_refs):]
    for prev, cur in zip(prev_refs, (sf_ref, sb_ref, nk_ref, nv_ref, wk_ref, wv_ref)):
        cur[:n_prev] = prev[...]
    l = n_prev

    def col(blk):
        return p_ref[:, blk * LANES:(blk + 1) * LANES]

    for hp in range(N_RET_PAIRS):
        y, sf, sb = _ret_pair(col(CB_RQ + hp), col(CB_RK + hp), col(CB_RV + hp), col(CB_RG + hp),
                              (lgf_ref[2 * hp], lgf_ref[2 * hp + 1]),
                              (lgb_ref[2 * hp], lgb_ref[2 * hp + 1]), None, None, SEQ)
        o_ref[:, (OB_RET + hp) * LANES:(OB_RET + hp + 1) * LANES] = y.astype(o_ref.dtype)
        sf_ref[l, 2 * hp] = sf[:HEAD_DIM, :HEAD_DIM]
        sf_ref[l, 2 * hp + 1] = sf[HEAD_DIM:, HEAD_DIM:]
        sb_ref[l, 2 * hp] = sb[:HEAD_DIM, :HEAD_DIM]
        sb_ref[l, 2 * hp + 1] = sb[HEAD_DIM:, HEAD_DIM:]

    for hp in range(N_NA_PAIRS):
        k = col(CB_NK + hp)
        v = col(CB_NV + hp)
        o = _attn_pair(col(CB_NQ + hp) * SCALE, [k.astype(BF16)], [v.astype(BF16)], [None], None)
        o_ref[:, (OB_NA + hp) * LANES:(OB_NA + hp + 1) * LANES] = o.astype(o_ref.dtype)
        nk_ref[l, 2 * hp] = k[:, :HEAD_DIM]
        nk_ref[l, 2 * hp + 1] = k[:, HEAD_DIM:]
        nv_ref[l, 2 * hp] = v[:, :HEAD_DIM]
        nv_ref[l, 2 * hp + 1] = v[:, HEAD_DIM:]

    k = col(CB_WK)
    v = col(CB_WV)
    for kv in range(KV_WIN):
        wk_ref[l, kv] = k[:, kv * HEAD_DIM:(kv + 1) * HEAD_DIM]
        wv_ref[l, kv] = v[:, kv * HEAD_DIM:(kv + 1) * HEAD_DIM]
    lo = _lane_lo(k.shape)
    k_sw = pltpu.roll(k, HEAD_DIM, 1)
    v_sw = pltpu.roll(v, HEAD_DIM, 1)
    for hp in range(N_WIN_PAIRS):
        kv_lo = (2 * hp) // G_WIN
        kv_hi = (2 * hp + 1) // G_WIN
        kk = jnp.where(lo, k if kv_lo == 0 else k_sw, k if kv_hi == 1 else k_sw).astype(BF16)
        vv = jnp.where(lo, v if kv_lo == 0 else v_sw, v if kv_hi == 1 else v_sw).astype(BF16)
        o = _attn_pair(col(CB_WQ + hp) * SCALE, [kk], [vv], [None],
                       (sink_ref[2 * hp], sink_ref[2 * hp + 1]))
        o_ref[:, (OB_WIN + hp) * LANES:(OB_WIN + hp + 1) * LANES] = o.astype(o_ref.dtype)


def _smem_spec():
    return pl.BlockSpec(memory_space=pltpu.SMEM)


def _ctx_mixer(proj, lgf, lgb, sink, n_ctx, prev_states):
    t = proj.shape[0]
    n_prev = prev_states[0].shape[1] if prev_states else 0
    o_init = jnp.zeros((t, W_MIX), BF16)
    dims = [(H_RET, HEAD_DIM, HEAD_DIM)] * 2 + [(H_NA, SEQ, HEAD_DIM)] * 2 + [(KV_WIN, SEQ, HEAD_DIM)] * 2
    st = lambda n, d: pl.BlockSpec((None, n) + d, lambda i: (i, 0, 0, 0, 0))
    return pl.pallas_call(
        functools.partial(_ctx_mixer_kernel, n_prev=n_prev),
        out_shape=(jax.ShapeDtypeStruct((t, W_MIX), BF16),
                   *[jax.ShapeDtypeStruct((n_ctx, n_prev + 1) + d, F32) for d in dims]),
        grid=(n_ctx,),
        in_specs=[_smem_spec(), _smem_spec(), _smem_spec(),
                  pl.BlockSpec((SEQ, W_IN), lambda i: (i, 0)), pl.BlockSpec(memory_space=pl.ANY),
                  *[st(n_prev, d) for d in dims[:len(prev_states)]]],
        out_specs=(pl.BlockSpec((SEQ, W_MIX), lambda i: (i, 0)), *[st(n_prev + 1, d) for d in dims]),
        input_output_aliases={4: 0},
        compiler_params=_cparams(("parallel",)), name="ctx_mixer",
    )(lgf, lgb, sink, proj, o_init, *prev_states)


def _ret_lat_kernel(lgf_ref, lgb_ref, q_ref, k_ref, v_ref, g_ref, s0f_ref, s0b_ref, o_in_ref, o_ref):
    hp = pl.program_id(1)
    y, _, _ = _ret_pair(q_ref[...], k_ref[...], v_ref[...], g_ref[...],
                        (lgf_ref[2 * hp], lgf_ref[2 * hp + 1]), (lgb_ref[2 * hp], lgb_ref[2 * hp + 1]),
                        _block_diag(s0f_ref[0], s0f_ref[1]), _block_diag(s0b_ref[0], s0b_ref[1]), DEC_SEQ)
    o_ref[...] = y.astype(o_ref.dtype)


def _ret_latent(proj, o, lgf, lgb, s0f, s0b, l, n_lat):
    rb0 = (proj.shape[0] - n_lat * DEC_SEQ) // DEC_SEQ
    cb = lambda c0: pl.BlockSpec((DEC_SEQ, LANES), lambda b, hp: (rb0 + b, c0 + hp))
    st = pl.BlockSpec((None, None, 2, HEAD_DIM, HEAD_DIM), lambda b, hp: (b, l, hp, 0, 0))
    return pl.pallas_call(
        _ret_lat_kernel, out_shape=jax.ShapeDtypeStruct(o.shape, o.dtype),
        grid=(n_lat, N_RET_PAIRS),
        in_specs=[_smem_spec(), _smem_spec(), cb(CB_RQ), cb(CB_RK), cb(CB_RV), cb(CB_RG), st, st,
                  pl.BlockSpec(memory_space=pl.ANY)],
        out_specs=pl.BlockSpec((DEC_SEQ, LANES), lambda b, hp: (rb0 + b, OB_RET + hp)),
        input_output_aliases={8: 0},
        compiler_params=_cparams(("parallel", "parallel")), name="ret_latent",
    )(lgf, lgb, proj, proj, proj, proj, s0f, s0b, o)


def _na_lat_kernel(q_ref, k_ref, v_ref, kc_ref, vc_ref, bias_ref, o_in_ref, o_ref):
    kc = jnp.concatenate([kc_ref[0], kc_ref[1]], axis=1).astype(BF16)
    vc = jnp.concatenate([vc_ref[0], vc_ref[1]], axis=1).astype(BF16)

    def body(r, carry):
        r0 = jnp.clip(r - NA_ROWS // 2, 0, LAT_ROWS - NA_ROWS)
        q0 = pl.multiple_of(r * GRID_W, GRID_W)
        k0 = pl.multiple_of(r0 * GRID_W, GRID_W)
        q = q_ref[pl.ds(q0, GRID_W), :] * SCALE
        kw = k_ref[pl.ds(k0, NA_WIN_KEYS), :].astype(BF16)
        vw = v_ref[pl.ds(k0, NA_WIN_KEYS), :].astype(BF16)
        dr0 = r0 - r + (NA_ROWS - 1)
        bias = [jnp.concatenate([bias_ref[h, dr0 + 2 * m] for m in range(NA_ROWS // 2)], axis=1)
                for h in (0, 1)]
        o = _attn_pair(q, [kw, kc], [vw, vc], [bias, None], None)
        o_ref[pl.ds(q0, GRID_W), :] = o.astype(o_ref.dtype)
        return carry

    lax.fori_loop(0, LAT_ROWS, body, 0, unroll=4)


N_ROW_OFFS = 2 * NA_ROWS - 1


def _na_bias_table(rpb):
    cols = np.arange(GRID_W)
    c0 = np.clip(cols - NA_COLS // 2, 0, GRID_W - NA_COLS)
    col_ok = (cols[None, :] >= c0[:, None]) & (cols[None, :] < c0[:, None] + NA_COLS)
    dc = np.clip(cols[None, :] - cols[:, None], -(NA_COLS - 1), NA_COLS - 1) + (NA_COLS - 1)
    by_col = jnp.where(col_ok[None, None], rpb.astype(F32)[:, :, dc], NEG)
    return jnp.concatenate([by_col[:, :-1], by_col[:, 1:]], axis=-1)


def _na_latent(proj, o, kc, vc, bias, l, n_lat):
    rb0 = (proj.shape[0] - n_lat * DEC_SEQ) // DEC_SEQ
    cb = lambda c0: pl.BlockSpec((DEC_SEQ, LANES), lambda b, hp: (rb0 + b, c0 + hp))
    cache = pl.BlockSpec((None, None, 2, PAST_LEN, HEAD_DIM), lambda b, hp: (b, l, hp, 0, 0))
    return pl.pallas_call(
        _na_lat_kernel, out_shape=jax.ShapeDtypeStruct(o.shape, o.dtype),
        grid=(n_lat, N_NA_PAIRS),
        in_specs=[cb(CB_NQ), cb(CB_NK), cb(CB_NV), cache, cache,
                  pl.BlockSpec((2, N_ROW_OFFS - 1, GRID_W, LANES), lambda b, hp: (hp, 0, 0, 0)),
                  pl.BlockSpec(memory_space=pl.ANY)],
        out_specs=pl.BlockSpec((DEC_SEQ, LANES), lambda b, hp: (rb0 + b, OB_NA + hp)),
        input_output_aliases={6: 0},
        compiler_params=_cparams(("parallel", "parallel")), name="na_latent",
    )(proj, proj, proj, kc, vc, bias, o)


def _rope_tables():
    t = jnp.arange(DEC_SEQ)
    d = np.arange(LANES) % HEAD_DIM
    quarter = HEAD_DIM // 4
    inv = ROPE_BASE ** (-jnp.arange(quarter, dtype=F32) / quarter)
    pos = jnp.where(jnp.asarray(d < HEAD_DIM // 2)[None, :], (t // GRID_W)[:, None], (t % GRID_W)[:, None])
    ang = pos.astype(F32) * inv[d % quarter][None, :]
    sign = jnp.asarray(np.where((d & quarter) == 0, -1.0, 1.0), F32)
    return jnp.cos(ang), jnp.sin(ang) * sign[None, :]


def _win_lat_kernel(sink_ref, q_ref, k_ref, v_ref, kc_ref, vc_ref, cos_ref, sin_ref, o_in_ref, o_ref,
                    qs_ref, kp_ref, vp_ref):
    hp = pl.program_id(1)
    quarter = HEAD_DIM // 4
    lane = lax.broadcasted_iota(jnp.int32, (DEC_SEQ, LANES), 1)
    first = (lane & quarter) == 0
    lo = lane < HEAD_DIM
    cos = cos_ref[...]
    sin = sin_ref[...]

    def rope(x):
        sw = jnp.where(first, pltpu.roll(x, LANES - quarter, 1), pltpu.roll(x, quarter, 1))
        return x * cos + sw * sin

    qs_ref[...] = rope(q_ref[...]) * SCALE
    k = rope(k_ref[...])
    v = v_ref[...]
    lo_orig = (2 * hp) // G_WIN == 0
    hi_orig = (2 * hp + 1) // G_WIN == 1
    orig = jnp.where(lo, lo_orig.astype(jnp.int32), hi_orig.astype(jnp.int32)) == 1
    zeros = jnp.zeros((WIN_BLOCK, LANES), BF16)
    kp_ref[:WIN_BLOCK] = zeros
    kp_ref[WIN_BLOCK + DEC_SEQ:] = zeros
    vp_ref[:WIN_BLOCK] = zeros
    vp_ref[WIN_BLOCK + DEC_SEQ:] = zeros
    kp_ref[WIN_BLOCK:WIN_BLOCK + DEC_SEQ] = jnp.where(orig, k, pltpu.roll(k, HEAD_DIM, 1)).astype(BF16)
    vp_ref[WIN_BLOCK:WIN_BLOCK + DEC_SEQ] = jnp.where(orig, v, pltpu.roll(v, HEAD_DIM, 1)).astype(BF16)
    kc = jnp.concatenate([jnp.where(lo_orig, kc_ref[0], kc_ref[1]),
                          jnp.where(hi_orig, kc_ref[1], kc_ref[0])], axis=1).astype(BF16)
    vc = jnp.concatenate([jnp.where(lo_orig, vc_ref[0], vc_ref[1]),
                          jnp.where(hi_orig, vc_ref[1], vc_ref[0])], axis=1).astype(BF16)
    sinks = (sink_ref[2 * hp], sink_ref[2 * hp + 1])
    n_band = 3 * WIN_BLOCK
    qi = lax.broadcasted_iota(jnp.int32, (WIN_BLOCK, n_band), 0)
    kj = lax.broadcasted_iota(jnp.int32, (WIN_BLOCK, n_band), 1)
    near = jnp.abs(qi + WIN_BLOCK - kj) <= WIN_HALF

    def body(n, carry):
        q0 = pl.multiple_of(n * WIN_BLOCK, WIN_BLOCK)
        kpos = (n - 1) * WIN_BLOCK + kj
        ok = jnp.logical_and(near, jnp.logical_and(kpos >= 0, kpos < DEC_SEQ))
        band = jnp.where(ok, 0.0, NEG)
        o = _attn_pair(qs_ref[pl.ds(q0, WIN_BLOCK), :],
                       [kp_ref[pl.ds(q0, n_band), :], kc], [vp_ref[pl.ds(q0, n_band), :], vc],
                       [(band, band), None], sinks)
        o_ref[pl.ds(q0, WIN_BLOCK), :] = o.astype(o_ref.dtype)
        return carry

    lax.fori_loop(0, DEC_SEQ // WIN_BLOCK, body, 0, unroll=8)


def _win_latent(proj, o, kc, vc, sink, cos, sin, l, n_lat):
    rb0 = (proj.shape[0] - n_lat * DEC_SEQ) // DEC_SEQ
    cache = pl.BlockSpec((None, None, KV_WIN, PAST_LEN, HEAD_DIM), lambda b, hp: (b, l, 0, 0, 0))
    tbl = pl.BlockSpec((DEC_SEQ, LANES), lambda b, hp: (0, 0))
    return pl.pallas_call(
        _win_lat_kernel, out_shape=jax.ShapeDtypeStruct(o.shape, o.dtype),
        grid=(n_lat, N_WIN_PAIRS),
        in_specs=[_smem_spec(),
                  pl.BlockSpec((DEC_SEQ, LANES), lambda b, hp: (rb0 + b, CB_WQ + hp)),
                  pl.BlockSpec((DEC_SEQ, LANES), lambda b, hp: (rb0 + b, CB_WK)),
                  pl.BlockSpec((DEC_SEQ, LANES), lambda b, hp: (rb0 + b, CB_WV)),
                  cache, cache, tbl, tbl, pl.BlockSpec(memory_space=pl.ANY)],
        out_specs=pl.BlockSpec((DEC_SEQ, LANES), lambda b, hp: (rb0 + b, OB_WIN + hp)),
        scratch_shapes=[pltpu.VMEM((DEC_SEQ, LANES), F32),
                        pltpu.VMEM((DEC_SEQ + 2 * WIN_BLOCK, LANES), BF16),
                        pltpu.VMEM((DEC_SEQ + 2 * WIN_BLOCK, LANES), BF16)],
        input_output_aliases={8: 0},
        compiler_params=_cparams(("parallel", "parallel")), name="win_latent",
    )(sink, proj, proj, proj, kc, vc, cos, sin, o)


def _mixers(proj, l, n_ctx, n_lat, prev_states, state_ret_fwd, state_ret_bwd, cache_na_k, cache_na_v,
            cache_win_k, cache_win_v, ret_decay_fwd, ret_decay_bwd, na_rpb, win_sink):
    lgf = jax.nn.log_sigmoid(ret_decay_fwd[l].astype(F32))
    lgb = jax.nn.log_sigmoid(ret_decay_bwd[l].astype(F32))
    sink = win_sink[l].astype(F32)
    o, *states = _ctx_mixer(proj, lgf, lgb, sink, n_ctx, prev_states)
    o = _ret_latent(proj, o, lgf, lgb, state_ret_fwd, state_ret_bwd, l, n_lat)
    o = _na_latent(proj, o, cache_na_k, cache_na_v, _na_bias_table(na_rpb[l]), l, n_lat)
    cos, sin = _rope_tables()
    o = _win_latent(proj, o, cache_win_k, cache_win_v, sink, cos, sin, l, n_lat)
    return o, tuple(states)


def kernel(x_prompt, x_sample, c, state_ret_fwd, state_ret_bwd, cache_na_k, cache_na_v, cache_win_k, cache_win_v, c_ctx, norm1_g, norm2_g, ada_w, ada_b, w_in, w_out, ret_decay_fwd, ret_decay_bwd, na_rpb, win_sink, ffn_w_in, ffn_w_out, moe_router, moe_w_in, moe_w_out, final_norm_g):
    x = jnp.concatenate([x_prompt.reshape(T_CTX, D_MODEL), x_sample.reshape(T_LAT, D_MODEL)], axis=0)
    cond = jnp.concatenate([c_ctx[None, :], c, jnp.zeros((N_GROUPS - 1 - DEC_BATCH, D_MODEL), F32)], axis=0)
    states = ()
    for l in range(DEPTH):
        mod = _adaln(cond, ada_w, ada_b, l)
        h = _norm_mod(x, norm1_g, mod, l, 0, 1)
        proj = _matmul(h, w_in, l)
        o, states = _mixers(proj, l, BATCH, DEC_BATCH, states, state_ret_fwd, state_ret_bwd, cache_na_k, cache_na_v,
                         cache_win_k, cache_win_v, ret_decay_fwd, ret_decay_bwd, na_rpb, win_sink)
        x = _matmul(o, w_out, l, resid=x, mod=mod, k_gate=2)
        i = l // 2
        if l % 2 == 0:
            h = _norm_mod(x, norm2_g, mod, l, 3, 4)
            x = _resid(x, _ffn_dense(h, ffn_w_in, ffn_w_out, i, D_FF, FFN_TM), mod, 5)
        else:
            router = jnp.pad(moe_router[i], ((0, 0), (0, LANES - N_EXPERTS)))
            h2, route, counts = _norm_mod(x, norm2_g, mod, l, 3, 4, router=router)
            x = _moe(x, h2, route, counts, moe_w_in[i], moe_w_out[i], mod, 5)
    y_prompt = _final_norm(x, final_norm_g, 0, T_CTX).reshape(BATCH, SEQ, D_MODEL)
    y_sample = _final_norm(x, final_norm_g, T_CTX, T_LAT).reshape(DEC_BATCH, DEC_SEQ, D_MODEL)
    return (y_prompt, y_sample, *states)
```

```python
import functools

import jax
import jax.numpy as jnp
from jax import lax
import numpy as np
from jax.experimental import pallas as pl
from jax.experimental.pallas import tpu as pltpu

D_MODEL = 2048
BATCH = 32
SEQ = 256
DEPTH = 2
DEC_BATCH = 4
DEC_SEQ = 1024
PAST_LEN = 256

GRID_W = 64
HEAD_DIM = 64
H_RET = 12
H_NA = 10
H_WIN = 10
KV_WIN = 2
G_WIN = H_WIN // KV_WIN
W_RET = H_RET * HEAD_DIM
W_NA = H_NA * HEAD_DIM
W_WIN = H_WIN * HEAD_DIM
W_MIX = W_RET + W_NA + W_WIN
W_IN = 4 * W_RET + 3 * W_NA + W_WIN + 2 * KV_WIN * HEAD_DIM
RET_CHUNK = 128
NA_ROWS = 8
NA_COLS = 16
WIN_HALF = 128
WIN_BLOCK = 128
ROPE_BASE = 10000.0
D_FF = 5632
N_EXPERTS = 8
TOP_K = 2
D_FF_EXPERT = 7168
EPS = 1e-6
NEG = -1e30
SCALE = HEAD_DIM ** -0.5

T_CTX = BATCH * SEQ
T_LAT = DEC_BATCH * DEC_SEQ
T_ALL = T_CTX + T_LAT
N_GROUPS = 8
LANES = 128

F32 = jnp.float32
BF16 = jnp.bfloat16

VMEM_LIMIT = 56 * 1024 * 1024


def _group_of_tile(i, tm):
    return jnp.maximum((i * tm - T_CTX) // DEC_SEQ + 1, 0)


def _cparams(sem):
    return pltpu.CompilerParams(dimension_semantics=sem, vmem_limit_bytes=VMEM_LIMIT)


def _adaln_kernel(c_ref, w_ref, b_ref, o_ref):
    c = c_ref[...]
    s = (c * jax.nn.sigmoid(c)).astype(BF16)
    o_ref[...] = jnp.dot(s, w_ref[...].astype(BF16), preferred_element_type=F32) + b_ref[...]


def _adaln(cond, ada_w, ada_b, l):
    tn = 1024
    n = 6 * D_MODEL
    out = pl.pallas_call(
        _adaln_kernel,
        out_shape=jax.ShapeDtypeStruct((N_GROUPS, n), F32),
        grid=(n // tn,),
        in_specs=[
            pl.BlockSpec((N_GROUPS, D_MODEL), lambda j: (0, 0)),
            pl.BlockSpec((None, D_MODEL, tn), lambda j: (l, 0, j)),
            pl.BlockSpec((None, 1, tn), lambda j: (l, 0, j)),
        ],
        out_specs=pl.BlockSpec((N_GROUPS, tn), lambda j: (0, j)),
        compiler_params=_cparams(("arbitrary",)),
        name="adaln",
    )(cond, ada_w, ada_b.reshape(DEPTH, 1, n))
    return out.reshape(N_GROUPS, 6, 1, D_MODEL)


N_SUB = D_MODEL // LANES


def _store_token_major(ref, val):
    rows = val.shape[0]
    for s in range(N_SUB):
        ref[pl.ds(s, rows, stride=N_SUB), :] = val[:, s * LANES:(s + 1) * LANES].astype(ref.dtype)


def _load_token_major(ref, rows, dtype=F32):
    return jnp.concatenate([ref[pl.ds(s, rows, stride=N_SUB), :].astype(dtype) for s in range(N_SUB)],
                           axis=1)


def _token_slab(ref, t):
    return ref.at[pl.ds(pl.multiple_of(t * N_SUB, N_SUB), N_SUB)]


def _norm_mod_body(x_ref, g_ref, sh_ref, sc_ref):
    x = x_ref[...]
    ms = jnp.mean(x * x, axis=-1, keepdims=True)
    y = x * lax.rsqrt(ms + EPS) * g_ref[...]
    return y * (1.0 + sc_ref[...]) + sh_ref[...]


def _norm_mod_kernel(x_ref, g_ref, sh_ref, sc_ref, h_ref):
    h_ref[...] = _norm_mod_body(x_ref, g_ref, sh_ref, sc_ref).astype(h_ref.dtype)


R_E0, R_E1, R_RANK0, R_RANK1, R_W0, R_W1 = range(6)


def _norm_mod_router_kernel(x_ref, g_ref, sh_ref, sc_ref, r_ref, h2_ref, route_ref, cnt_ref, run_ref):
    i = pl.program_id(0)

    @pl.when(i == 0)
    def _():
        run_ref[...] = jnp.zeros_like(run_ref)

    h = _norm_mod_body(x_ref, g_ref, sh_ref, sc_ref)
    _store_token_major(h2_ref, h)
    tm = h.shape[0]
    lg = jnp.dot(h, r_ref[...], preferred_element_type=F32, precision=lax.Precision.HIGHEST)
    lane = lax.broadcasted_iota(jnp.int32, (tm, LANES), 1)
    lane_f = lane.astype(F32)
    lg = jnp.where(lane < N_EXPERTS, lg, -jnp.inf)
    v0 = jnp.max(lg, axis=-1, keepdims=True)
    e0 = jnp.min(jnp.where(lg == v0, lane_f, float(LANES)), axis=-1, keepdims=True)
    lg1 = jnp.where(lane_f == e0, -jnp.inf, lg)
    v1 = jnp.max(lg1, axis=-1, keepdims=True)
    e1 = jnp.min(jnp.where(lg1 == v1, lane_f, float(LANES)), axis=-1, keepdims=True)
    ex = jnp.exp(v1 - v0)
    w0 = 1.0 / (1.0 + ex)
    w1 = ex / (1.0 + ex)
    oh0 = jnp.where(lane_f == e0, 1.0, 0.0)
    oh1 = jnp.where(lane_f == e1, 1.0, 0.0)
    oh = oh0 + oh1
    earlier = (lax.broadcasted_iota(jnp.int32, (tm, tm), 0)
               > lax.broadcasted_iota(jnp.int32, (tm, tm), 1))
    before = jnp.dot(jnp.where(earlier, 1.0, 0.0).astype(BF16), oh.astype(BF16),
                     preferred_element_type=F32) + run_ref[0:1, :]
    rank0 = jnp.sum(oh0 * before, axis=-1, keepdims=True)
    rank1 = jnp.sum(oh1 * before, axis=-1, keepdims=True)
    rec = jnp.zeros((tm, LANES), F32)
    for k, val in ((R_E0, e0), (R_E1, e1), (R_RANK0, rank0), (R_RANK1, rank1), (R_W0, w0), (R_W1, w1)):
        rec = jnp.where(lane == k, val, rec)
    route_ref[...] = rec
    run_ref[0:1, :] = run_ref[0:1, :] + jnp.sum(oh, axis=0, keepdims=True)
    cnt_ref[...] = run_ref[...]


def _norm_mod(x, g, mod, l, k_shift, k_scale, *, router=None):
    tm = 512
    g3 = g.reshape(DEPTH, 1, D_MODEL)
    in_specs = [
        pl.BlockSpec((tm, D_MODEL), lambda i: (i, 0)),
        pl.BlockSpec((None, 1, D_MODEL), lambda i: (l, 0, 0)),
        pl.BlockSpec((None, None, 1, D_MODEL), lambda i: (_group_of_tile(i, tm), k_shift, 0, 0)),
        pl.BlockSpec((None, None, 1, D_MODEL), lambda i: (_group_of_tile(i, tm), k_scale, 0, 0)),
    ]
    if router is not None:
        in_specs.append(pl.BlockSpec((D_MODEL, LANES), lambda i: (0, 0)))
        return pl.pallas_call(
            _norm_mod_router_kernel,
            out_shape=(jax.ShapeDtypeStruct((T_ALL * N_SUB, LANES), F32),
                       jax.ShapeDtypeStruct((T_ALL, LANES), F32),
                       jax.ShapeDtypeStruct((8, LANES), F32)),
            grid=(T_ALL // tm,),
            in_specs=in_specs,
            out_specs=(pl.BlockSpec((tm * N_SUB, LANES), lambda i: (i, 0)),
                       pl.BlockSpec((tm, LANES), lambda i: (i, 0)),
                       pl.BlockSpec((8, LANES), lambda i: (0, 0))),
            scratch_shapes=[pltpu.VMEM((8, LANES), F32)],
            compiler_params=_cparams(("arbitrary",)), name="norm_mod_router",
        )(x, g3, mod, mod, router)
    return pl.pallas_call(
        _norm_mod_kernel, out_shape=jax.ShapeDtypeStruct((T_ALL, D_MODEL), BF16), grid=(T_ALL // tm,),
        in_specs=in_specs, out_specs=pl.BlockSpec((tm, D_MODEL), lambda i: (i, 0)),
        compiler_params=_cparams(("parallel",)), name="norm_mod",
    )(x, g3, mod, mod)


def _mm_kernel(a_ref, w_ref, o_ref):
    o_ref[...] = jnp.dot(a_ref[...].astype(BF16), w_ref[...].astype(BF16),
                         preferred_element_type=F32)


def _mm_res_kernel(a_ref, w_ref, x_ref, gate_ref, o_ref):
    acc = jnp.dot(a_ref[...].astype(BF16), w_ref[...].astype(BF16),
                  preferred_element_type=F32)
    o_ref[...] = x_ref[...] + gate_ref[...] * acc


def _matmul(a, w, l, *, resid=None, mod=None, k_gate=None):
    tm, tn = (2048 if resid is None else DEC_SEQ), 512
    t, k = a.shape
    n = w.shape[-1]
    grid = (t // tm, pl.cdiv(n, tn))
    in_specs = [
        pl.BlockSpec((tm, k), lambda i, j: (i, 0)),
        pl.BlockSpec((None, k, tn), lambda i, j: (l, 0, j)),
    ]
    args = [a, w]
    kern = _mm_kernel
    if resid is not None:
        in_specs += [
            pl.BlockSpec((tm, tn), lambda i, j: (i, j)),
            pl.BlockSpec((None, None, 1, tn), lambda i, j: (_group_of_tile(i, tm), k_gate, 0, j)),
        ]
        args += [resid, mod]
        kern = _mm_res_kernel
    return pl.pallas_call(
        kern, out_shape=jax.ShapeDtypeStruct((t, n), F32), grid=grid,
        in_specs=in_specs, out_specs=pl.BlockSpec((tm, tn), lambda i, j: (i, j)),
        compiler_params=_cparams(("parallel", "arbitrary")), name="proj",
    )(*args)


FFN_TF = 256


def _ffn_step(h_ref, wa_ref, wg_ref, wo_ref, acc_ref, j, n_rows=None):
    @pl.when(j == 0)
    def _():
        acc_ref[...] = jnp.zeros_like(acc_ref)

    def rows(r0, n):
        h = h_ref[r0:r0 + n, :]
        a = jnp.dot(h, wa_ref[...].astype(BF16), preferred_element_type=F32)
        g = jnp.dot(h, wg_ref[...].astype(BF16), preferred_element_type=F32)
        act = (a * jax.nn.sigmoid(a) * g).astype(BF16)
        acc_ref[r0:r0 + n, :] += jnp.dot(act, wo_ref[...].astype(BF16), preferred_element_type=F32)

    tm = h_ref.shape[0]
    if n_rows is None:
        rows(0, tm)
    else:
        rows(0, tm // 2)

        @pl.when(n_rows > tm // 2)
        def _():
            rows(tm // 2, tm // 2)


def _ffn_dense_kernel(h_ref, wa_ref, wg_ref, wo_ref, x_ref, gate_ref, o_ref, *, n_f):
    j = pl.program_id(1)
    _ffn_step(h_ref, wa_ref, wg_ref, wo_ref, o_ref, j)

    @pl.when(j == n_f - 1)
    def _():
        o_ref[...] = x_ref[...] + gate_ref[...] * o_ref[...]


def _ffn_dense(h, x, mod, k_gate, w_in, w_out, e, d_ff, tm):
    tf = FFN_TF
    n_f = d_ff // tf
    return pl.pallas_call(
        functools.partial(_ffn_dense_kernel, n_f=n_f),
        out_shape=jax.ShapeDtypeStruct((h.shape[0], D_MODEL), F32),
        grid=(h.shape[0] // tm, n_f),
        in_specs=[
            pl.BlockSpec((tm, D_MODEL), lambda i, j: (i, 0)),
            pl.BlockSpec((None, D_MODEL, tf), lambda i, j: (e, 0, j)),
            pl.BlockSpec((None, D_MODEL, tf), lambda i, j: (e, 0, j + n_f)),
            pl.BlockSpec((None, tf, D_MODEL), lambda i, j: (e, j, 0)),
            pl.BlockSpec((tm, D_MODEL), lambda i, j: (i, 0), pipeline_mode=pl.Buffered(1)),
            pl.BlockSpec((None, None, 1, D_MODEL), lambda i, j: (_group_of_tile(i, tm), k_gate, 0, 0)),
        ],
        out_specs=pl.BlockSpec((tm, D_MODEL), lambda i, j: (i, 0)),
        compiler_params=_cparams(("parallel", "arbitrary")), name="ffn_dense",
    )(h, w_in, w_in, w_out, x, mod)


N_DMA_PRIORITIES = 2


def _ffn_expert_kernel(te_ref, nt_ref, nrows_ref, tok_ref, h2_ref, wa_ref, wg_ref, wo_ref, o_ref,
                       gbuf_ref, hb_ref, acc_ref, sem, *, n_f, tm):
    i = pl.program_id(0)
    j = pl.program_id(1)
    valid = i < nt_ref[0]

    def start_gather(tile):
        def body(p, c):
            for u in range(N_DMA_PRIORITIES):
                r = p * N_DMA_PRIORITIES + u
                pltpu.make_async_copy(_token_slab(h2_ref, tok_ref[tile * tm + r]),
                                      _token_slab(gbuf_ref, r), sem).start(priority=u)
            return c
        lax.fori_loop(0, tm // N_DMA_PRIORITIES, body, 0, unroll=4)

    @pl.when(jnp.logical_and(valid, j == 0))
    def _():
        @pl.when(i == 0)
        def _():
            start_gather(0)

        pltpu.make_async_copy(h2_ref.at[pl.ds(0, tm * N_SUB)], gbuf_ref, sem).wait()
        for s in range(N_SUB):
            hb_ref[:, s * LANES:(s + 1) * LANES] = gbuf_ref[pl.ds(s, tm, stride=N_SUB), :].astype(BF16)

        @pl.when(i + 1 < nt_ref[0])
        def _():
            start_gather(i + 1)

    @pl.when(valid)
    def _():
        _ffn_step(hb_ref, wa_ref, wg_ref, wo_ref, acc_ref, j, nrows_ref[i])

        @pl.when(j == n_f - 1)
        def _():
            _store_token_major(o_ref, acc_ref[...])

    @pl.when(jnp.logical_and(jnp.logical_not(valid), j == 0))
    def _():
        o_ref[...] = jnp.zeros_like(o_ref)


def _ffn_expert(h2, row_token, te, nt, tile_rows, w_in, w_out, d_ff, tm):
    tf = FFN_TF
    r = row_token.shape[0]
    n_f = d_ff // tf

    def jj(i, j, nt_ref):
        return jnp.where(i < nt_ref[0], j, n_f - 1)

    grid_spec = pltpu.PrefetchScalarGridSpec(
        num_scalar_prefetch=4,
        grid=(r // tm, n_f),
        in_specs=[
            pl.BlockSpec(memory_space=pl.ANY),
            pl.BlockSpec((None, D_MODEL, tf), lambda i, j, te_ref, nt_ref, *_: (te_ref[i], 0, jj(i, j, nt_ref))),
            pl.BlockSpec((None, D_MODEL, tf), lambda i, j, te_ref, nt_ref, *_: (te_ref[i], 0, jj(i, j, nt_ref) + n_f)),
            pl.BlockSpec((None, tf, D_MODEL), lambda i, j, te_ref, nt_ref, *_: (te_ref[i], jj(i, j, nt_ref), 0)),
        ],
        out_specs=pl.BlockSpec((tm * N_SUB, LANES), lambda i, j, *_: (i, 0)),
        scratch_shapes=[pltpu.VMEM((tm * N_SUB, LANES), F32), pltpu.VMEM((tm, D_MODEL), BF16),
                        pltpu.VMEM((tm, D_MODEL), F32), pltpu.SemaphoreType.DMA],
    )
    return pl.pallas_call(
        functools.partial(_ffn_expert_kernel, n_f=n_f, tm=tm),
        out_shape=jax.ShapeDtypeStruct((r * N_SUB, LANES), F32),
        grid_spec=grid_spec,
        compiler_params=_cparams(("arbitrary", "arbitrary")), name="ffn_expert",
    )(te, nt, tile_rows, row_token, h2, w_in, w_in, w_out)


COMBINE_TM = 256


def _combine_kernel(eid_ref, rank_ref, first_row_ref, x_ref, route_ref, gate_ref, ys2_ref, o_ref,
                    buf_ref, sem):
    i = pl.program_id(0)
    n = pl.num_programs(0)
    tm = COMBINE_TM

    def start_tile(tile, slot):
        def start(t, c):
            for k in range(TOP_K):
                a = (tile * tm + t) * TOP_K + k
                row = first_row_ref[eid_ref[a]] + rank_ref[a]
                pltpu.make_async_copy(_token_slab(ys2_ref, row), _token_slab(buf_ref.at[slot, k], t),
                                      sem.at[slot]).start(priority=k % N_DMA_PRIORITIES)
            return c
        lax.fori_loop(0, tm, start, 0, unroll=4)

    @pl.when(i == 0)
    def _():
        start_tile(0, 0)

    @pl.when(i + 1 < n)
    def _():
        start_tile(i + 1, (i + 1) % 2)

    slot = i % 2
    for k in range(TOP_K):
        pltpu.make_async_copy(ys2_ref.at[pl.ds(0, tm * N_SUB)], buf_ref.at[slot, k], sem.at[slot]).wait()

    rec = route_ref[...]
    w0 = rec[:, R_W0:R_W0 + 1]
    w1 = rec[:, R_W1:R_W1 + 1]
    y = (w0 * _load_token_major(buf_ref.at[slot, 0], tm) + w1 * _load_token_major(buf_ref.at[slot, 1], tm))
    o_ref[...] = x_ref[...] + gate_ref[...] * y


def _combine(x, ys2, route, eid, rank, first_row, mod, k_gate):
    tm = COMBINE_TM
    grid_spec = pltpu.PrefetchScalarGridSpec(
        num_scalar_prefetch=3,
        grid=(T_ALL // tm,),
        in_specs=[
            pl.BlockSpec((tm, D_MODEL), lambda i, *_: (i, 0)),
            pl.BlockSpec((tm, LANES), lambda i, *_: (i, 0)),
            pl.BlockSpec((None, None, 1, D_MODEL), lambda i, *_: (_group_of_tile(i, tm), k_gate, 0, 0)),
            pl.BlockSpec(memory_space=pl.ANY),
        ],
        out_specs=pl.BlockSpec((tm, D_MODEL), lambda i, *_: (i, 0)),
        scratch_shapes=[pltpu.VMEM((2, TOP_K, tm * N_SUB, LANES), F32), pltpu.SemaphoreType.DMA((2,))],
    )
    return pl.pallas_call(
        _combine_kernel,
        out_shape=jax.ShapeDtypeStruct((T_ALL, D_MODEL), F32),
        grid_spec=grid_spec,
        compiler_params=_cparams(("arbitrary",)), name="moe_combine",
    )(eid, rank, first_row, x, route, mod, ys2)


def _final_norm_kernel(x_ref, g_ref, o_ref):
    x = x_ref[...]
    ms = jnp.mean(x * x, axis=-1, keepdims=True)
    o_ref[...] = x * lax.rsqrt(ms + EPS) * g_ref[...]


def _final_norm(x, g, row0, rows):
    tm = 512
    return pl.pallas_call(
        _final_norm_kernel,
        out_shape=jax.ShapeDtypeStruct((rows, D_MODEL), F32), grid=(rows // tm,),
        in_specs=[pl.BlockSpec((tm, D_MODEL), lambda i: (row0 // tm + i, 0)),
                  pl.BlockSpec((1, D_MODEL), lambda i: (0, 0))],
        out_specs=pl.BlockSpec((tm, D_MODEL), lambda i: (i, 0)),
        compiler_params=_cparams(("parallel",)), name="final_norm",
    )(x, g.reshape(1, D_MODEL))


FFN_TM = 1024
MOE_TM = 1024
MOE_TILES = (TOP_K * T_ALL) // MOE_TM + N_EXPERTS


def _moe(x, h2, route, counts, w_in, w_out, mod, k_gate):
    tm = MOE_TM
    eid = route[:, R_E0:R_E1 + 1].astype(jnp.int32).reshape(-1)
    rank = route[:, R_RANK0:R_RANK1 + 1].astype(jnp.int32).reshape(-1)
    counts = counts[0, :N_EXPERTS].astype(jnp.int32)
    ptiles = (counts + tm - 1) // tm
    tile_end = jnp.cumsum(ptiles)
    first_row = (tile_end - ptiles) * tm
    n_tiles = tile_end[-1]
    tile_ids = jnp.arange(MOE_TILES, dtype=jnp.int32)
    te = jnp.sum((tile_ids[:, None] >= tile_end[None, :]).astype(jnp.int32), axis=1)
    te = jnp.minimum(te, N_EXPERTS - 1)
    te_last = jnp.sum(jnp.where(tile_ids == n_tiles - 1, te, 0))
    te = jnp.where(tile_ids < n_tiles, te, te_last)
    experts = jnp.arange(N_EXPERTS, dtype=jnp.int32)
    row_end = jnp.sum(jnp.where(te[:, None] == experts[None, :], (first_row + counts)[None, :], 0), axis=1)
    tile_rows = jnp.clip(row_end - tile_ids * tm, 1, tm)
    pos = jnp.sum(jnp.where(eid[:, None] == experts[None, :], first_row[None, :], 0), axis=1) + rank
    tok = jnp.arange(TOP_K * T_ALL, dtype=jnp.int32) // TOP_K
    row_token = jnp.zeros((MOE_TILES * tm,), jnp.int32).at[pos].set(tok)
    ys2 = _ffn_expert(h2, row_token, te, n_tiles.reshape(1), tile_rows, w_in, w_out, D_FF_EXPERT, tm)
    return _combine(x, ys2, route, eid, rank, first_row, mod, k_gate)


CB_RQ, CB_RK, CB_RV, CB_RG = 0, 6, 12, 18
CB_NQ, CB_NK, CB_NV = 24, 29, 34
CB_WQ, CB_WK, CB_WV = 39, 44, 45
OB_RET, OB_NA, OB_WIN = 0, 6, 11
N_RET_PAIRS = H_RET // 2
N_NA_PAIRS = H_NA // 2
N_WIN_PAIRS = H_WIN // 2
NA_WIN_KEYS = NA_ROWS * GRID_W
LAT_ROWS = DEC_SEQ // GRID_W


def _lane_lo(shape):
    return lax.broadcasted_iota(jnp.int32, shape, len(shape) - 1) < HEAD_DIM


def _dot_nt(a, b):
    return lax.dot_general(a, b, (((1,), (1,)), ((), ())), preferred_element_type=F32)


def _dot(a, b):
    return jnp.dot(a, b, preferred_element_type=F32)


def _attn_pair(q, ks, vs, biases, sinks):
    lo = _lane_lo(q.shape)
    outs = []
    for half in (0, 1):
        qm = jnp.where(lo if half == 0 else jnp.logical_not(lo), q, 0.0).astype(BF16)
        ss = []
        for kb, bb in zip(ks, biases):
            s = _dot_nt(qm, kb)
            if bb is not None:
                s = s + bb[half]
            ss.append(s)
        m = jnp.max(ss[0], axis=-1, keepdims=True)
        for s in ss[1:]:
            m = jnp.maximum(m, jnp.max(s, axis=-1, keepdims=True))
        if sinks is not None:
            m = jnp.maximum(m, sinks[half])
        ps = [jnp.exp(s - m) for s in ss]
        den = jnp.sum(ps[0], axis=-1, keepdims=True)
        for p in ps[1:]:
            den = den + jnp.sum(p, axis=-1, keepdims=True)
        if sinks is not None:
            den = den + jnp.exp(sinks[half] - m)
        o = _dot(ps[0].astype(BF16), vs[0])
        for p, vb in zip(ps[1:], vs[1:]):
            o = o + _dot(p.astype(BF16), vb)
        outs.append(o / den)
    return jnp.where(lo, outs[0], outs[1])


def _block_diag(s0, s1):
    z = jnp.zeros_like(s0)
    return jnp.concatenate([jnp.concatenate([s0, z], axis=1), jnp.concatenate([z, s1], axis=1)], axis=0)


def _ret_pair(q, k, v, g, lgf, lgb, s0f, s0b, seq):
    c = RET_CHUNK
    n = seq // c
    lo1 = _lane_lo((1, LANES))
    lo = _lane_lo((c, LANES))
    lgf_v = jnp.where(lo1, lgf[0], lgf[1])
    lgb_v = jnp.where(lo1, lgb[0], lgb[1])
    pos = lax.broadcasted_iota(jnp.int32, (c, LANES), 0).astype(F32)
    qw_f = jnp.exp(lgf_v * (pos + 1.0))
    kw_f = jnp.exp(lgf_v * (c - 1.0 - pos))
    qw_b = jnp.exp(lgb_v * (c - pos))
    kw_b = jnp.exp(lgb_v * pos)
    gc_f = jnp.exp(lgf_v * float(c))
    gc_b = jnp.exp(lgb_v * float(c))
    diff = (lax.broadcasted_iota(jnp.int32, (c, c), 0)
            - lax.broadcasted_iota(jnp.int32, (c, c), 1)).astype(F32)
    decay = [jnp.where(diff >= 0, jnp.exp(lgf[h] * jnp.maximum(diff, 0.0)), 0.0)
             + jnp.where(diff <= 0, jnp.exp(lgb[h] * jnp.maximum(-diff, 0.0)), 0.0) for h in (0, 1)]
    bd = ((lax.broadcasted_iota(jnp.int32, (LANES, LANES), 0) < HEAD_DIM)
          == (lax.broadcasted_iota(jnp.int32, (LANES, LANES), 1) < HEAD_DIM))
    zero = jnp.zeros((LANES, LANES), F32)
    sf = zero if s0f is None else s0f
    sb = zero if s0b is None else s0b
    qs = [q[i * c:(i + 1) * c] for i in range(n)]
    ks = [k[i * c:(i + 1) * c] * SCALE for i in range(n)]
    vs = [v[i * c:(i + 1) * c].astype(BF16) for i in range(n)]
    outs = []
    for i in range(n):
        kb = ks[i].astype(BF16)
        p0 = (_dot_nt(jnp.where(lo, qs[i], 0.0).astype(BF16), kb) * decay[0]).astype(BF16)
        p1 = (_dot_nt(jnp.where(lo, 0.0, qs[i]).astype(BF16), kb) * decay[1]).astype(BF16)
        o = jnp.where(lo, _dot(p0, vs[i]), _dot(p1, vs[i]))
        o = o + _dot((qs[i] * qw_f).astype(BF16), sf.astype(BF16))
        sf = gc_f * sf + jnp.where(bd, _dot((ks[i] * kw_f).T.astype(BF16), vs[i]), 0.0)
        outs.append(o)
    for i in reversed(range(n)):
        outs[i] = outs[i] + _dot((qs[i] * qw_b).astype(BF16), sb.astype(BF16))
        sb = gc_b * sb + jnp.where(bd, _dot((ks[i] * kw_b).T.astype(BF16), vs[i]), 0.0)
    o = jnp.concatenate(outs, axis=0) if n > 1 else outs[0]
    lo_s = _lane_lo((seq, LANES))
    inv_d = 1.0 / HEAD_DIM
    mu = jnp.where(lo_s, jnp.sum(jnp.where(lo_s, o, 0.0), axis=-1, keepdims=True),
                   jnp.sum(jnp.where(lo_s, 0.0, o), axis=-1, keepdims=True)) * inv_d
    d = o - mu
    d2 = d * d
    var = jnp.where(lo_s, jnp.sum(jnp.where(lo_s, d2, 0.0), axis=-1, keepdims=True),
                    jnp.sum(jnp.where(lo_s, 0.0, d2), axis=-1, keepdims=True)) * inv_d
    y = d * lax.rsqrt(var + EPS) * (g * jax.nn.sigmoid(g))
    return y, sf, sb


N_STATES = 6


def _ctx_mixer_kernel(lgf_ref, lgb_ref, sink_ref, p_ref, o_in_ref, *refs, n_prev):
    prev_refs = refs[:N_STATES] if n_prev else ()
    o_ref, sf_ref, sb_ref, nk_ref, nv_ref, wk_ref, wv_ref = refs[len(prev_refs):]
    for prev, cur in zip(prev_refs, (sf_ref, sb_ref, nk_ref, nv_ref, wk_ref, wv_ref)):
        cur[:n_prev] = prev[...]
    l = n_prev

    def col(blk):
        return p_ref[:, blk * LANES:(blk + 1) * LANES]

    for hp in range(N_RET_PAIRS):
        y, sf, sb = _ret_pair(col(CB_RQ + hp), col(CB_RK + hp), col(CB_RV + hp), col(CB_RG + hp),
                              (lgf_ref[2 * hp], lgf_ref[2 * hp + 1]),
                              (lgb_ref[2 * hp], lgb_ref[2 * hp + 1]), None, None, SEQ)
        o_ref[:, (OB_RET + hp) * LANES:(OB_RET + hp + 1) * LANES] = y.astype(o_ref.dtype)
        sf_ref[l, 2 * hp] = sf[:HEAD_DIM, :HEAD_DIM]
        sf_ref[l, 2 * hp + 1] = sf[HEAD_DIM:, HEAD_DIM:]
        sb_ref[l, 2 * hp] = sb[:HEAD_DIM, :HEAD_DIM]
        sb_ref[l, 2 * hp + 1] = sb[HEAD_DIM:, HEAD_DIM:]

    for hp in range(N_NA_PAIRS):
        k = col(CB_NK + hp)
        v = col(CB_NV + hp)
        o = _attn_pair(col(CB_NQ + hp) * SCALE, [k.astype(BF16)], [v.astype(BF16)], [None], None)
        o_ref[:, (OB_NA + hp) * LANES:(OB_NA + hp + 1) * LANES] = o.astype(o_ref.dtype)
        nk_ref[l, 2 * hp] = k[:, :HEAD_DIM]
        nk_ref[l, 2 * hp + 1] = k[:, HEAD_DIM:]
        nv_ref[l, 2 * hp] = v[:, :HEAD_DIM]
        nv_ref[l, 2 * hp + 1] = v[:, HEAD_DIM:]

    k = col(CB_WK)
    v = col(CB_WV)
    for kv in range(KV_WIN):
        wk_ref[l, kv] = k[:, kv * HEAD_DIM:(kv + 1) * HEAD_DIM]
        wv_ref[l, kv] = v[:, kv * HEAD_DIM:(kv + 1) * HEAD_DIM]
    lo = _lane_lo(k.shape)
    k_sw = pltpu.roll(k, HEAD_DIM, 1)
    v_sw = pltpu.roll(v, HEAD_DIM, 1)
    for hp in range(N_WIN_PAIRS):
        kv_lo = (2 * hp) // G_WIN
        kv_hi = (2 * hp + 1) // G_WIN
        kk = jnp.where(lo, k if kv_lo == 0 else k_sw, k if kv_hi == 1 else k_sw).astype(BF16)
        vv = jnp.where(lo, v if kv_lo == 0 else v_sw, v if kv_hi == 1 else v_sw).astype(BF16)
        o = _attn_pair(col(CB_WQ + hp) * SCALE, [kk], [vv], [None],
                       (sink_ref[2 * hp], sink_ref[2 * hp + 1]))
        o_ref[:, (OB_WIN + hp) * LANES:(OB_WIN + hp + 1) * LANES] = o.astype(o_ref.dtype)


def _smem_spec():
    return pl.BlockSpec(memory_space=pltpu.SMEM)


def _ctx_mixer(proj, lgf, lgb, sink, n_ctx, prev_states):
    t = proj.shape[0]
    n_prev = prev_states[0].shape[1] if prev_states else 0
    o_init = jnp.zeros((t, W_MIX), BF16)
    dims = [(H_RET, HEAD_DIM, HEAD_DIM)] * 2 + [(H_NA, SEQ, HEAD_DIM)] * 2 + [(KV_WIN, SEQ, HEAD_DIM)] * 2
    st = lambda n, d: pl.BlockSpec((None, n) + d, lambda i: (i, 0, 0, 0, 0))
    return pl.pallas_call(
        functools.partial(_ctx_mixer_kernel, n_prev=n_prev),
        out_shape=(jax.ShapeDtypeStruct((t, W_MIX), BF16),
                   *[jax.ShapeDtypeStruct((n_ctx, n_prev + 1) + d, F32) for d in dims]),
        grid=(n_ctx,),
        in_specs=[_smem_spec(), _smem_spec(), _smem_spec(),
                  pl.BlockSpec((SEQ, W_IN), lambda i: (i, 0)), pl.BlockSpec(memory_space=pl.ANY),
                  *[st(n_prev, d) for d in dims[:len(prev_states)]]],
        out_specs=(pl.BlockSpec((SEQ, W_MIX), lambda i: (i, 0)), *[st(n_prev + 1, d) for d in dims]),
        input_output_aliases={4: 0},
        compiler_params=_cparams(("parallel",)), name="ctx_mixer",
    )(lgf, lgb, sink, proj, o_init, *prev_states)


def _ret_lat_kernel(lgf_ref, lgb_ref, q_ref, k_ref, v_ref, g_ref, s0f_ref, s0b_ref, o_in_ref, o_ref):
    hp = pl.program_id(1)
    y, _, _ = _ret_pair(q_ref[...], k_ref[...], v_ref[...], g_ref[...],
                        (lgf_ref[2 * hp], lgf_ref[2 * hp + 1]), (lgb_ref[2 * hp], lgb_ref[2 * hp + 1]),
                        _block_diag(s0f_ref[0], s0f_ref[1]), _block_diag(s0b_ref[0], s0b_ref[1]), DEC_SEQ)
    o_ref[...] = y.astype(o_ref.dtype)


def _ret_latent(proj, o, lgf, lgb, s0f, s0b, l, n_lat):
    rb0 = (proj.shape[0] - n_lat * DEC_SEQ) // DEC_SEQ
    cb = lambda c0: pl.BlockSpec((DEC_SEQ, LANES), lambda b, hp: (rb0 + b, c0 + hp))
    st = pl.BlockSpec((None, None, 2, HEAD_DIM, HEAD_DIM), lambda b, hp: (b, l, hp, 0, 0))
    return pl.pallas_call(
        _ret_lat_kernel, out_shape=jax.ShapeDtypeStruct(o.shape, o.dtype),
        grid=(n_lat, N_RET_PAIRS),
        in_specs=[_smem_spec(), _smem_spec(), cb(CB_RQ), cb(CB_RK), cb(CB_RV), cb(CB_RG), st, st,
                  pl.BlockSpec(memory_space=pl.ANY)],
        out_specs=pl.BlockSpec((DEC_SEQ, LANES), lambda b, hp: (rb0 + b, OB_RET + hp)),
        input_output_aliases={8: 0},
        compiler_params=_cparams(("parallel", "parallel")), name="ret_latent",
    )(lgf, lgb, proj, proj, proj, proj, s0f, s0b, o)


NA_QROWS = 4
NA_KROWS = NA_ROWS + NA_QROWS
N_ROW_OFFS = 2 * NA_ROWS - 1


def _na_lat_kernel(q_ref, k_ref, v_ref, kc_ref, vc_ref, bias_ref, o_in_ref, o_ref):
    kc = jnp.concatenate([kc_ref[0], kc_ref[1]], axis=1).astype(BF16)
    vc = jnp.concatenate([vc_ref[0], vc_ref[1]], axis=1).astype(BF16)
    n_keys = NA_KROWS * GRID_W
    key_row = lax.broadcasted_iota(jnp.int32, (GRID_W, n_keys), 1) // GRID_W
    for blk in range(LAT_ROWS // NA_QROWS):
        w0 = min(max(blk * NA_QROWS - NA_ROWS // 2, 0), LAT_ROWS - NA_KROWS)
        q = q_ref[blk * NA_QROWS * GRID_W:(blk + 1) * NA_QROWS * GRID_W, :] * SCALE
        kw = k_ref[w0 * GRID_W:(w0 + NA_KROWS) * GRID_W, :].astype(BF16)
        vw = v_ref[w0 * GRID_W:(w0 + NA_KROWS) * GRID_W, :].astype(BF16)
        bias = [[], []]
        for qr in range(NA_QROWS):
            r = blk * NA_QROWS + qr
            r0 = min(max(r - NA_ROWS // 2, 0), LAT_ROWS - NA_ROWS)
            in_window = jnp.logical_and(key_row >= r0 - w0, key_row < r0 - w0 + NA_ROWS)
            for h in (0, 1):
                tiles = [bias_ref[h, min(max(w0 + 2 * m - r + NA_ROWS, 0), N_ROW_OFFS)]
                         for m in range(NA_KROWS // 2)]
                bias[h].append(jnp.where(in_window, jnp.concatenate(tiles, axis=1), NEG))
        bias = [jnp.concatenate(b, axis=0) for b in bias]
        o = _attn_pair(q, [kw, kc], [vw, vc], [bias, None], None)
        o_ref[blk * NA_QROWS * GRID_W:(blk + 1) * NA_QROWS * GRID_W, :] = o.astype(o_ref.dtype)


def _na_bias_table(rpb):
    cols = np.arange(GRID_W)
    c0 = np.clip(cols - NA_COLS // 2, 0, GRID_W - NA_COLS)
    col_ok = (cols[None, :] >= c0[:, None]) & (cols[None, :] < c0[:, None] + NA_COLS)
    dc = np.clip(cols[None, :] - cols[:, None], -(NA_COLS - 1), NA_COLS - 1) + (NA_COLS - 1)
    by_col = jnp.where(col_ok[None, None], rpb.astype(F32)[:, :, dc], NEG)
    masked = jnp.full((H_NA, 1, GRID_W, GRID_W), NEG, F32)
    by_col = jnp.concatenate([masked, by_col, masked], axis=1)
    return jnp.concatenate([by_col[:, :-1], by_col[:, 1:]], axis=-1)


def _na_latent(proj, o, kc, vc, bias, l, n_lat):
    rb0 = (proj.shape[0] - n_lat * DEC_SEQ) // DEC_SEQ
    cb = lambda c0: pl.BlockSpec((DEC_SEQ, LANES), lambda b, hp: (rb0 + b, c0 + hp))
    cache = pl.BlockSpec((None, None, 2, PAST_LEN, HEAD_DIM), lambda b, hp: (b, l, hp, 0, 0))
    return pl.pallas_call(
        _na_lat_kernel, out_shape=jax.ShapeDtypeStruct(o.shape, o.dtype),
        grid=(n_lat, N_NA_PAIRS),
        in_specs=[cb(CB_NQ), cb(CB_NK), cb(CB_NV), cache, cache,
                  pl.BlockSpec((2, N_ROW_OFFS + 1, GRID_W, LANES), lambda b, hp: (hp, 0, 0, 0)),
                  pl.BlockSpec(memory_space=pl.ANY)],
        out_specs=pl.BlockSpec((DEC_SEQ, LANES), lambda b, hp: (rb0 + b, OB_NA + hp)),
        input_output_aliases={6: 0},
        compiler_params=_cparams(("parallel", "parallel")), name="na_latent",
    )(proj, proj, proj, kc, vc, bias, o)


def _rope_tables():
    t = jnp.arange(DEC_SEQ)
    d = np.arange(LANES) % HEAD_DIM
    quarter = HEAD_DIM // 4
    inv = ROPE_BASE ** (-jnp.arange(quarter, dtype=F32) / quarter)
    pos = jnp.where(jnp.asarray(d < HEAD_DIM // 2)[None, :], (t // GRID_W)[:, None], (t % GRID_W)[:, None])
    ang = pos.astype(F32) * inv[d % quarter][None, :]
    sign = jnp.asarray(np.where((d & quarter) == 0, -1.0, 1.0), F32)
    return jnp.cos(ang), jnp.sin(ang) * sign[None, :]


def _win_lat_kernel(sink_ref, q_ref, k_ref, v_ref, kc_ref, vc_ref, cos_ref, sin_ref, o_in_ref, o_ref,
                    qs_ref, kp_ref, vp_ref):
    hp = pl.program_id(1)
    quarter = HEAD_DIM // 4
    lane = lax.broadcasted_iota(jnp.int32, (DEC_SEQ, LANES), 1)
    first = (lane & quarter) == 0
    lo = lane < HEAD_DIM
    cos = cos_ref[...]
    sin = sin_ref[...]

    def rope(x):
        sw = jnp.where(first, pltpu.roll(x, LANES - quarter, 1), pltpu.roll(x, quarter, 1))
        return x * cos + sw * sin

    qs_ref[...] = rope(q_ref[...]) * SCALE
    k = rope(k_ref[...])
    v = v_ref[...]
    lo_orig = (2 * hp) // G_WIN == 0
    hi_orig = (2 * hp + 1) // G_WIN == 1
    orig = jnp.where(lo, lo_orig.astype(jnp.int32), hi_orig.astype(jnp.int32)) == 1
    zeros = jnp.zeros((WIN_BLOCK, LANES), BF16)
    kp_ref[:WIN_BLOCK] = zeros
    kp_ref[WIN_BLOCK + DEC_SEQ:] = zeros
    vp_ref[:WIN_BLOCK] = zeros
    vp_ref[WIN_BLOCK + DEC_SEQ:] = zeros
    kp_ref[WIN_BLOCK:WIN_BLOCK + DEC_SEQ] = jnp.where(orig, k, pltpu.roll(k, HEAD_DIM, 1)).astype(BF16)
    vp_ref[WIN_BLOCK:WIN_BLOCK + DEC_SEQ] = jnp.where(orig, v, pltpu.roll(v, HEAD_DIM, 1)).astype(BF16)
    kc = jnp.concatenate([jnp.where(lo_orig, kc_ref[0], kc_ref[1]),
                          jnp.where(hi_orig, kc_ref[1], kc_ref[0])], axis=1).astype(BF16)
    vc = jnp.concatenate([jnp.where(lo_orig, vc_ref[0], vc_ref[1]),
                          jnp.where(hi_orig, vc_ref[1], vc_ref[0])], axis=1).astype(BF16)
    sinks = (sink_ref[2 * hp], sink_ref[2 * hp + 1])
    n_band = 3 * WIN_BLOCK
    qi = lax.broadcasted_iota(jnp.int32, (WIN_BLOCK, n_band), 0)
    kj = lax.broadcasted_iota(jnp.int32, (WIN_BLOCK, n_band), 1)
    near = jnp.abs(qi + WIN_BLOCK - kj) <= WIN_HALF

    def body(n, carry):
        q0 = pl.multiple_of(n * WIN_BLOCK, WIN_BLOCK)
        kpos = (n - 1) * WIN_BLOCK + kj
        ok = jnp.logical_and(near, jnp.logical_and(kpos >= 0, kpos < DEC_SEQ))
        band = jnp.where(ok, 0.0, NEG)
        o = _attn_pair(qs_ref[pl.ds(q0, WIN_BLOCK), :],
                       [kp_ref[pl.ds(q0, n_band), :], kc], [vp_ref[pl.ds(q0, n_band), :], vc],
                       [(band, band), None], sinks)
        o_ref[pl.ds(q0, WIN_BLOCK), :] = o.astype(o_ref.dtype)
        return carry

    lax.fori_loop(0, DEC_SEQ // WIN_BLOCK, body, 0, unroll=8)


def _win_latent(proj, o, kc, vc, sink, cos, sin, l, n_lat):
    rb0 = (proj.shape[0] - n_lat * DEC_SEQ) // DEC_SEQ
    cache = pl.BlockSpec((None, None, KV_WIN, PAST_LEN, HEAD_DIM), lambda b, hp: (b, l, 0, 0, 0))
    tbl = pl.BlockSpec((DEC_SEQ, LANES), lambda b, hp: (0, 0))
    return pl.pallas_call(
        _win_lat_kernel, out_shape=jax.ShapeDtypeStruct(o.shape, o.dtype),
        grid=(n_lat, N_WIN_PAIRS),
        in_specs=[_smem_spec(),
                  pl.BlockSpec((DEC_SEQ, LANES), lambda b, hp: (rb0 + b, CB_WQ + hp)),
                  pl.BlockSpec((DEC_SEQ, LANES), lambda b, hp: (rb0 + b, CB_WK)),
                  pl.BlockSpec((DEC_SEQ, LANES), lambda b, hp: (rb0 + b, CB_WV)),
                  cache, cache, tbl, tbl, pl.BlockSpec(memory_space=pl.ANY)],
        out_specs=pl.BlockSpec((DEC_SEQ, LANES), lambda b, hp: (rb0 + b, OB_WIN + hp)),
        scratch_shapes=[pltpu.VMEM((DEC_SEQ, LANES), F32),
                        pltpu.VMEM((DEC_SEQ + 2 * WIN_BLOCK, LANES), BF16),
                        pltpu.VMEM((DEC_SEQ + 2 * WIN_BLOCK, LANES), BF16)],
        input_output_aliases={8: 0},
        compiler_params=_cparams(("parallel", "parallel")), name="win_latent",
    )(sink, proj, proj, proj, kc, vc, cos, sin, o)


def _mixers(proj, l, n_ctx, n_lat, prev_states, state_ret_fwd, state_ret_bwd, cache_na_k, cache_na_v,
            cache_win_k, cache_win_v, ret_decay_fwd, ret_decay_bwd, na_rpb, win_sink):
    lgf = jax.nn.log_sigmoid(ret_decay_fwd[l].astype(F32))
    lgb = jax.nn.log_sigmoid(ret_decay_bwd[l].astype(F32))
    sink = win_sink[l].astype(F32)
    o, *states = _ctx_mixer(proj, lgf, lgb, sink, n_ctx, prev_states)
    o = _ret_latent(proj, o, lgf, lgb, state_ret_fwd, state_ret_bwd, l, n_lat)
    o = _na_latent(proj, o, cache_na_k, cache_na_v, _na_bias_table(na_rpb[l]), l, n_lat)
    cos, sin = _rope_tables()
    o = _win_latent(proj, o, cache_win_k, cache_win_v, sink, cos, sin, l, n_lat)
    return o, tuple(states)


def kernel(x_prompt, x_sample, c, state_ret_fwd, state_ret_bwd, cache_na_k, cache_na_v, cache_win_k, cache_win_v, c_ctx, norm1_g, norm2_g, ada_w, ada_b, w_in, w_out, ret_decay_fwd, ret_decay_bwd, na_rpb, win_sink, ffn_w_in, ffn_w_out, moe_router, moe_w_in, moe_w_out, final_norm_g):
    x = jnp.concatenate([x_prompt.reshape(T_CTX, D_MODEL), x_sample.reshape(T_LAT, D_MODEL)], axis=0)
    cond = jnp.concatenate([c_ctx[None, :], c, jnp.zeros((N_GROUPS - 1 - DEC_BATCH, D_MODEL), F32)], axis=0)
    states = ()
    for l in range(DEPTH):
        mod = _adaln(cond, ada_w, ada_b, l)
        h = _norm_mod(x, norm1_g, mod, l, 0, 1)
        proj = _matmul(h, w_in, l)
        o, states = _mixers(proj, l, BATCH, DEC_BATCH, states, state_ret_fwd, state_ret_bwd, cache_na_k, cache_na_v,
                         cache_win_k, cache_win_v, ret_decay_fwd, ret_decay_bwd, na_rpb, win_sink)
        x = _matmul(o, w_out, l, resid=x, mod=mod, k_gate=2)
        i = l // 2
        if l % 2 == 0:
            h = _norm_mod(x, norm2_g, mod, l, 3, 4)
            x = _ffn_dense(h, x, mod, 5, ffn_w_in, ffn_w_out, i, D_FF, FFN_TM)
        else:
            router = jnp.pad(moe_router[i], ((0, 0), (0, LANES - N_EXPERTS)))
            h2, route, counts = _norm_mod(x, norm2_g, mod, l, 3, 4, router=router)
            x = _moe(x, h2, route, counts, moe_w_in[i], moe_w_out[i], mod, 5)
    y_prompt = _final_norm(x, final_norm_g, 0, T_CTX).reshape(BATCH, SEQ, D_MODEL)
    y_sample = _final_norm(x, final_norm_g, T_CTX, T_LAT).reshape(DEC_BATCH, DEC_SEQ, D_MODEL)
    return (y_prompt, y_sample, *states)
```

```python
import functools

import jax
import jax.numpy as jnp
from jax import lax
import numpy as np
from jax.experimental import pallas as pl
from jax.experimental.pallas import tpu as pltpu

D_MODEL = 2048
BATCH = 32
SEQ = 256
DEPTH = 2
DEC_BATCH = 4
DEC_SEQ = 1024
PAST_LEN = 256

GRID_W = 64
HEAD_DIM = 64
H_RET = 12
H_NA = 10
H_WIN = 10
KV_WIN = 2
G_WIN = H_WIN // KV_WIN
W_RET = H_RET * HEAD_DIM
W_NA = H_NA * HEAD_DIM
W_WIN = H_WIN * HEAD_DIM
W_MIX = W_RET + W_NA + W_WIN
W_IN = 4 * W_RET + 3 * W_NA + W_WIN + 2 * KV_WIN * HEAD_DIM
RET_CHUNK = 128
NA_ROWS = 8
NA_COLS = 16
WIN_HALF = 128
WIN_BLOCK = 128
ROPE_BASE = 10000.0
D_FF = 5632
N_EXPERTS = 8
TOP_K = 2
D_FF_EXPERT = 7168
EPS = 1e-6
NEG = -1e30
SCALE = HEAD_DIM ** -0.5

T_CTX = BATCH * SEQ
T_LAT = DEC_BATCH * DEC_SEQ
T_ALL = T_CTX + T_LAT
N_GROUPS = 8
LANES = 128

F32 = jnp.float32
BF16 = jnp.bfloat16

VMEM_LIMIT = 56 * 1024 * 1024


def _group_of_tile(i, tm):
    return jnp.maximum((i * tm - T_CTX) // DEC_SEQ + 1, 0)


def _cparams(sem):
    return pltpu.CompilerParams(dimension_semantics=sem, vmem_limit_bytes=VMEM_LIMIT)


def _adaln_kernel(c_ref, w_ref, b_ref, o_ref):
    c = c_ref[...]
    s = (c * jax.nn.sigmoid(c)).astype(BF16)
    o_ref[...] = jnp.dot(s, w_ref[...].astype(BF16), preferred_element_type=F32) + b_ref[...]


def _adaln(cond, ada_w, ada_b, l):
    tn = 1024
    n = 6 * D_MODEL
    out = pl.pallas_call(
        _adaln_kernel,
        out_shape=jax.ShapeDtypeStruct((N_GROUPS, n), F32),
        grid=(n // tn,),
        in_specs=[
            pl.BlockSpec((N_GROUPS, D_MODEL), lambda j: (0, 0)),
            pl.BlockSpec((None, D_MODEL, tn), lambda j: (l, 0, j)),
            pl.BlockSpec((None, 1, tn), lambda j: (l, 0, j)),
        ],
        out_specs=pl.BlockSpec((N_GROUPS, tn), lambda j: (0, j)),
        compiler_params=_cparams(("arbitrary",)),
        name="adaln",
    )(cond, ada_w, ada_b.reshape(DEPTH, 1, n))
    return out.reshape(N_GROUPS, 6, 1, D_MODEL)


N_SUB = D_MODEL // LANES


def _store_token_major(ref, val):
    rows = val.shape[0]
    for s in range(N_SUB):
        ref[pl.ds(s, rows, stride=N_SUB), :] = val[:, s * LANES:(s + 1) * LANES].astype(ref.dtype)


def _load_token_major(ref, rows, dtype=F32):
    return jnp.concatenate([ref[pl.ds(s, rows, stride=N_SUB), :].astype(dtype) for s in range(N_SUB)],
                           axis=1)


def _token_slab(ref, t):
    return ref.at[pl.ds(pl.multiple_of(t * N_SUB, N_SUB), N_SUB)]


def _norm_mod_body(x_ref, g_ref, sh_ref, sc_ref):
    x = x_ref[...]
    ms = jnp.mean(x * x, axis=-1, keepdims=True)
    y = x * lax.rsqrt(ms + EPS) * g_ref[...]
    return y * (1.0 + sc_ref[...]) + sh_ref[...]


def _norm_mod_kernel(x_ref, g_ref, sh_ref, sc_ref, h_ref):
    h_ref[...] = _norm_mod_body(x_ref, g_ref, sh_ref, sc_ref).astype(h_ref.dtype)


R_E0, R_E1, R_RANK0, R_RANK1, R_W0, R_W1 = range(6)


def _norm_mod_router_kernel(x_ref, g_ref, sh_ref, sc_ref, r_ref, h2_ref, route_ref, cnt_ref, run_ref):
    i = pl.program_id(0)

    @pl.when(i == 0)
    def _():
        run_ref[...] = jnp.zeros_like(run_ref)

    h = _norm_mod_body(x_ref, g_ref, sh_ref, sc_ref)
    _store_token_major(h2_ref, h)
    tm = h.shape[0]
    r = r_ref[...]
    h_hi = h.astype(BF16)
    r_hi = r.astype(BF16)
    h_lo = (h - h_hi.astype(F32)).astype(BF16)
    r_lo = (r - r_hi.astype(F32)).astype(BF16)
    lg = _dot(h_hi, r_hi) + (_dot(h_hi, r_lo) + _dot(h_lo, r_hi))
    lane = lax.broadcasted_iota(jnp.int32, (tm, LANES), 1)
    lane_f = lane.astype(F32)
    lg = jnp.where(lane < N_EXPERTS, lg, -jnp.inf)
    v0 = jnp.max(lg, axis=-1, keepdims=True)
    e0 = jnp.min(jnp.where(lg == v0, lane_f, float(LANES)), axis=-1, keepdims=True)
    lg1 = jnp.where(lane_f == e0, -jnp.inf, lg)
    v1 = jnp.max(lg1, axis=-1, keepdims=True)
    e1 = jnp.min(jnp.where(lg1 == v1, lane_f, float(LANES)), axis=-1, keepdims=True)
    ex = jnp.exp(v1 - v0)
    w0 = 1.0 / (1.0 + ex)
    w1 = ex / (1.0 + ex)
    oh0 = jnp.where(lane_f == e0, 1.0, 0.0)
    oh1 = jnp.where(lane_f == e1, 1.0, 0.0)
    oh = oh0 + oh1
    earlier = (lax.broadcasted_iota(jnp.int32, (tm, tm), 0)
               > lax.broadcasted_iota(jnp.int32, (tm, tm), 1))
    before = jnp.dot(jnp.where(earlier, 1.0, 0.0).astype(BF16), oh.astype(BF16),
                     preferred_element_type=F32) + run_ref[0:1, :]
    rank0 = jnp.sum(oh0 * before, axis=-1, keepdims=True)
    rank1 = jnp.sum(oh1 * before, axis=-1, keepdims=True)
    rec = jnp.zeros((tm, LANES), F32)
    for k, val in ((R_E0, e0), (R_E1, e1), (R_RANK0, rank0), (R_RANK1, rank1), (R_W0, w0), (R_W1, w1)):
        rec = jnp.where(lane == k, val, rec)
    route_ref[...] = rec
    run_ref[0:1, :] = run_ref[0:1, :] + jnp.sum(oh, axis=0, keepdims=True)
    cnt_ref[...] = run_ref[...]


def _norm_mod(x, g, mod, l, k_shift, k_scale, *, router=None):
    tm = 512
    g3 = g.reshape(DEPTH, 1, D_MODEL)
    in_specs = [
        pl.BlockSpec((tm, D_MODEL), lambda i: (i, 0)),
        pl.BlockSpec((None, 1, D_MODEL), lambda i: (l, 0, 0)),
        pl.BlockSpec((None, None, 1, D_MODEL), lambda i: (_group_of_tile(i, tm), k_shift, 0, 0)),
        pl.BlockSpec((None, None, 1, D_MODEL), lambda i: (_group_of_tile(i, tm), k_scale, 0, 0)),
    ]
    if router is not None:
        in_specs.append(pl.BlockSpec((D_MODEL, LANES), lambda i: (0, 0)))
        return pl.pallas_call(
            _norm_mod_router_kernel,
            out_shape=(jax.ShapeDtypeStruct((T_ALL * N_SUB, LANES), F32),
                       jax.ShapeDtypeStruct((T_ALL, LANES), F32),
                       jax.ShapeDtypeStruct((8, LANES), F32)),
            grid=(T_ALL // tm,),
            in_specs=in_specs,
            out_specs=(pl.BlockSpec((tm * N_SUB, LANES), lambda i: (i, 0)),
                       pl.BlockSpec((tm, LANES), lambda i: (i, 0)),
                       pl.BlockSpec((8, LANES), lambda i: (0, 0))),
            scratch_shapes=[pltpu.VMEM((8, LANES), F32)],
            compiler_params=_cparams(("arbitrary",)), name="norm_mod_router",
        )(x, g3, mod, mod, router)
    return pl.pallas_call(
        _norm_mod_kernel, out_shape=jax.ShapeDtypeStruct((T_ALL, D_MODEL), BF16), grid=(T_ALL // tm,),
        in_specs=in_specs, out_specs=pl.BlockSpec((tm, D_MODEL), lambda i: (i, 0)),
        compiler_params=_cparams(("parallel",)), name="norm_mod",
    )(x, g3, mod, mod)


def _mm_kernel(a_ref, w_ref, o_ref):
    o_ref[...] = jnp.dot(a_ref[...].astype(BF16), w_ref[...].astype(BF16),
                         preferred_element_type=F32)


def _mm_res_kernel(a_ref, w_ref, x_ref, gate_ref, o_ref):
    acc = jnp.dot(a_ref[...].astype(BF16), w_ref[...].astype(BF16),
                  preferred_element_type=F32)
    o_ref[...] = x_ref[...] + gate_ref[...] * acc


def _matmul(a, w, l, *, resid=None, mod=None, k_gate=None):
    tm, tn = (2048 if resid is None else DEC_SEQ), 512
    t, k = a.shape
    n = w.shape[-1]
    grid = (t // tm, pl.cdiv(n, tn))
    in_specs = [
        pl.BlockSpec((tm, k), lambda i, j: (i, 0)),
        pl.BlockSpec((None, k, tn), lambda i, j: (l, 0, j)),
    ]
    args = [a, w]
    kern = _mm_kernel
    if resid is not None:
        in_specs += [
            pl.BlockSpec((tm, tn), lambda i, j: (i, j)),
            pl.BlockSpec((None, None, 1, tn), lambda i, j: (_group_of_tile(i, tm), k_gate, 0, j)),
        ]
        args += [resid, mod]
        kern = _mm_res_kernel
    return pl.pallas_call(
        kern, out_shape=jax.ShapeDtypeStruct((t, n), F32), grid=grid,
        in_specs=in_specs, out_specs=pl.BlockSpec((tm, tn), lambda i, j: (i, j)),
        compiler_params=_cparams(("parallel", "arbitrary")), name="proj",
    )(*args)


FFN_TF = 256


def _ffn_step(h_ref, wa_ref, wg_ref, wo_ref, acc_ref, j, n_rows=None):
    @pl.when(j == 0)
    def _():
        acc_ref[...] = jnp.zeros_like(acc_ref)

    def rows(r0, n):
        h = h_ref[r0:r0 + n, :]
        a = jnp.dot(h, wa_ref[...].astype(BF16), preferred_element_type=F32)
        g = jnp.dot(h, wg_ref[...].astype(BF16), preferred_element_type=F32)
        act = (a * jax.nn.sigmoid(a) * g).astype(BF16)
        acc_ref[r0:r0 + n, :] += jnp.dot(act, wo_ref[...].astype(BF16), preferred_element_type=F32)

    tm = h_ref.shape[0]
    if n_rows is None:
        rows(0, tm)
    else:
        @pl.when(n_rows > tm // 2)
        def _():
            rows(0, tm)

        @pl.when(n_rows <= tm // 2)
        def _():
            rows(0, tm // 2)


def _ffn_dense_kernel(h_ref, wa_ref, wg_ref, wo_ref, x_ref, gate_ref, o_ref, *, n_f):
    j = pl.program_id(1)
    _ffn_step(h_ref, wa_ref, wg_ref, wo_ref, o_ref, j)

    @pl.when(j == n_f - 1)
    def _():
        o_ref[...] = x_ref[...] + gate_ref[...] * o_ref[...]


def _ffn_dense(h, x, mod, k_gate, w_in, w_out, e, d_ff, tm):
    tf = FFN_TF
    n_f = d_ff // tf
    return pl.pallas_call(
        functools.partial(_ffn_dense_kernel, n_f=n_f),
        out_shape=jax.ShapeDtypeStruct((h.shape[0], D_MODEL), F32),
        grid=(h.shape[0] // tm, n_f),
        in_specs=[
            pl.BlockSpec((tm, D_MODEL), lambda i, j: (i, 0)),
            pl.BlockSpec((None, D_MODEL, tf), lambda i, j: (e, 0, j)),
            pl.BlockSpec((None, D_MODEL, tf), lambda i, j: (e, 0, j + n_f)),
            pl.BlockSpec((None, tf, D_MODEL), lambda i, j: (e, j, 0)),
            pl.BlockSpec((tm, D_MODEL), lambda i, j: (i, 0), pipeline_mode=pl.Buffered(1)),
            pl.BlockSpec((None, None, 1, D_MODEL), lambda i, j: (_group_of_tile(i, tm), k_gate, 0, 0)),
        ],
        out_specs=pl.BlockSpec((tm, D_MODEL), lambda i, j: (i, 0)),
        compiler_params=_cparams(("parallel", "arbitrary")), name="ffn_dense",
    )(h, w_in, w_in, w_out, x, mod)


N_DMA_PRIORITIES = 2


def _ffn_expert_kernel(te_ref, nt_ref, nrows_ref, tok_ref, h2_ref, wa_ref, wg_ref, wo_ref, o_ref,
                       gbuf_ref, hb_ref, acc_ref, sem, *, n_f, tm):
    i = pl.program_id(0)
    j = pl.program_id(1)
    valid = i < nt_ref[0]

    def start_gather(tile):
        def body(p, c):
            for u in range(N_DMA_PRIORITIES):
                r = p * N_DMA_PRIORITIES + u
                pltpu.make_async_copy(_token_slab(h2_ref, tok_ref[tile * tm + r]),
                                      _token_slab(gbuf_ref, r), sem).start(priority=u)
            return c
        lax.fori_loop(0, tm // N_DMA_PRIORITIES, body, 0, unroll=4)

    @pl.when(jnp.logical_and(valid, j == 0))
    def _():
        @pl.when(i == 0)
        def _():
            start_gather(0)

        pltpu.make_async_copy(h2_ref.at[pl.ds(0, tm * N_SUB)], gbuf_ref, sem).wait()
        for s in range(N_SUB):
            hb_ref[:, s * LANES:(s + 1) * LANES] = gbuf_ref[pl.ds(s, tm, stride=N_SUB), :].astype(BF16)

        @pl.when(i + 1 < nt_ref[0])
        def _():
            start_gather(i + 1)

    @pl.when(valid)
    def _():
        _ffn_step(hb_ref, wa_ref, wg_ref, wo_ref, acc_ref, j, nrows_ref[i])

        @pl.when(j == n_f - 1)
        def _():
            _store_token_major(o_ref, acc_ref[...])

    @pl.when(jnp.logical_and(jnp.logical_not(valid), j == 0))
    def _():
        o_ref[...] = jnp.zeros_like(o_ref)


def _ffn_expert(h2, row_token, te, nt, tile_rows, w_in, w_out, d_ff, tm):
    tf = FFN_TF
    r = row_token.shape[0]
    n_f = d_ff // tf

    def jj(i, j, nt_ref):
        return jnp.where(i < nt_ref[0], j, n_f - 1)

    grid_spec = pltpu.PrefetchScalarGridSpec(
        num_scalar_prefetch=4,
        grid=(r // tm, n_f),
        in_specs=[
            pl.BlockSpec(memory_space=pl.ANY),
            pl.BlockSpec((None, D_MODEL, tf), lambda i, j, te_ref, nt_ref, *_: (te_ref[i], 0, jj(i, j, nt_ref))),
            pl.BlockSpec((None, D_MODEL, tf), lambda i, j, te_ref, nt_ref, *_: (te_ref[i], 0, jj(i, j, nt_ref) + n_f)),
            pl.BlockSpec((None, tf, D_MODEL), lambda i, j, te_ref, nt_ref, *_: (te_ref[i], jj(i, j, nt_ref), 0)),
        ],
        out_specs=pl.BlockSpec((tm * N_SUB, LANES), lambda i, j, *_: (i, 0)),
        scratch_shapes=[pltpu.VMEM((tm * N_SUB, LANES), F32), pltpu.VMEM((tm, D_MODEL), BF16),
                        pltpu.VMEM((tm, D_MODEL), F32), pltpu.SemaphoreType.DMA],
    )
    return pl.pallas_call(
        functools.partial(_ffn_expert_kernel, n_f=n_f, tm=tm),
        out_shape=jax.ShapeDtypeStruct((r * N_SUB, LANES), F32),
        grid_spec=grid_spec,
        compiler_params=_cparams(("arbitrary", "arbitrary")), name="ffn_expert",
    )(te, nt, tile_rows, row_token, h2, w_in, w_in, w_out)


COMBINE_TM = 256


def _combine_kernel(eid_ref, rank_ref, first_row_ref, x_ref, route_ref, gate_ref, ys2_ref, o_ref,
                    buf_ref, sem):
    i = pl.program_id(0)
    n = pl.num_programs(0)
    tm = COMBINE_TM

    def start_tile(tile, slot):
        def start(t, c):
            for k in range(TOP_K):
                a = (tile * tm + t) * TOP_K + k
                row = first_row_ref[eid_ref[a]] + rank_ref[a]
                pltpu.make_async_copy(_token_slab(ys2_ref, row), _token_slab(buf_ref.at[slot, k], t),
                                      sem.at[slot]).start(priority=k % N_DMA_PRIORITIES)
            return c
        lax.fori_loop(0, tm, start, 0, unroll=4)

    @pl.when(i == 0)
    def _():
        start_tile(0, 0)

    @pl.when(i + 1 < n)
    def _():
        start_tile(i + 1, (i + 1) % 2)

    slot = i % 2
    for k in range(TOP_K):
        pltpu.make_async_copy(ys2_ref.at[pl.ds(0, tm * N_SUB)], buf_ref.at[slot, k], sem.at[slot]).wait()

    rec = route_ref[...]
    w0 = rec[:, R_W0:R_W0 + 1]
    w1 = rec[:, R_W1:R_W1 + 1]
    y = (w0 * _load_token_major(buf_ref.at[slot, 0], tm) + w1 * _load_token_major(buf_ref.at[slot, 1], tm))
    o_ref[...] = x_ref[...] + gate_ref[...] * y


def _combine(x, ys2, route, eid, rank, first_row, mod, k_gate):
    tm = COMBINE_TM
    grid_spec = pltpu.PrefetchScalarGridSpec(
        num_scalar_prefetch=3,
        grid=(T_ALL // tm,),
        in_specs=[
            pl.BlockSpec((tm, D_MODEL), lambda i, *_: (i, 0)),
            pl.BlockSpec((tm, LANES), lambda i, *_: (i, 0)),
            pl.BlockSpec((None, None, 1, D_MODEL), lambda i, *_: (_group_of_tile(i, tm), k_gate, 0, 0)),
            pl.BlockSpec(memory_space=pl.ANY),
        ],
        out_specs=pl.BlockSpec((tm, D_MODEL), lambda i, *_: (i, 0)),
        scratch_shapes=[pltpu.VMEM((2, TOP_K, tm * N_SUB, LANES), F32), pltpu.SemaphoreType.DMA((2,))],
    )
    return pl.pallas_call(
        _combine_kernel,
        out_shape=jax.ShapeDtypeStruct((T_ALL, D_MODEL), F32),
        grid_spec=grid_spec,
        compiler_params=_cparams(("arbitrary",)), name="moe_combine",
    )(eid, rank, first_row, x, route, mod, ys2)


def _final_norm_kernel(x_ref, g_ref, o_ref):
    x = x_ref[...]
    ms = jnp.mean(x * x, axis=-1, keepdims=True)
    o_ref[...] = x * lax.rsqrt(ms + EPS) * g_ref[...]


def _final_norm(x, g, row0, rows):
    tm = 512
    return pl.pallas_call(
        _final_norm_kernel,
        out_shape=jax.ShapeDtypeStruct((rows, D_MODEL), F32), grid=(rows // tm,),
        in_specs=[pl.BlockSpec((tm, D_MODEL), lambda i: (row0 // tm + i, 0)),
                  pl.BlockSpec((1, D_MODEL), lambda i: (0, 0))],
        out_specs=pl.BlockSpec((tm, D_MODEL), lambda i: (i, 0)),
        compiler_params=_cparams(("parallel",)), name="final_norm",
    )(x, g.reshape(1, D_MODEL))


FFN_TM = 1024
MOE_TM = 1024
MOE_TILES = (TOP_K * T_ALL) // MOE_TM + N_EXPERTS


def _moe(x, h2, route, counts, w_in, w_out, mod, k_gate):
    tm = MOE_TM
    eid = route[:, R_E0:R_E1 + 1].astype(jnp.int32).reshape(-1)
    rank = route[:, R_RANK0:R_RANK1 + 1].astype(jnp.int32).reshape(-1)
    counts = counts[0, :N_EXPERTS].astype(jnp.int32)
    ptiles = (counts + tm - 1) // tm
    tile_end = jnp.cumsum(ptiles)
    first_row = (tile_end - ptiles) * tm
    n_tiles = tile_end[-1]
    tile_ids = jnp.arange(MOE_TILES, dtype=jnp.int32)
    te = jnp.sum((tile_ids[:, None] >= tile_end[None, :]).astype(jnp.int32), axis=1)
    te = jnp.minimum(te, N_EXPERTS - 1)
    te_last = jnp.sum(jnp.where(tile_ids == n_tiles - 1, te, 0))
    te = jnp.where(tile_ids < n_tiles, te, te_last)
    experts = jnp.arange(N_EXPERTS, dtype=jnp.int32)
    row_end = jnp.sum(jnp.where(te[:, None] == experts[None, :], (first_row + counts)[None, :], 0), axis=1)
    tile_rows = jnp.clip(row_end - tile_ids * tm, 1, tm)
    pos = jnp.sum(jnp.where(eid[:, None] == experts[None, :], first_row[None, :], 0), axis=1) + rank
    tok = jnp.arange(TOP_K * T_ALL, dtype=jnp.int32) // TOP_K
    row_token = jnp.zeros((MOE_TILES * tm,), jnp.int32).at[pos].set(tok)
    ys2 = _ffn_expert(h2, row_token, te, n_tiles.reshape(1), tile_rows, w_in, w_out, D_FF_EXPERT, tm)
    return _combine(x, ys2, route, eid, rank, first_row, mod, k_gate)


CB_RQ, CB_RK, CB_RV, CB_RG = 0, 6, 12, 18
CB_NQ, CB_NK, CB_NV = 24, 29, 34
CB_WQ, CB_WK, CB_WV = 39, 44, 45
OB_RET, OB_NA, OB_WIN = 0, 6, 11
N_RET_PAIRS = H_RET // 2
N_NA_PAIRS = H_NA // 2
N_WIN_PAIRS = H_WIN // 2
NA_WIN_KEYS = NA_ROWS * GRID_W
LAT_ROWS = DEC_SEQ // GRID_W
WIN_Q = 2 * WIN_BLOCK


def _lane_lo(shape):
    return lax.broadcasted_iota(jnp.int32, shape, len(shape) - 1) < HEAD_DIM


def _dot_nt(a, b):
    return lax.dot_general(a, b, (((1,), (1,)), ((), ())), preferred_element_type=F32)


def _dot(a, b):
    return jnp.dot(a, b, preferred_element_type=F32)


def _attn_pair(q, ks, vs, biases, sinks):
    lo = _lane_lo(q.shape)
    outs = []
    for half in (0, 1):
        qm = jnp.where(lo if half == 0 else jnp.logical_not(lo), q, 0.0).astype(BF16)
        ss = []
        for kb, bb in zip(ks, biases):
            s = _dot_nt(qm, kb)
            if bb is not None:
                s = s + bb[half]
            ss.append(s)
        m = jnp.max(ss[0], axis=-1, keepdims=True)
        for s in ss[1:]:
            m = jnp.maximum(m, jnp.max(s, axis=-1, keepdims=True))
        if sinks is not None:
            m = jnp.maximum(m, sinks[half])
        ps = [jnp.exp(s - m) for s in ss]
        den = jnp.sum(ps[0], axis=-1, keepdims=True)
        for p in ps[1:]:
            den = den + jnp.sum(p, axis=-1, keepdims=True)
        if sinks is not None:
            den = den + jnp.exp(sinks[half] - m)
        o = _dot(ps[0].astype(BF16), vs[0])
        for p, vb in zip(ps[1:], vs[1:]):
            o = o + _dot(p.astype(BF16), vb)
        outs.append(o / den)
    return jnp.where(lo, outs[0], outs[1])


def _block_diag(s0, s1):
    z = jnp.zeros_like(s0)
    return jnp.concatenate([jnp.concatenate([s0, z], axis=1), jnp.concatenate([z, s1], axis=1)], axis=0)


def _ret_pair(q, k, v, g, lgf, lgb, s0f, s0b, seq):
    c = RET_CHUNK
    n = seq // c
    lo1 = _lane_lo((1, LANES))
    lo = _lane_lo((c, LANES))
    lgf_v = jnp.where(lo1, lgf[0], lgf[1])
    lgb_v = jnp.where(lo1, lgb[0], lgb[1])
    pos = lax.broadcasted_iota(jnp.int32, (c, LANES), 0).astype(F32)
    qw_f = jnp.exp(lgf_v * (pos + 1.0))
    kw_f = jnp.exp(lgf_v * (c - 1.0 - pos))
    qw_b = jnp.exp(lgb_v * (c - pos))
    kw_b = jnp.exp(lgb_v * pos)
    gc_f = jnp.exp(lgf_v * float(c))
    gc_b = jnp.exp(lgb_v * float(c))
    diff = (lax.broadcasted_iota(jnp.int32, (c, c), 0)
            - lax.broadcasted_iota(jnp.int32, (c, c), 1)).astype(F32)
    decay = [jnp.where(diff >= 0, jnp.exp(lgf[h] * jnp.maximum(diff, 0.0)), 0.0)
             + jnp.where(diff <= 0, jnp.exp(lgb[h] * jnp.maximum(-diff, 0.0)), 0.0) for h in (0, 1)]
    bd = ((lax.broadcasted_iota(jnp.int32, (LANES, LANES), 0) < HEAD_DIM)
          == (lax.broadcasted_iota(jnp.int32, (LANES, LANES), 1) < HEAD_DIM))
    zero = jnp.zeros((LANES, LANES), F32)
    sf = zero if s0f is None else s0f
    sb = zero if s0b is None else s0b
    qs = [q[i * c:(i + 1) * c] for i in range(n)]
    ks = [k[i * c:(i + 1) * c] * SCALE for i in range(n)]
    vs = [v[i * c:(i + 1) * c].astype(BF16) for i in range(n)]
    outs = []
    for i in range(n):
        kb = ks[i].astype(BF16)
        p0 = (_dot_nt(jnp.where(lo, qs[i], 0.0).astype(BF16), kb) * decay[0]).astype(BF16)
        p1 = (_dot_nt(jnp.where(lo, 0.0, qs[i]).astype(BF16), kb) * decay[1]).astype(BF16)
        o = jnp.where(lo, _dot(p0, vs[i]), _dot(p1, vs[i]))
        o = o + _dot((qs[i] * qw_f).astype(BF16), sf.astype(BF16))
        sf = gc_f * sf + jnp.where(bd, _dot((ks[i] * kw_f).T.astype(BF16), vs[i]), 0.0)
        outs.append(o)
    for i in reversed(range(n)):
        outs[i] = outs[i] + _dot((qs[i] * qw_b).astype(BF16), sb.astype(BF16))
        sb = gc_b * sb + jnp.where(bd, _dot((ks[i] * kw_b).T.astype(BF16), vs[i]), 0.0)
    o = jnp.concatenate(outs, axis=0) if n > 1 else outs[0]
    lo_s = _lane_lo((seq, LANES))
    inv_d = 1.0 / HEAD_DIM
    mu = jnp.where(lo_s, jnp.sum(jnp.where(lo_s, o, 0.0), axis=-1, keepdims=True),
                   jnp.sum(jnp.where(lo_s, 0.0, o), axis=-1, keepdims=True)) * inv_d
    d = o - mu
    d2 = d * d
    var = jnp.where(lo_s, jnp.sum(jnp.where(lo_s, d2, 0.0), axis=-1, keepdims=True),
                    jnp.sum(jnp.where(lo_s, 0.0, d2), axis=-1, keepdims=True)) * inv_d
    y = d * lax.rsqrt(var + EPS) * (g * jax.nn.sigmoid(g))
    return y, sf, sb


N_STATES = 6


def _ctx_mixer_kernel(lgf_ref, lgb_ref, sink_ref, p_ref, o_in_ref, *refs, n_prev):
    prev_refs = refs[:N_STATES] if n_prev else ()
    o_ref, sf_ref, sb_ref, nk_ref, nv_ref, wk_ref, wv_ref = refs[len(prev_refs):]
    for prev, cur in zip(prev_refs, (sf_ref, sb_ref, nk_ref, nv_ref, wk_ref, wv_ref)):
        cur[:n_prev] = prev[...]
    l = n_prev

    def col(blk):
        return p_ref[:, blk * LANES:(blk + 1) * LANES]

    for hp in range(N_RET_PAIRS):
        y, sf, sb = _ret_pair(col(CB_RQ + hp), col(CB_RK + hp), col(CB_RV + hp), col(CB_RG + hp),
                              (lgf_ref[2 * hp], lgf_ref[2 * hp + 1]),
                              (lgb_ref[2 * hp], lgb_ref[2 * hp + 1]), None, None, SEQ)
        o_ref[:, (OB_RET + hp) * LANES:(OB_RET + hp + 1) * LANES] = y.astype(o_ref.dtype)
        sf_ref[l, 2 * hp] = sf[:HEAD_DIM, :HEAD_DIM]
        sf_ref[l, 2 * hp + 1] = sf[HEAD_DIM:, HEAD_DIM:]
        sb_ref[l, 2 * hp] = sb[:HEAD_DIM, :HEAD_DIM]
        sb_ref[l, 2 * hp + 1] = sb[HEAD_DIM:, HEAD_DIM:]

    for hp in range(N_NA_PAIRS):
        k = col(CB_NK + hp)
        v = col(CB_NV + hp)
        o = _attn_pair(col(CB_NQ + hp) * SCALE, [k.astype(BF16)], [v.astype(BF16)], [None], None)
        o_ref[:, (OB_NA + hp) * LANES:(OB_NA + hp + 1) * LANES] = o.astype(o_ref.dtype)
        nk_ref[l, 2 * hp] = k[:, :HEAD_DIM]
        nk_ref[l, 2 * hp + 1] = k[:, HEAD_DIM:]
        nv_ref[l, 2 * hp] = v[:, :HEAD_DIM]
        nv_ref[l, 2 * hp + 1] = v[:, HEAD_DIM:]

    k = col(CB_WK)
    v = col(CB_WV)
    for kv in range(KV_WIN):
        wk_ref[l, kv] = k[:, kv * HEAD_DIM:(kv + 1) * HEAD_DIM]
        wv_ref[l, kv] = v[:, kv * HEAD_DIM:(kv + 1) * HEAD_DIM]
    lo = _lane_lo(k.shape)
    k_sw = pltpu.roll(k, HEAD_DIM, 1)
    v_sw = pltpu.roll(v, HEAD_DIM, 1)
    for hp in range(N_WIN_PAIRS):
        kv_lo = (2 * hp) // G_WIN
        kv_hi = (2 * hp + 1) // G_WIN
        kk = jnp.where(lo, k if kv_lo == 0 else k_sw, k if kv_hi == 1 else k_sw).astype(BF16)
        vv = jnp.where(lo, v if kv_lo == 0 else v_sw, v if kv_hi == 1 else v_sw).astype(BF16)
        o = _attn_pair(col(CB_WQ + hp) * SCALE, [kk], [vv], [None],
                       (sink_ref[2 * hp], sink_ref[2 * hp + 1]))
        o_ref[:, (OB_WIN + hp) * LANES:(OB_WIN + hp + 1) * LANES] = o.astype(o_ref.dtype)


def _smem_spec():
    return pl.BlockSpec(memory_space=pltpu.SMEM)


def _ctx_mixer(proj, lgf, lgb, sink, n_ctx, prev_states):
    t = proj.shape[0]
    n_prev = prev_states[0].shape[1] if prev_states else 0
    o_init = jnp.zeros((t, W_MIX), BF16)
    dims = [(H_RET, HEAD_DIM, HEAD_DIM)] * 2 + [(H_NA, SEQ, HEAD_DIM)] * 2 + [(KV_WIN, SEQ, HEAD_DIM)] * 2
    st = lambda n, d: pl.BlockSpec((None, n) + d, lambda i: (i, 0, 0, 0, 0))
    return pl.pallas_call(
        functools.partial(_ctx_mixer_kernel, n_prev=n_prev),
        out_shape=(jax.ShapeDtypeStruct((t, W_MIX), BF16),
                   *[jax.ShapeDtypeStruct((n_ctx, n_prev + 1) + d, F32) for d in dims]),
        grid=(n_ctx,),
        in_specs=[_smem_spec(), _smem_spec(), _smem_spec(),
                  pl.BlockSpec((SEQ, W_IN), lambda i: (i, 0)), pl.BlockSpec(memory_space=pl.ANY),
                  *[st(n_prev, d) for d in dims[:len(prev_states)]]],
        out_specs=(pl.BlockSpec((SEQ, W_MIX), lambda i: (i, 0)), *[st(n_prev + 1, d) for d in dims]),
        input_output_aliases={4: 0},
        compiler_params=_cparams(("parallel",)), name="ctx_mixer",
    )(lgf, lgb, sink, proj, o_init, *prev_states)


def _ret_lat_kernel(lgf_ref, lgb_ref, q_ref, k_ref, v_ref, g_ref, s0f_ref, s0b_ref, o_in_ref, o_ref):
    hp = pl.program_id(1)
    y, _, _ = _ret_pair(q_ref[...], k_ref[...], v_ref[...], g_ref[...],
                        (lgf_ref[2 * hp], lgf_ref[2 * hp + 1]), (lgb_ref[2 * hp], lgb_ref[2 * hp + 1]),
                        _block_diag(s0f_ref[0], s0f_ref[1]), _block_diag(s0b_ref[0], s0b_ref[1]), DEC_SEQ)
    o_ref[...] = y.astype(o_ref.dtype)


def _ret_latent(proj, o, lgf, lgb, s0f, s0b, l, n_lat):
    rb0 = (proj.shape[0] - n_lat * DEC_SEQ) // DEC_SEQ
    cb = lambda c0: pl.BlockSpec((DEC_SEQ, LANES), lambda b, hp: (rb0 + b, c0 + hp))
    st = pl.BlockSpec((None, None, 2, HEAD_DIM, HEAD_DIM), lambda b, hp: (b, l, hp, 0, 0))
    return pl.pallas_call(
        _ret_lat_kernel, out_shape=jax.ShapeDtypeStruct(o.shape, o.dtype),
        grid=(n_lat, N_RET_PAIRS),
        in_specs=[_smem_spec(), _smem_spec(), cb(CB_RQ), cb(CB_RK), cb(CB_RV), cb(CB_RG), st, st,
                  pl.BlockSpec(memory_space=pl.ANY)],
        out_specs=pl.BlockSpec((DEC_SEQ, LANES), lambda b, hp: (rb0 + b, OB_RET + hp)),
        input_output_aliases={8: 0},
        compiler_params=_cparams(("parallel", "parallel")), name="ret_latent",
    )(lgf, lgb, proj, proj, proj, proj, s0f, s0b, o)


NA_QROWS = 4
NA_KROWS = NA_ROWS + NA_QROWS
N_ROW_OFFS = 2 * NA_ROWS - 1


def _na_lat_kernel(q_ref, k_ref, v_ref, kc_ref, vc_ref, bias_ref, o_in_ref, o_ref):
    kc = jnp.concatenate([kc_ref[0], kc_ref[1]], axis=1).astype(BF16)
    vc = jnp.concatenate([vc_ref[0], vc_ref[1]], axis=1).astype(BF16)
    n_keys = NA_KROWS * GRID_W
    key_row = lax.broadcasted_iota(jnp.int32, (GRID_W, n_keys), 1) // GRID_W
    for blk in range(LAT_ROWS // NA_QROWS):
        w0 = min(max(blk * NA_QROWS - NA_ROWS // 2, 0), LAT_ROWS - NA_KROWS)
        q = q_ref[blk * NA_QROWS * GRID_W:(blk + 1) * NA_QROWS * GRID_W, :] * SCALE
        kw = k_ref[w0 * GRID_W:(w0 + NA_KROWS) * GRID_W, :].astype(BF16)
        vw = v_ref[w0 * GRID_W:(w0 + NA_KROWS) * GRID_W, :].astype(BF16)
        bias = [[], []]
        for qr in range(NA_QROWS):
            r = blk * NA_QROWS + qr
            r0 = min(max(r - NA_ROWS // 2, 0), LAT_ROWS - NA_ROWS)
            in_window = jnp.logical_and(key_row >= r0 - w0, key_row < r0 - w0 + NA_ROWS)
            for h in (0, 1):
                tiles = [bias_ref[h, min(max(w0 + 2 * m - r + NA_ROWS, 0), N_ROW_OFFS)]
                         for m in range(NA_KROWS // 2)]
                bias[h].append(jnp.where(in_window, jnp.concatenate(tiles, axis=1), NEG))
        bias = [jnp.concatenate(b, axis=0) for b in bias]
        o = _attn_pair(q, [kw, kc], [vw, vc], [bias, None], None)
        o_ref[blk * NA_QROWS * GRID_W:(blk + 1) * NA_QROWS * GRID_W, :] = o.astype(o_ref.dtype)


def _na_bias_table(rpb):
    cols = np.arange(GRID_W)
    c0 = np.clip(cols - NA_COLS // 2, 0, GRID_W - NA_COLS)
    col_ok = (cols[None, :] >= c0[:, None]) & (cols[None, :] < c0[:, None] + NA_COLS)
    dc = np.clip(cols[None, :] - cols[:, None], -(NA_COLS - 1), NA_COLS - 1) + (NA_COLS - 1)
    by_col = jnp.where(col_ok[None, None], rpb.astype(F32)[:, :, dc], NEG)
    masked = jnp.full((H_NA, 1, GRID_W, GRID_W), NEG, F32)
    by_col = jnp.concatenate([masked, by_col, masked], axis=1)
    return jnp.concatenate([by_col[:, :-1], by_col[:, 1:]], axis=-1)


def _na_latent(proj, o, kc, vc, bias, l, n_lat):
    rb0 = (proj.shape[0] - n_lat * DEC_SEQ) // DEC_SEQ
    cb = lambda c0: pl.BlockSpec((DEC_SEQ, LANES), lambda b, hp: (rb0 + b, c0 + hp))
    cache = pl.BlockSpec((None, None, 2, PAST_LEN, HEAD_DIM), lambda b, hp: (b, l, hp, 0, 0))
    return pl.pallas_call(
        _na_lat_kernel, out_shape=jax.ShapeDtypeStruct(o.shape, o.dtype),
        grid=(n_lat, N_NA_PAIRS),
        in_specs=[cb(CB_NQ), cb(CB_NK), cb(CB_NV), cache, cache,
                  pl.BlockSpec((2, N_ROW_OFFS + 1, GRID_W, LANES), lambda b, hp: (hp, 0, 0, 0)),
                  pl.BlockSpec(memory_space=pl.ANY)],
        out_specs=pl.BlockSpec((DEC_SEQ, LANES), lambda b, hp: (rb0 + b, OB_NA + hp)),
        input_output_aliases={6: 0},
        compiler_params=_cparams(("parallel", "parallel")), name="na_latent",
    )(proj, proj, proj, kc, vc, bias, o)


def _rope_tables():
    t = jnp.arange(DEC_SEQ)
    d = np.arange(LANES) % HEAD_DIM
    quarter = HEAD_DIM // 4
    inv = ROPE_BASE ** (-jnp.arange(quarter, dtype=F32) / quarter)
    pos = jnp.where(jnp.asarray(d < HEAD_DIM // 2)[None, :], (t // GRID_W)[:, None], (t % GRID_W)[:, None])
    ang = pos.astype(F32) * inv[d % quarter][None, :]
    sign = jnp.asarray(np.where((d & quarter) == 0, -1.0, 1.0), F32)
    return jnp.cos(ang), jnp.sin(ang) * sign[None, :]


def _win_lat_kernel(sink_ref, q_ref, k_ref, v_ref, kc_ref, vc_ref, cos_ref, sin_ref, o_in_ref, o_ref,
                    qs_ref, kp_ref, vp_ref):
    hp = pl.program_id(1)
    quarter = HEAD_DIM // 4
    lane = lax.broadcasted_iota(jnp.int32, (DEC_SEQ, LANES), 1)
    first = (lane & quarter) == 0
    lo = lane < HEAD_DIM
    cos = cos_ref[...]
    sin = sin_ref[...]

    def rope(x):
        sw = jnp.where(first, pltpu.roll(x, LANES - quarter, 1), pltpu.roll(x, quarter, 1))
        return x * cos + sw * sin

    qs_ref[...] = rope(q_ref[...]) * SCALE
    k = rope(k_ref[...])
    v = v_ref[...]
    lo_orig = (2 * hp) // G_WIN == 0
    hi_orig = (2 * hp + 1) // G_WIN == 1
    orig = jnp.where(lo, lo_orig.astype(jnp.int32), hi_orig.astype(jnp.int32)) == 1
    zeros = jnp.zeros((WIN_BLOCK, LANES), BF16)
    kp_ref[:WIN_BLOCK] = zeros
    kp_ref[WIN_BLOCK + DEC_SEQ:] = zeros
    vp_ref[:WIN_BLOCK] = zeros
    vp_ref[WIN_BLOCK + DEC_SEQ:] = zeros
    kp_ref[WIN_BLOCK:WIN_BLOCK + DEC_SEQ] = jnp.where(orig, k, pltpu.roll(k, HEAD_DIM, 1)).astype(BF16)
    vp_ref[WIN_BLOCK:WIN_BLOCK + DEC_SEQ] = jnp.where(orig, v, pltpu.roll(v, HEAD_DIM, 1)).astype(BF16)
    kc = jnp.concatenate([jnp.where(lo_orig, kc_ref[0], kc_ref[1]),
                          jnp.where(hi_orig, kc_ref[1], kc_ref[0])], axis=1).astype(BF16)
    vc = jnp.concatenate([jnp.where(lo_orig, vc_ref[0], vc_ref[1]),
                          jnp.where(hi_orig, vc_ref[1], vc_ref[0])], axis=1).astype(BF16)
    sinks = (sink_ref[2 * hp], sink_ref[2 * hp + 1])
    n_band = WIN_Q + 2 * WIN_BLOCK
    qi = lax.broadcasted_iota(jnp.int32, (WIN_Q, n_band), 0)
    kj = lax.broadcasted_iota(jnp.int32, (WIN_Q, n_band), 1)
    near = jnp.abs(qi + WIN_BLOCK - kj) <= WIN_HALF
    for n in range(DEC_SEQ // WIN_Q):
        q0 = n * WIN_Q
        kpos = q0 - WIN_BLOCK + kj
        ok = jnp.logical_and(near, jnp.logical_and(kpos >= 0, kpos < DEC_SEQ))
        band = jnp.where(ok, 0.0, NEG)
        o = _attn_pair(qs_ref[q0:q0 + WIN_Q, :],
                       [kp_ref[q0:q0 + n_band, :], kc], [vp_ref[q0:q0 + n_band, :], vc],
                       [(band, band), None], sinks)
        o_ref[q0:q0 + WIN_Q, :] = o.astype(o_ref.dtype)


def _win_latent(proj, o, kc, vc, sink, cos, sin, l, n_lat):
    rb0 = (proj.shape[0] - n_lat * DEC_SEQ) // DEC_SEQ
    cache = pl.BlockSpec((None, None, KV_WIN, PAST_LEN, HEAD_DIM), lambda b, hp: (b, l, 0, 0, 0))
    tbl = pl.BlockSpec((DEC_SEQ, LANES), lambda b, hp: (0, 0))
    return pl.pallas_call(
        _win_lat_kernel, out_shape=jax.ShapeDtypeStruct(o.shape, o.dtype),
        grid=(n_lat, N_WIN_PAIRS),
        in_specs=[_smem_spec(),
                  pl.BlockSpec((DEC_SEQ, LANES), lambda b, hp: (rb0 + b, CB_WQ + hp)),
                  pl.BlockSpec((DEC_SEQ, LANES), lambda b, hp: (rb0 + b, CB_WK)),
                  pl.BlockSpec((DEC_SEQ, LANES), lambda b, hp: (rb0 + b, CB_WV)),
                  cache, cache, tbl, tbl, pl.BlockSpec(memory_space=pl.ANY)],
        out_specs=pl.BlockSpec((DEC_SEQ, LANES), lambda b, hp: (rb0 + b, OB_WIN + hp)),
        scratch_shapes=[pltpu.VMEM((DEC_SEQ, LANES), F32),
                        pltpu.VMEM((DEC_SEQ + 2 * WIN_BLOCK, LANES), BF16),
                        pltpu.VMEM((DEC_SEQ + 2 * WIN_BLOCK, LANES), BF16)],
        input_output_aliases={8: 0},
        compiler_params=_cparams(("parallel", "parallel")), name="win_latent",
    )(sink, proj, proj, proj, kc, vc, cos, sin, o)


def _mixers(proj, l, n_ctx, n_lat, prev_states, state_ret_fwd, state_ret_bwd, cache_na_k, cache_na_v,
            cache_win_k, cache_win_v, ret_decay_fwd, ret_decay_bwd, na_rpb, win_sink):
    lgf = jax.nn.log_sigmoid(ret_decay_fwd[l].astype(F32))
    lgb = jax.nn.log_sigmoid(ret_decay_bwd[l].astype(F32))
    sink = win_sink[l].astype(F32)
    o, *states = _ctx_mixer(proj, lgf, lgb, sink, n_ctx, prev_states)
    o = _ret_latent(proj, o, lgf, lgb, state_ret_fwd, state_ret_bwd, l, n_lat)
    o = _na_latent(proj, o, cache_na_k, cache_na_v, _na_bias_table(na_rpb[l]), l, n_lat)
    cos, sin = _rope_tables()
    o = _win_latent(proj, o, cache_win_k, cache_win_v, sink, cos, sin, l, n_lat)
    return o, tuple(states)


def kernel(x_prompt, x_sample, c, state_ret_fwd, state_ret_bwd, cache_na_k, cache_na_v, cache_win_k, cache_win_v, c_ctx, norm1_g, norm2_g, ada_w, ada_b, w_in, w_out, ret_decay_fwd, ret_decay_bwd, na_rpb, win_sink, ffn_w_in, ffn_w_out, moe_router, moe_w_in, moe_w_out, final_norm_g):
    x = jnp.concatenate([x_prompt.reshape(T_CTX, D_MODEL), x_sample.reshape(T_LAT, D_MODEL)], axis=0)
    cond = jnp.concatenate([c_ctx[None, :], c, jnp.zeros((N_GROUPS - 1 - DEC_BATCH, D_MODEL), F32)], axis=0)
    states = ()
    for l in range(DEPTH):
        mod = _adaln(cond, ada_w, ada_b, l)
        h = _norm_mod(x, norm1_g, mod, l, 0, 1)
        proj = _matmul(h, w_in, l)
        o, states = _mixers(proj, l, BATCH, DEC_BATCH, states, state_ret_fwd, state_ret_bwd, cache_na_k, cache_na_v,
                         cache_win_k, cache_win_v, ret_decay_fwd, ret_decay_bwd, na_rpb, win_sink)
        x = _matmul(o, w_out, l, resid=x, mod=mod, k_gate=2)
        i = l // 2
        if l % 2 == 0:
            h = _norm_mod(x, norm2_g, mod, l, 3, 4)
            x = _ffn_dense(h, x, mod, 5, ffn_w_in, ffn_w_out, i, D_FF, FFN_TM)
        else:
            router = jnp.pad(moe_router[i], ((0, 0), (0, LANES - N_EXPERTS)))
            h2, route, counts = _norm_mod(x, norm2_g, mod, l, 3, 4, router=router)
            x = _moe(x, h2, route, counts, moe_w_in[i], moe_w_out[i], mod, 5)
    y_prompt = _final_norm(x, final_norm_g, 0, T_CTX).reshape(BATCH, SEQ, D_MODEL)
    y_sample = _final_norm(x, final_norm_g, T_CTX, T_LAT).reshape(DEC_BATCH, DEC_SEQ, D_MODEL)
    return (y_prompt, y_sample, *states)
```

```python
import functools

import jax
import jax.numpy as jnp
from jax import lax
import numpy as np
from jax.experimental import pallas as pl
from jax.experimental.pallas import tpu as pltpu

D_MODEL = 2048
BATCH = 32
SEQ = 256
DEPTH = 2
DEC_BATCH = 4
DEC_SEQ = 1024
PAST_LEN = 256

GRID_W = 64
HEAD_DIM = 64
H_RET = 12
H_NA = 10
H_WIN = 10
KV_WIN = 2
G_WIN = H_WIN // KV_WIN
W_RET = H_RET * HEAD_DIM
W_NA = H_NA * HEAD_DIM
W_WIN = H_WIN * HEAD_DIM
W_MIX = W_RET + W_NA + W_WIN
W_IN = 4 * W_RET + 3 * W_NA + W_WIN + 2 * KV_WIN * HEAD_DIM
RET_CHUNK = 128
NA_ROWS = 8
NA_COLS = 16
WIN_HALF = 128
WIN_BLOCK = 128
ROPE_BASE = 10000.0
D_FF = 5632
N_EXPERTS = 8
TOP_K = 2
D_FF_EXPERT = 7168
EPS = 1e-6
NEG = -1e30
SCALE = HEAD_DIM ** -0.5

T_CTX = BATCH * SEQ
T_LAT = DEC_BATCH * DEC_SEQ
T_ALL = T_CTX + T_LAT
N_GROUPS = 8
LANES = 128

F32 = jnp.float32
BF16 = jnp.bfloat16

VMEM_LIMIT = 56 * 1024 * 1024


def _group_of_tile(i, tm):
    return jnp.maximum((i * tm - T_CTX) // DEC_SEQ + 1, 0)


def _cparams(sem):
    return pltpu.CompilerParams(dimension_semantics=sem, vmem_limit_bytes=VMEM_LIMIT)


def _adaln_kernel(c_ref, w_ref, b_ref, o_ref):
    c = c_ref[...]
    s = (c * jax.nn.sigmoid(c)).astype(BF16)
    o_ref[...] = jnp.dot(s, w_ref[...].astype(BF16), preferred_element_type=F32) + b_ref[...]


def _adaln(cond, ada_w, ada_b, l):
    tn = 1024
    n = 6 * D_MODEL
    out = pl.pallas_call(
        _adaln_kernel,
        out_shape=jax.ShapeDtypeStruct((N_GROUPS, n), F32),
        grid=(n // tn,),
        in_specs=[
            pl.BlockSpec((N_GROUPS, D_MODEL), lambda j: (0, 0)),
            pl.BlockSpec((None, D_MODEL, tn), lambda j: (l, 0, j)),
            pl.BlockSpec((None, 1, tn), lambda j: (l, 0, j)),
        ],
        out_specs=pl.BlockSpec((N_GROUPS, tn), lambda j: (0, j)),
        compiler_params=_cparams(("arbitrary",)),
        name="adaln",
    )(cond, ada_w, ada_b.reshape(DEPTH, 1, n))
    return out.reshape(N_GROUPS, 6, 1, D_MODEL)


N_SUB = D_MODEL // LANES


def _store_token_major(ref, val):
    rows = val.shape[0]
    for s in range(N_SUB):
        ref[pl.ds(s, rows, stride=N_SUB), :] = val[:, s * LANES:(s + 1) * LANES].astype(ref.dtype)


def _load_token_major(ref, rows, dtype=F32):
    return jnp.concatenate([ref[pl.ds(s, rows, stride=N_SUB), :].astype(dtype) for s in range(N_SUB)],
                           axis=1)


def _token_slab(ref, t):
    return ref.at[pl.ds(pl.multiple_of(t * N_SUB, N_SUB), N_SUB)]


def _norm_mod_body(x_ref, g_ref, sh_ref, sc_ref):
    x = x_ref[...]
    ms = jnp.mean(x * x, axis=-1, keepdims=True)
    y = x * lax.rsqrt(ms + EPS) * g_ref[...]
    return y * (1.0 + sc_ref[...]) + sh_ref[...]


def _norm_mod_kernel(x_ref, g_ref, sh_ref, sc_ref, h_ref):
    h_ref[...] = _norm_mod_body(x_ref, g_ref, sh_ref, sc_ref).astype(h_ref.dtype)


R_E0, R_E1, R_RANK0, R_RANK1, R_W0, R_W1 = range(6)


def _norm_mod_router_kernel(x_ref, g_ref, sh_ref, sc_ref, r_ref, h2_ref, route_ref, cnt_ref, run_ref):
    i = pl.program_id(0)

    @pl.when(i == 0)
    def _():
        run_ref[...] = jnp.zeros_like(run_ref)

    h = _norm_mod_body(x_ref, g_ref, sh_ref, sc_ref)
    _store_token_major(h2_ref, h)
    tm = h.shape[0]
    r = r_ref[...]
    h_hi = h.astype(BF16)
    r_hi = r.astype(BF16)
    h_lo = (h - h_hi.astype(F32)).astype(BF16)
    r_lo = (r - r_hi.astype(F32)).astype(BF16)
    lg = _dot(h_hi, r_hi) + (_dot(h_hi, r_lo) + _dot(h_lo, r_hi))
    lane = lax.broadcasted_iota(jnp.int32, (tm, LANES), 1)
    lane_f = lane.astype(F32)
    lg = jnp.where(lane < N_EXPERTS, lg, -jnp.inf)
    v0 = jnp.max(lg, axis=-1, keepdims=True)
    e0 = jnp.min(jnp.where(lg == v0, lane_f, float(LANES)), axis=-1, keepdims=True)
    lg1 = jnp.where(lane_f == e0, -jnp.inf, lg)
    v1 = jnp.max(lg1, axis=-1, keepdims=True)
    e1 = jnp.min(jnp.where(lg1 == v1, lane_f, float(LANES)), axis=-1, keepdims=True)
    ex = jnp.exp(v1 - v0)
    w0 = 1.0 / (1.0 + ex)
    w1 = ex / (1.0 + ex)
    oh0 = jnp.where(lane_f == e0, 1.0, 0.0)
    oh1 = jnp.where(lane_f == e1, 1.0, 0.0)
    oh = oh0 + oh1
    earlier = (lax.broadcasted_iota(jnp.int32, (tm, tm), 0)
               > lax.broadcasted_iota(jnp.int32, (tm, tm), 1))
    before = jnp.dot(jnp.where(earlier, 1.0, 0.0).astype(BF16), oh.astype(BF16),
                     preferred_element_type=F32) + run_ref[0:1, :]
    rank0 = jnp.sum(oh0 * before, axis=-1, keepdims=True)
    rank1 = jnp.sum(oh1 * before, axis=-1, keepdims=True)
    rec = jnp.zeros((tm, LANES), F32)
    for k, val in ((R_E0, e0), (R_E1, e1), (R_RANK0, rank0), (R_RANK1, rank1), (R_W0, w0), (R_W1, w1)):
        rec = jnp.where(lane == k, val, rec)
    route_ref[...] = rec
    run_ref[0:1, :] = run_ref[0:1, :] + jnp.sum(oh, axis=0, keepdims=True)
    cnt_ref[...] = run_ref[...]


def _norm_mod(x, g, mod, l, k_shift, k_scale, *, router=None):
    tm = 512
    g3 = g.reshape(DEPTH, 1, D_MODEL)
    in_specs = [
        pl.BlockSpec((tm, D_MODEL), lambda i: (i, 0)),
        pl.BlockSpec((None, 1, D_MODEL), lambda i: (l, 0, 0)),
        pl.BlockSpec((None, None, 1, D_MODEL), lambda i: (_group_of_tile(i, tm), k_shift, 0, 0)),
        pl.BlockSpec((None, None, 1, D_MODEL), lambda i: (_group_of_tile(i, tm), k_scale, 0, 0)),
    ]
    if router is not None:
        in_specs.append(pl.BlockSpec((D_MODEL, LANES), lambda i: (0, 0)))
        return pl.pallas_call(
            _norm_mod_router_kernel,
            out_shape=(jax.ShapeDtypeStruct((T_ALL * N_SUB, LANES), F32),
                       jax.ShapeDtypeStruct((T_ALL, LANES), F32),
                       jax.ShapeDtypeStruct((8, LANES), F32)),
            grid=(T_ALL // tm,),
            in_specs=in_specs,
            out_specs=(pl.BlockSpec((tm * N_SUB, LANES), lambda i: (i, 0)),
                       pl.BlockSpec((tm, LANES), lambda i: (i, 0)),
                       pl.BlockSpec((8, LANES), lambda i: (0, 0))),
            scratch_shapes=[pltpu.VMEM((8, LANES), F32)],
            compiler_params=_cparams(("arbitrary",)), name="norm_mod_router",
        )(x, g3, mod, mod, router)
    return pl.pallas_call(
        _norm_mod_kernel, out_shape=jax.ShapeDtypeStruct((T_ALL, D_MODEL), BF16), grid=(T_ALL // tm,),
        in_specs=in_specs, out_specs=pl.BlockSpec((tm, D_MODEL), lambda i: (i, 0)),
        compiler_params=_cparams(("parallel",)), name="norm_mod",
    )(x, g3, mod, mod)


def _mm_kernel(a_ref, w_ref, o_ref):
    o_ref[...] = jnp.dot(a_ref[...].astype(BF16), w_ref[...].astype(BF16),
                         preferred_element_type=F32)


def _mm_res_kernel(a_ref, w_ref, x_ref, gate_ref, o_ref):
    acc = jnp.dot(a_ref[...].astype(BF16), w_ref[...].astype(BF16),
                  preferred_element_type=F32)
    o_ref[...] = x_ref[...] + gate_ref[...] * acc


def _matmul(a, w, l, *, resid=None, mod=None, k_gate=None):
    tm, tn = (2048 if resid is None else DEC_SEQ), 512
    t, k = a.shape
    n = w.shape[-1]
    grid = (t // tm, pl.cdiv(n, tn))
    in_specs = [
        pl.BlockSpec((tm, k), lambda i, j: (i, 0)),
        pl.BlockSpec((None, k, tn), lambda i, j: (l, 0, j)),
    ]
    args = [a, w]
    kern = _mm_kernel
    if resid is not None:
        in_specs += [
            pl.BlockSpec((tm, tn), lambda i, j: (i, j)),
            pl.BlockSpec((None, None, 1, tn), lambda i, j: (_group_of_tile(i, tm), k_gate, 0, j)),
        ]
        args += [resid, mod]
        kern = _mm_res_kernel
    return pl.pallas_call(
        kern, out_shape=jax.ShapeDtypeStruct((t, n), F32), grid=grid,
        in_specs=in_specs, out_specs=pl.BlockSpec((tm, tn), lambda i, j: (i, j)),
        compiler_params=_cparams(("parallel", "arbitrary")), name="proj",
    )(*args)


FFN_TF = 256


def _ffn_step(h_ref, wa_ref, wg_ref, wo_ref, acc_ref, j, n_rows=None):
    @pl.when(j == 0)
    def _():
        acc_ref[...] = jnp.zeros_like(acc_ref)

    def rows(r0, n):
        h = h_ref[r0:r0 + n, :]
        a = jnp.dot(h, wa_ref[...].astype(BF16), preferred_element_type=F32)
        g = jnp.dot(h, wg_ref[...].astype(BF16), preferred_element_type=F32)
        act = (a * jax.nn.sigmoid(a) * g).astype(BF16)
        acc_ref[r0:r0 + n, :] += jnp.dot(act, wo_ref[...].astype(BF16), preferred_element_type=F32)

    tm = h_ref.shape[0]
    if n_rows is None:
        rows(0, tm)
    else:
        quarter = tm // 4
        for k in range(1, 5):
            @pl.when(jnp.logical_and(n_rows > (k - 1) * quarter, n_rows <= k * quarter))
            def _():
                rows(0, k * quarter)


def _ffn_dense_kernel(h_ref, wa_ref, wg_ref, wo_ref, x_ref, gate_ref, o_ref, *, n_f):
    j = pl.program_id(1)
    _ffn_step(h_ref, wa_ref, wg_ref, wo_ref, o_ref, j)

    @pl.when(j == n_f - 1)
    def _():
        o_ref[...] = x_ref[...] + gate_ref[...] * o_ref[...]


def _ffn_dense(h, x, mod, k_gate, w_in, w_out, e, d_ff, tm):
    tf = FFN_TF
    n_f = d_ff // tf
    return pl.pallas_call(
        functools.partial(_ffn_dense_kernel, n_f=n_f),
        out_shape=jax.ShapeDtypeStruct((h.shape[0], D_MODEL), F32),
        grid=(h.shape[0] // tm, n_f),
        in_specs=[
            pl.BlockSpec((tm, D_MODEL), lambda i, j: (i, 0)),
            pl.BlockSpec((None, D_MODEL, tf), lambda i, j: (e, 0, j)),
            pl.BlockSpec((None, D_MODEL, tf), lambda i, j: (e, 0, j + n_f)),
            pl.BlockSpec((None, tf, D_MODEL), lambda i, j: (e, j, 0)),
            pl.BlockSpec((tm, D_MODEL), lambda i, j: (i, 0), pipeline_mode=pl.Buffered(1)),
            pl.BlockSpec((None, None, 1, D_MODEL), lambda i, j: (_group_of_tile(i, tm), k_gate, 0, 0)),
        ],
        out_specs=pl.BlockSpec((tm, D_MODEL), lambda i, j: (i, 0)),
        compiler_params=_cparams(("parallel", "arbitrary")), name="ffn_dense",
    )(h, w_in, w_in, w_out, x, mod)


N_DMA_PRIORITIES = 2


def _ffn_expert_kernel(te_ref, nt_ref, nrows_ref, tok_ref, h2_ref, wa_ref, wg_ref, wo_ref, o_ref,
                       gbuf_ref, hb_ref, acc_ref, sem, *, n_f, tm):
    i = pl.program_id(0)
    j = pl.program_id(1)
    valid = i < nt_ref[0]

    def start_gather(tile):
        def body(p, c):
            for u in range(N_DMA_PRIORITIES):
                r = p * N_DMA_PRIORITIES + u
                pltpu.make_async_copy(_token_slab(h2_ref, tok_ref[tile * tm + r]),
                                      _token_slab(gbuf_ref, r), sem).start(priority=u)
            return c
        lax.fori_loop(0, tm // N_DMA_PRIORITIES, body, 0, unroll=4)

    @pl.when(jnp.logical_and(valid, j == 0))
    def _():
        @pl.when(i == 0)
        def _():
            start_gather(0)

        pltpu.make_async_copy(h2_ref.at[pl.ds(0, tm * N_SUB)], gbuf_ref, sem).wait()
        for s in range(N_SUB):
            hb_ref[:, s * LANES:(s + 1) * LANES] = gbuf_ref[pl.ds(s, tm, stride=N_SUB), :].astype(BF16)

        @pl.when(i + 1 < nt_ref[0])
        def _():
            start_gather(i + 1)

    @pl.when(valid)
    def _():
        _ffn_step(hb_ref, wa_ref, wg_ref, wo_ref, acc_ref, j, nrows_ref[i])

        @pl.when(j == n_f - 1)
        def _():
            _store_token_major(o_ref, acc_ref[...])

    @pl.when(jnp.logical_and(jnp.logical_not(valid), j == 0))
    def _():
        o_ref[...] = jnp.zeros_like(o_ref)


def _ffn_expert(h2, row_token, te, nt, tile_rows, w_in, w_out, d_ff, tm):
    tf = FFN_TF
    r = row_token.shape[0]
    n_f = d_ff // tf

    def jj(i, j, nt_ref):
        return jnp.where(i < nt_ref[0], j, n_f - 1)

    grid_spec = pltpu.PrefetchScalarGridSpec(
        num_scalar_prefetch=4,
        grid=(r // tm, n_f),
        in_specs=[
            pl.BlockSpec(memory_space=pl.ANY),
            pl.BlockSpec((None, D_MODEL, tf), lambda i, j, te_ref, nt_ref, *_: (te_ref[i], 0, jj(i, j, nt_ref))),
            pl.BlockSpec((None, D_MODEL, tf), lambda i, j, te_ref, nt_ref, *_: (te_ref[i], 0, jj(i, j, nt_ref) + n_f)),
            pl.BlockSpec((None, tf, D_MODEL), lambda i, j, te_ref, nt_ref, *_: (te_ref[i], jj(i, j, nt_ref), 0)),
        ],
        out_specs=pl.BlockSpec((tm * N_SUB, LANES), lambda i, j, *_: (i, 0)),
        scratch_shapes=[pltpu.VMEM((tm * N_SUB, LANES), F32), pltpu.VMEM((tm, D_MODEL), BF16),
                        pltpu.VMEM((tm, D_MODEL), F32), pltpu.SemaphoreType.DMA],
    )
    return pl.pallas_call(
        functools.partial(_ffn_expert_kernel, n_f=n_f, tm=tm),
        out_shape=jax.ShapeDtypeStruct((r * N_SUB, LANES), F32),
        grid_spec=grid_spec,
        compiler_params=_cparams(("arbitrary", "arbitrary")), name="ffn_expert",
    )(te, nt, tile_rows, row_token, h2, w_in, w_in, w_out)


COMBINE_TM = 256


def _combine_kernel(eid_ref, rank_ref, first_row_ref, x_ref, route_ref, gate_ref, ys2_ref, *refs,
                    final):
    if final:
        fg_ref, yc_ref, yl_ref, buf_ref, sem = refs
    else:
        o_ref, buf_ref, sem = refs
    i = pl.program_id(0)
    n = pl.num_programs(0)
    tm = COMBINE_TM

    def start_tile(tile, slot):
        def start(t, c):
            for k in range(TOP_K):
                a = (tile * tm + t) * TOP_K + k
                row = first_row_ref[eid_ref[a]] + rank_ref[a]
                pltpu.make_async_copy(_token_slab(ys2_ref, row), _token_slab(buf_ref.at[slot, k], t),
                                      sem.at[slot]).start(priority=k % N_DMA_PRIORITIES)
            return c
        lax.fori_loop(0, tm, start, 0, unroll=4)

    @pl.when(i == 0)
    def _():
        start_tile(0, 0)

    @pl.when(i + 1 < n)
    def _():
        start_tile(i + 1, (i + 1) % 2)

    slot = i % 2
    for k in range(TOP_K):
        pltpu.make_async_copy(ys2_ref.at[pl.ds(0, tm * N_SUB)], buf_ref.at[slot, k], sem.at[slot]).wait()

    rec = route_ref[...]
    w0 = rec[:, R_W0:R_W0 + 1]
    w1 = rec[:, R_W1:R_W1 + 1]
    y = (w0 * _load_token_major(buf_ref.at[slot, 0], tm) + w1 * _load_token_major(buf_ref.at[slot, 1], tm))
    val = x_ref[...] + gate_ref[...] * y
    if not final:
        o_ref[...] = val
        return
    val = val * lax.rsqrt(jnp.mean(val * val, axis=-1, keepdims=True) + EPS) * fg_ref[...]

    @pl.when(i < T_CTX // tm)
    def _():
        yc_ref[...] = val

    @pl.when(i >= T_CTX // tm)
    def _():
        yl_ref[...] = val


def _combine(x, ys2, route, eid, rank, first_row, mod, k_gate, final_g=None):
    tm = COMBINE_TM
    final = final_g is not None
    n_ctx = T_CTX // tm
    row_spec = pl.BlockSpec((tm, D_MODEL), lambda i, *_: (i, 0))
    if final:
        extra_in = [pl.BlockSpec((1, D_MODEL), lambda i, *_: (0, 0))]
        extra_args = [final_g.reshape(1, D_MODEL)]
        out_specs = (pl.BlockSpec((tm, D_MODEL), lambda i, *_: (jnp.minimum(i, n_ctx - 1), 0)),
                     pl.BlockSpec((tm, D_MODEL), lambda i, *_: (jnp.maximum(i - n_ctx, 0), 0)))
        out_shape = (jax.ShapeDtypeStruct((T_CTX, D_MODEL), F32), jax.ShapeDtypeStruct((T_LAT, D_MODEL), F32))
    else:
        extra_in, extra_args, out_specs = [], [], row_spec
        out_shape = jax.ShapeDtypeStruct((T_ALL, D_MODEL), F32)
    grid_spec = pltpu.PrefetchScalarGridSpec(
        num_scalar_prefetch=3,
        grid=(T_ALL // tm,),
        in_specs=[
            row_spec,
            pl.BlockSpec((tm, LANES), lambda i, *_: (i, 0)),
            pl.BlockSpec((None, None, 1, D_MODEL), lambda i, *_: (_group_of_tile(i, tm), k_gate, 0, 0)),
            pl.BlockSpec(memory_space=pl.ANY),
            *extra_in,
        ],
        out_specs=out_specs,
        scratch_shapes=[pltpu.VMEM((2, TOP_K, tm * N_SUB, LANES), F32), pltpu.SemaphoreType.DMA((2,))],
    )
    return pl.pallas_call(
        functools.partial(_combine_kernel, final=final),
        out_shape=out_shape,
        grid_spec=grid_spec,
        compiler_params=_cparams(("arbitrary",)), name="moe_combine",
    )(eid, rank, first_row, x, route, mod, ys2, *extra_args)


def _final_norm_kernel(x_ref, g_ref, o_ref):
    x = x_ref[...]
    ms = jnp.mean(x * x, axis=-1, keepdims=True)
    o_ref[...] = x * lax.rsqrt(ms + EPS) * g_ref[...]


def _final_norm(x, g, row0, rows):
    tm = 512
    return pl.pallas_call(
        _final_norm_kernel,
        out_shape=jax.ShapeDtypeStruct((rows, D_MODEL), F32), grid=(rows // tm,),
        in_specs=[pl.BlockSpec((tm, D_MODEL), lambda i: (row0 // tm + i, 0)),
                  pl.BlockSpec((1, D_MODEL), lambda i: (0, 0))],
        out_specs=pl.BlockSpec((tm, D_MODEL), lambda i: (i, 0)),
        compiler_params=_cparams(("parallel",)), name="final_norm",
    )(x, g.reshape(1, D_MODEL))


FFN_TM = 1024
MOE_TM = 1024
MOE_TILES = (TOP_K * T_ALL) // MOE_TM + N_EXPERTS


def _moe(x, h2, route, counts, w_in, w_out, mod, k_gate, final_g=None):
    tm = MOE_TM
    eid = route[:, R_E0:R_E1 + 1].astype(jnp.int32).reshape(-1)
    rank = route[:, R_RANK0:R_RANK1 + 1].astype(jnp.int32).reshape(-1)
    counts = counts[0, :N_EXPERTS].astype(jnp.int32)
    ptiles = (counts + tm - 1) // tm
    tile_end = jnp.cumsum(ptiles)
    first_row = (tile_end - ptiles) * tm
    n_tiles = tile_end[-1]
    tile_ids = jnp.arange(MOE_TILES, dtype=jnp.int32)
    te = jnp.sum((tile_ids[:, None] >= tile_end[None, :]).astype(jnp.int32), axis=1)
    te = jnp.minimum(te, N_EXPERTS - 1)
    te_last = jnp.sum(jnp.where(tile_ids == n_tiles - 1, te, 0))
    te = jnp.where(tile_ids < n_tiles, te, te_last)
    experts = jnp.arange(N_EXPERTS, dtype=jnp.int32)
    row_end = jnp.sum(jnp.where(te[:, None] == experts[None, :], (first_row + counts)[None, :], 0), axis=1)
    tile_rows = jnp.clip(row_end - tile_ids * tm, 1, tm)
    pos = jnp.sum(jnp.where(eid[:, None] == experts[None, :], first_row[None, :], 0), axis=1) + rank
    tok = jnp.arange(TOP_K * T_ALL, dtype=jnp.int32) // TOP_K
    row_token = jnp.zeros((MOE_TILES * tm,), jnp.int32).at[pos].set(tok)
    ys2 = _ffn_expert(h2, row_token, te, n_tiles.reshape(1), tile_rows, w_in, w_out, D_FF_EXPERT, tm)
    return _combine(x, ys2, route, eid, rank, first_row, mod, k_gate, final_g)


CB_RQ, CB_RK, CB_RV, CB_RG = 0, 6, 12, 18
CB_NQ, CB_NK, CB_NV = 24, 29, 34
CB_WQ, CB_WK, CB_WV = 39, 44, 45
OB_RET, OB_NA, OB_WIN = 0, 6, 11
N_RET_PAIRS = H_RET // 2
N_NA_PAIRS = H_NA // 2
N_WIN_PAIRS = H_WIN // 2
NA_WIN_KEYS = NA_ROWS * GRID_W
LAT_ROWS = DEC_SEQ // GRID_W
WIN_Q = 2 * WIN_BLOCK


def _lane_lo(shape):
    return lax.broadcasted_iota(jnp.int32, shape, len(shape) - 1) < HEAD_DIM


def _dot_nt(a, b):
    return lax.dot_general(a, b, (((1,), (1,)), ((), ())), preferred_element_type=F32)


def _dot(a, b):
    return jnp.dot(a, b, preferred_element_type=F32)


def _attn_pair(q, ks, vs, biases, sinks):
    lo = _lane_lo(q.shape)
    outs = []
    for half in (0, 1):
        qm = jnp.where(lo if half == 0 else jnp.logical_not(lo), q, 0.0).astype(BF16)
        ss = []
        for kb, bb in zip(ks, biases):
            s = _dot_nt(qm, kb)
            if bb is not None:
                s = s + bb[half]
            ss.append(s)
        m = jnp.max(ss[0], axis=-1, keepdims=True)
        for s in ss[1:]:
            m = jnp.maximum(m, jnp.max(s, axis=-1, keepdims=True))
        if sinks is not None:
            m = jnp.maximum(m, sinks[half])
        ps = [jnp.exp(s - m) for s in ss]
        den = jnp.sum(ps[0], axis=-1, keepdims=True)
        for p in ps[1:]:
            den = den + jnp.sum(p, axis=-1, keepdims=True)
        if sinks is not None:
            den = den + jnp.exp(sinks[half] - m)
        o = _dot(ps[0].astype(BF16), vs[0])
        for p, vb in zip(ps[1:], vs[1:]):
            o = o + _dot(p.astype(BF16), vb)
        outs.append(o / den)
    return jnp.where(lo, outs[0], outs[1])


def _block_diag(s0, s1):
    z = jnp.zeros_like(s0)
    return jnp.concatenate([jnp.concatenate([s0, z], axis=1), jnp.concatenate([z, s1], axis=1)], axis=0)


def _ret_pair(q, k, v, g, lgf, lgb, s0f, s0b, seq):
    c = RET_CHUNK
    n = seq // c
    lo1 = _lane_lo((1, LANES))
    lo = _lane_lo((c, LANES))
    lgf_v = jnp.where(lo1, lgf[0], lgf[1])
    lgb_v = jnp.where(lo1, lgb[0], lgb[1])
    pos = lax.broadcasted_iota(jnp.int32, (c, LANES), 0).astype(F32)
    qw_f = jnp.exp(lgf_v * (pos + 1.0))
    kw_f = jnp.exp(lgf_v * (c - 1.0 - pos))
    qw_b = jnp.exp(lgb_v * (c - pos))
    kw_b = jnp.exp(lgb_v * pos)
    gc_f = jnp.exp(lgf_v * float(c))
    gc_b = jnp.exp(lgb_v * float(c))
    diff = (lax.broadcasted_iota(jnp.int32, (c, c), 0)
            - lax.broadcasted_iota(jnp.int32, (c, c), 1)).astype(F32)
    decay = [jnp.where(diff >= 0, jnp.exp(lgf[h] * jnp.maximum(diff, 0.0)), 0.0)
             + jnp.where(diff <= 0, jnp.exp(lgb[h] * jnp.maximum(-diff, 0.0)), 0.0) for h in (0, 1)]
    bd = ((lax.broadcasted_iota(jnp.int32, (LANES, LANES), 0) < HEAD_DIM)
          == (lax.broadcasted_iota(jnp.int32, (LANES, LANES), 1) < HEAD_DIM))
    zero = jnp.zeros((LANES, LANES), F32)
    sf = zero if s0f is None else s0f
    sb = zero if s0b is None else s0b
    qs = [q[i * c:(i + 1) * c] for i in range(n)]
    ks = [k[i * c:(i + 1) * c] * SCALE for i in range(n)]
    vs = [v[i * c:(i + 1) * c].astype(BF16) for i in range(n)]
    outs = []
    for i in range(n):
        kb = ks[i].astype(BF16)
        p0 = (_dot_nt(jnp.where(lo, qs[i], 0.0).astype(BF16), kb) * decay[0]).astype(BF16)
        p1 = (_dot_nt(jnp.where(lo, 0.0, qs[i]).astype(BF16), kb) * decay[1]).astype(BF16)
        o = jnp.where(lo, _dot(p0, vs[i]), _dot(p1, vs[i]))
        o = o + _dot((qs[i] * qw_f).astype(BF16), sf.astype(BF16))
        sf = gc_f * sf + jnp.where(bd, _dot((ks[i] * kw_f).T.astype(BF16), vs[i]), 0.0)
        outs.append(o)
    for i in reversed(range(n)):
        outs[i] = outs[i] + _dot((qs[i] * qw_b).astype(BF16), sb.astype(BF16))
        sb = gc_b * sb + jnp.where(bd, _dot((ks[i] * kw_b).T.astype(BF16), vs[i]), 0.0)
    o = jnp.concatenate(outs, axis=0) if n > 1 else outs[0]
    lo_s = _lane_lo((seq, LANES))
    inv_d = 1.0 / HEAD_DIM
    mu = jnp.where(lo_s, jnp.sum(jnp.where(lo_s, o, 0.0), axis=-1, keepdims=True),
                   jnp.sum(jnp.where(lo_s, 0.0, o), axis=-1, keepdims=True)) * inv_d
    d = o - mu
    d2 = d * d
    var = jnp.where(lo_s, jnp.sum(jnp.where(lo_s, d2, 0.0), axis=-1, keepdims=True),
                    jnp.sum(jnp.where(lo_s, 0.0, d2), axis=-1, keepdims=True)) * inv_d
    y = d * lax.rsqrt(var + EPS) * (g * jax.nn.sigmoid(g))
    return y, sf, sb


N_STATES = 6


def _ctx_mixer_kernel(lgf_ref, lgb_ref, sink_ref, p_ref, o_in_ref, *refs, n_prev):
    prev_refs = refs[:N_STATES] if n_prev else ()
    o_ref, sf_ref, sb_ref, nk_ref, nv_ref, wk_ref, wv_ref = refs[len(prev_refs):]
    for prev, cur in zip(prev_refs, (sf_ref, sb_ref, nk_ref, nv_ref, wk_ref, wv_ref)):
        cur[:n_prev] = prev[...]
    l = n_prev

    def col(blk):
        return p_ref[:, blk * LANES:(blk + 1) * LANES]

    for hp in range(N_RET_PAIRS):
        y, sf, sb = _ret_pair(col(CB_RQ + hp), col(CB_RK + hp), col(CB_RV + hp), col(CB_RG + hp),
                              (lgf_ref[2 * hp], lgf_ref[2 * hp + 1]),
                              (lgb_ref[2 * hp], lgb_ref[2 * hp + 1]), None, None, SEQ)
        o_ref[:, (OB_RET + hp) * LANES:(OB_RET + hp + 1) * LANES] = y.astype(o_ref.dtype)
        sf_ref[l, 2 * hp] = sf[:HEAD_DIM, :HEAD_DIM]
        sf_ref[l, 2 * hp + 1] = sf[HEAD_DIM:, HEAD_DIM:]
        sb_ref[l, 2 * hp] = sb[:HEAD_DIM, :HEAD_DIM]
        sb_ref[l, 2 * hp + 1] = sb[HEAD_DIM:, HEAD_DIM:]

    for hp in range(N_NA_PAIRS):
        k = col(CB_NK + hp)
        v = col(CB_NV + hp)
        o = _attn_pair(col(CB_NQ + hp) * SCALE, [k.astype(BF16)], [v.astype(BF16)], [None], None)
        o_ref[:, (OB_NA + hp) * LANES:(OB_NA + hp + 1) * LANES] = o.astype(o_ref.dtype)
        nk_ref[l, 2 * hp] = k[:, :HEAD_DIM]
        nk_ref[l, 2 * hp + 1] = k[:, HEAD_DIM:]
        nv_ref[l, 2 * hp] = v[:, :HEAD_DIM]
        nv_ref[l, 2 * hp + 1] = v[:, HEAD_DIM:]

    k = col(CB_WK)
    v = col(CB_WV)
    for kv in range(KV_WIN):
        wk_ref[l, kv] = k[:, kv * HEAD_DIM:(kv + 1) * HEAD_DIM]
        wv_ref[l, kv] = v[:, kv * HEAD_DIM:(kv + 1) * HEAD_DIM]
    lo = _lane_lo(k.shape)
    k_sw = pltpu.roll(k, HEAD_DIM, 1)
    v_sw = pltpu.roll(v, HEAD_DIM, 1)
    for hp in range(N_WIN_PAIRS):
        kv_lo = (2 * hp) // G_WIN
        kv_hi = (2 * hp + 1) // G_WIN
        kk = jnp.where(lo, k if kv_lo == 0 else k_sw, k if kv_hi == 1 else k_sw).astype(BF16)
        vv = jnp.where(lo, v if kv_lo == 0 else v_sw, v if kv_hi == 1 else v_sw).astype(BF16)
        o = _attn_pair(col(CB_WQ + hp) * SCALE, [kk], [vv], [None],
                       (sink_ref[2 * hp], sink_ref[2 * hp + 1]))
        o_ref[:, (OB_WIN + hp) * LANES:(OB_WIN + hp + 1) * LANES] = o.astype(o_ref.dtype)


def _smem_spec():
    return pl.BlockSpec(memory_space=pltpu.SMEM)


def _ctx_mixer(proj, lgf, lgb, sink, n_ctx, prev_states):
    t = proj.shape[0]
    n_prev = prev_states[0].shape[1] if prev_states else 0
    o_init = jnp.zeros((t, W_MIX), BF16)
    dims = [(H_RET, HEAD_DIM, HEAD_DIM)] * 2 + [(H_NA, SEQ, HEAD_DIM)] * 2 + [(KV_WIN, SEQ, HEAD_DIM)] * 2
    st = lambda n, d: pl.BlockSpec((None, n) + d, lambda i: (i, 0, 0, 0, 0))
    return pl.pallas_call(
        functools.partial(_ctx_mixer_kernel, n_prev=n_prev),
        out_shape=(jax.ShapeDtypeStruct((t, W_MIX), BF16),
                   *[jax.ShapeDtypeStruct((n_ctx, n_prev + 1) + d, F32) for d in dims]),
        grid=(n_ctx,),
        in_specs=[_smem_spec(), _smem_spec(), _smem_spec(),
                  pl.BlockSpec((SEQ, W_IN), lambda i: (i, 0)), pl.BlockSpec(memory_space=pl.ANY),
                  *[st(n_prev, d) for d in dims[:len(prev_states)]]],
        out_specs=(pl.BlockSpec((SEQ, W_MIX), lambda i: (i, 0)), *[st(n_prev + 1, d) for d in dims]),
        input_output_aliases={4: 0},
        compiler_params=_cparams(("parallel",)), name="ctx_mixer",
    )(lgf, lgb, sink, proj, o_init, *prev_states)


def _ret_lat_kernel(lgf_ref, lgb_ref, q_ref, k_ref, v_ref, g_ref, s0f_ref, s0b_ref, o_in_ref, o_ref):
    hp = pl.program_id(1)
    y, _, _ = _ret_pair(q_ref[...], k_ref[...], v_ref[...], g_ref[...],
                        (lgf_ref[2 * hp], lgf_ref[2 * hp + 1]), (lgb_ref[2 * hp], lgb_ref[2 * hp + 1]),
                        _block_diag(s0f_ref[0], s0f_ref[1]), _block_diag(s0b_ref[0], s0b_ref[1]), DEC_SEQ)
    o_ref[...] = y.astype(o_ref.dtype)


def _ret_latent(proj, o, lgf, lgb, s0f, s0b, l, n_lat):
    rb0 = (proj.shape[0] - n_lat * DEC_SEQ) // DEC_SEQ
    cb = lambda c0: pl.BlockSpec((DEC_SEQ, LANES), lambda b, hp: (rb0 + b, c0 + hp))
    st = pl.BlockSpec((None, None, 2, HEAD_DIM, HEAD_DIM), lambda b, hp: (b, l, hp, 0, 0))
    return pl.pallas_call(
        _ret_lat_kernel, out_shape=jax.ShapeDtypeStruct(o.shape, o.dtype),
        grid=(n_lat, N_RET_PAIRS),
        in_specs=[_smem_spec(), _smem_spec(), cb(CB_RQ), cb(CB_RK), cb(CB_RV), cb(CB_RG), st, st,
                  pl.BlockSpec(memory_space=pl.ANY)],
        out_specs=pl.BlockSpec((DEC_SEQ, LANES), lambda b, hp: (rb0 + b, OB_RET + hp)),
        input_output_aliases={8: 0},
        compiler_params=_cparams(("parallel", "parallel")), name="ret_latent",
    )(lgf, lgb, proj, proj, proj, proj, s0f, s0b, o)


NA_QROWS = 4
NA_KROWS = NA_ROWS + NA_QROWS
N_ROW_OFFS = 2 * NA_ROWS - 1


def _na_lat_kernel(q_ref, k_ref, v_ref, kc_ref, vc_ref, bias_ref, o_in_ref, o_ref):
    kc = jnp.concatenate([kc_ref[0], kc_ref[1]], axis=1).astype(BF16)
    vc = jnp.concatenate([vc_ref[0], vc_ref[1]], axis=1).astype(BF16)
    n_keys = NA_KROWS * GRID_W
    key_row = lax.broadcasted_iota(jnp.int32, (GRID_W, n_keys), 1) // GRID_W
    for blk in range(LAT_ROWS // NA_QROWS):
        w0 = min(max(blk * NA_QROWS - NA_ROWS // 2, 0), LAT_ROWS - NA_KROWS)
        q = q_ref[blk * NA_QROWS * GRID_W:(blk + 1) * NA_QROWS * GRID_W, :] * SCALE
        kw = k_ref[w0 * GRID_W:(w0 + NA_KROWS) * GRID_W, :].astype(BF16)
        vw = v_ref[w0 * GRID_W:(w0 + NA_KROWS) * GRID_W, :].astype(BF16)
        bias = [[], []]
        for qr in range(NA_QROWS):
            r = blk * NA_QROWS + qr
            r0 = min(max(r - NA_ROWS // 2, 0), LAT_ROWS - NA_ROWS)
            in_window = jnp.logical_and(key_row >= r0 - w0, key_row < r0 - w0 + NA_ROWS)
            for h in (0, 1):
                tiles = [bias_ref[h, min(max(w0 + 2 * m - r + NA_ROWS, 0), N_ROW_OFFS)]
                         for m in range(NA_KROWS // 2)]
                bias[h].append(jnp.where(in_window, jnp.concatenate(tiles, axis=1), NEG))
        bias = [jnp.concatenate(b, axis=0) for b in bias]
        o = _attn_pair(q, [kw, kc], [vw, vc], [bias, None], None)
        o_ref[blk * NA_QROWS * GRID_W:(blk + 1) * NA_QROWS * GRID_W, :] = o.astype(o_ref.dtype)


def _na_bias_table(rpb):
    cols = np.arange(GRID_W)
    c0 = np.clip(cols - NA_COLS // 2, 0, GRID_W - NA_COLS)
    col_ok = (cols[None, :] >= c0[:, None]) & (cols[None, :] < c0[:, None] + NA_COLS)
    dc = np.clip(cols[None, :] - cols[:, None], -(NA_COLS - 1), NA_COLS - 1) + (NA_COLS - 1)
    by_col = jnp.where(col_ok[None, None], rpb.astype(F32)[:, :, dc], NEG)
    masked = jnp.full((H_NA, 1, GRID_W, GRID_W), NEG, F32)
    by_col = jnp.concatenate([masked, by_col, masked], axis=1)
    return jnp.concatenate([by_col[:, :-1], by_col[:, 1:]], axis=-1)


def _na_latent(proj, o, kc, vc, bias, l, n_lat):
    rb0 = (proj.shape[0] - n_lat * DEC_SEQ) // DEC_SEQ
    cb = lambda c0: pl.BlockSpec((DEC_SEQ, LANES), lambda b, hp: (rb0 + b, c0 + hp))
    cache = pl.BlockSpec((None, None, 2, PAST_LEN, HEAD_DIM), lambda b, hp: (b, l, hp, 0, 0))
    return pl.pallas_call(
        _na_lat_kernel, out_shape=jax.ShapeDtypeStruct(o.shape, o.dtype),
        grid=(n_lat, N_NA_PAIRS),
        in_specs=[cb(CB_NQ), cb(CB_NK), cb(CB_NV), cache, cache,
                  pl.BlockSpec((2, N_ROW_OFFS + 1, GRID_W, LANES), lambda b, hp: (hp, 0, 0, 0)),
                  pl.BlockSpec(memory_space=pl.ANY)],
        out_specs=pl.BlockSpec((DEC_SEQ, LANES), lambda b, hp: (rb0 + b, OB_NA + hp)),
        input_output_aliases={6: 0},
        compiler_params=_cparams(("parallel", "parallel")), name="na_latent",
    )(proj, proj, proj, kc, vc, bias, o)


def _rope_tables():
    t = jnp.arange(DEC_SEQ)
    d = np.arange(LANES) % HEAD_DIM
    quarter = HEAD_DIM // 4
    inv = ROPE_BASE ** (-jnp.arange(quarter, dtype=F32) / quarter)
    pos = jnp.where(jnp.asarray(d < HEAD_DIM // 2)[None, :], (t // GRID_W)[:, None], (t % GRID_W)[:, None])
    ang = pos.astype(F32) * inv[d % quarter][None, :]
    sign = jnp.asarray(np.where((d & quarter) == 0, -1.0, 1.0), F32)
    return jnp.cos(ang), jnp.sin(ang) * sign[None, :]


def _win_lat_kernel(sink_ref, q_ref, k_ref, v_ref, kc_ref, vc_ref, cos_ref, sin_ref, o_in_ref, o_ref,
                    qs_ref, kp_ref, vp_ref):
    hp = pl.program_id(1)
    quarter = HEAD_DIM // 4
    lane = lax.broadcasted_iota(jnp.int32, (DEC_SEQ, LANES), 1)
    first = (lane & quarter) == 0
    lo = lane < HEAD_DIM
    cos = cos_ref[...]
    sin = sin_ref[...]

    def rope(x):
        sw = jnp.where(first, pltpu.roll(x, LANES - quarter, 1), pltpu.roll(x, quarter, 1))
        return x * cos + sw * sin

    qs_ref[...] = rope(q_ref[...]) * SCALE
    k = rope(k_ref[...])
    v = v_ref[...]
    lo_orig = (2 * hp) // G_WIN == 0
    hi_orig = (2 * hp + 1) // G_WIN == 1
    orig = jnp.where(lo, lo_orig.astype(jnp.int32), hi_orig.astype(jnp.int32)) == 1
    zeros = jnp.zeros((WIN_BLOCK, LANES), BF16)
    kp_ref[:WIN_BLOCK] = zeros
    kp_ref[WIN_BLOCK + DEC_SEQ:] = zeros
    vp_ref[:WIN_BLOCK] = zeros
    vp_ref[WIN_BLOCK + DEC_SEQ:] = zeros
    kp_ref[WIN_BLOCK:WIN_BLOCK + DEC_SEQ] = jnp.where(orig, k, pltpu.roll(k, HEAD_DIM, 1)).astype(BF16)
    vp_ref[WIN_BLOCK:WIN_BLOCK + DEC_SEQ] = jnp.where(orig, v, pltpu.roll(v, HEAD_DIM, 1)).astype(BF16)
    kc = jnp.concatenate([jnp.where(lo_orig, kc_ref[0], kc_ref[1]),
                          jnp.where(hi_orig, kc_ref[1], kc_ref[0])], axis=1).astype(BF16)
    vc = jnp.concatenate([jnp.where(lo_orig, vc_ref[0], vc_ref[1]),
                          jnp.where(hi_orig, vc_ref[1], vc_ref[0])], axis=1).astype(BF16)
    sinks = (sink_ref[2 * hp], sink_ref[2 * hp + 1])
    n_band = WIN_Q + 2 * WIN_BLOCK
    qi = lax.broadcasted_iota(jnp.int32, (WIN_Q, n_band), 0)
    kj = lax.broadcasted_iota(jnp.int32, (WIN_Q, n_band), 1)
    near = jnp.abs(qi + WIN_BLOCK - kj) <= WIN_HALF
    for n in range(DEC_SEQ // WIN_Q):
        q0 = n * WIN_Q
        kpos = q0 - WIN_BLOCK + kj
        ok = jnp.logical_and(near, jnp.logical_and(kpos >= 0, kpos < DEC_SEQ))
        band = jnp.where(ok, 0.0, NEG)
        o = _attn_pair(qs_ref[q0:q0 + WIN_Q, :],
                       [kp_ref[q0:q0 + n_band, :], kc], [vp_ref[q0:q0 + n_band, :], vc],
                       [(band, band), None], sinks)
        o_ref[q0:q0 + WIN_Q, :] = o.astype(o_ref.dtype)


def _win_latent(proj, o, kc, vc, sink, cos, sin, l, n_lat):
    rb0 = (proj.shape[0] - n_lat * DEC_SEQ) // DEC_SEQ
    cache = pl.BlockSpec((None, None, KV_WIN, PAST_LEN, HEAD_DIM), lambda b, hp: (b, l, 0, 0, 0))
    tbl = pl.BlockSpec((DEC_SEQ, LANES), lambda b, hp: (0, 0))
    return pl.pallas_call(
        _win_lat_kernel, out_shape=jax.ShapeDtypeStruct(o.shape, o.dtype),
        grid=(n_lat, N_WIN_PAIRS),
        in_specs=[_smem_spec(),
                  pl.BlockSpec((DEC_SEQ, LANES), lambda b, hp: (rb0 + b, CB_WQ + hp)),
                  pl.BlockSpec((DEC_SEQ, LANES), lambda b, hp: (rb0 + b, CB_WK)),
                  pl.BlockSpec((DEC_SEQ, LANES), lambda b, hp: (rb0 + b, CB_WV)),
                  cache, cache, tbl, tbl, pl.BlockSpec(memory_space=pl.ANY)],
        out_specs=pl.BlockSpec((DEC_SEQ, LANES), lambda b, hp: (rb0 + b, OB_WIN + hp)),
        scratch_shapes=[pltpu.VMEM((DEC_SEQ, LANES), F32),
                        pltpu.VMEM((DEC_SEQ + 2 * WIN_BLOCK, LANES), BF16),
                        pltpu.VMEM((DEC_SEQ + 2 * WIN_BLOCK, LANES), BF16)],
        input_output_aliases={8: 0},
        compiler_params=_cparams(("parallel", "parallel")), name="win_latent",
    )(sink, proj, proj, proj, kc, vc, cos, sin, o)


def _mixers(proj, l, n_ctx, n_lat, prev_states, state_ret_fwd, state_ret_bwd, cache_na_k, cache_na_v,
            cache_win_k, cache_win_v, ret_decay_fwd, ret_decay_bwd, na_rpb, win_sink):
    lgf = jax.nn.log_sigmoid(ret_decay_fwd[l].astype(F32))
    lgb = jax.nn.log_sigmoid(ret_decay_bwd[l].astype(F32))
    sink = win_sink[l].astype(F32)
    o, *states = _ctx_mixer(proj, lgf, lgb, sink, n_ctx, prev_states)
    o = _ret_latent(proj, o, lgf, lgb, state_ret_fwd, state_ret_bwd, l, n_lat)
    o = _na_latent(proj, o, cache_na_k, cache_na_v, _na_bias_table(na_rpb[l]), l, n_lat)
    cos, sin = _rope_tables()
    o = _win_latent(proj, o, cache_win_k, cache_win_v, sink, cos, sin, l, n_lat)
    return o, tuple(states)


def kernel(x_prompt, x_sample, c, state_ret_fwd, state_ret_bwd, cache_na_k, cache_na_v, cache_win_k, cache_win_v, c_ctx, norm1_g, norm2_g, ada_w, ada_b, w_in, w_out, ret_decay_fwd, ret_decay_bwd, na_rpb, win_sink, ffn_w_in, ffn_w_out, moe_router, moe_w_in, moe_w_out, final_norm_g):
    x = jnp.concatenate([x_prompt.reshape(T_CTX, D_MODEL), x_sample.reshape(T_LAT, D_MODEL)], axis=0)
    cond = jnp.concatenate([c_ctx[None, :], c, jnp.zeros((N_GROUPS - 1 - DEC_BATCH, D_MODEL), F32)], axis=0)
    states = ()
    for l in range(DEPTH):
        mod = _adaln(cond, ada_w, ada_b, l)
        h = _norm_mod(x, norm1_g, mod, l, 0, 1)
        proj = _matmul(h, w_in, l)
        o, states = _mixers(proj, l, BATCH, DEC_BATCH, states, state_ret_fwd, state_ret_bwd, cache_na_k, cache_na_v,
                         cache_win_k, cache_win_v, ret_decay_fwd, ret_decay_bwd, na_rpb, win_sink)
        x = _matmul(o, w_out, l, resid=x, mod=mod, k_gate=2)
        i = l // 2
        if l % 2 == 0:
            h = _norm_mod(x, norm2_g, mod, l, 3, 4)
            x = _ffn_dense(h, x, mod, 5, ffn_w_in, ffn_w_out, i, D_FF, FFN_TM)
        else:
            router = jnp.pad(moe_router[i], ((0, 0), (0, LANES - N_EXPERTS)))
            h2, route, counts = _norm_mod(x, norm2_g, mod, l, 3, 4, router=router)
            x = _moe(x, h2, route, counts, moe_w_in[i], moe_w_out[i], mod, 5,
                     final_g=final_norm_g if l == DEPTH - 1 else None)
    if DEPTH % 2 == 0:
        y_prompt, y_sample = x
    else:
        y_prompt = _final_norm(x, final_norm_g, 0, T_CTX)
        y_sample = _final_norm(x, final_norm_g, T_CTX, T_LAT)
    return (y_prompt.reshape(BATCH, SEQ, D_MODEL), y_sample.reshape(DEC_BATCH, DEC_SEQ, D_MODEL), *states)
```

```python
import functools

import jax
import jax.numpy as jnp
from jax import lax
import numpy as np
from jax.experimental import pallas as pl
from jax.experimental.pallas import tpu as pltpu

D_MODEL = 2048
BATCH = 32
SEQ = 256
DEPTH = 2
DEC_BATCH = 4
DEC_SEQ = 1024
PAST_LEN = 256

GRID_W = 64
HEAD_DIM = 64
H_RET = 12
H_NA = 10
H_WIN = 10
KV_WIN = 2
G_WIN = H_WIN // KV_WIN
W_RET = H_RET * HEAD_DIM
W_NA = H_NA * HEAD_DIM
W_WIN = H_WIN * HEAD_DIM
W_MIX = W_RET + W_NA + W_WIN
W_IN = 4 * W_RET + 3 * W_NA + W_WIN + 2 * KV_WIN * HEAD_DIM
RET_CHUNK = 128
NA_ROWS = 8
NA_COLS = 16
WIN_HALF = 128
WIN_BLOCK = 128
ROPE_BASE = 10000.0
D_FF = 5632
N_EXPERTS = 8
TOP_K = 2
D_FF_EXPERT = 7168
EPS = 1e-6
NEG = -1e30
SCALE = HEAD_DIM ** -0.5

T_CTX = BATCH * SEQ
T_LAT = DEC_BATCH * DEC_SEQ
T_ALL = T_CTX + T_LAT
N_GROUPS = 8
LANES = 128

F32 = jnp.float32
BF16 = jnp.bfloat16

VMEM_LIMIT = 56 * 1024 * 1024


def _group_of_tile(i, tm):
    return jnp.maximum((i * tm - T_CTX) // DEC_SEQ + 1, 0)


def _cparams(sem):
    return pltpu.CompilerParams(dimension_semantics=sem, vmem_limit_bytes=VMEM_LIMIT)


def _adaln_kernel(c_ref, w_ref, b_ref, o_ref):
    c = c_ref[...]
    s = (c * jax.nn.sigmoid(c)).astype(BF16)
    o_ref[...] = jnp.dot(s, w_ref[...].astype(BF16), preferred_element_type=F32) + b_ref[...]


def _adaln(cond, ada_w, ada_b, l):
    tn = 1024
    n = 6 * D_MODEL
    out = pl.pallas_call(
        _adaln_kernel,
        out_shape=jax.ShapeDtypeStruct((N_GROUPS, n), F32),
        grid=(n // tn,),
        in_specs=[
            pl.BlockSpec((N_GROUPS, D_MODEL), lambda j: (0, 0)),
            pl.BlockSpec((None, D_MODEL, tn), lambda j: (l, 0, j)),
            pl.BlockSpec((None, 1, tn), lambda j: (l, 0, j)),
        ],
        out_specs=pl.BlockSpec((N_GROUPS, tn), lambda j: (0, j)),
        compiler_params=_cparams(("arbitrary",)),
        name="adaln",
    )(cond, ada_w, ada_b.reshape(DEPTH, 1, n))
    return out.reshape(N_GROUPS, 6, 1, D_MODEL)


N_SUB = D_MODEL // LANES


def _store_token_major(ref, val):
    rows = val.shape[0]
    for s in range(N_SUB):
        ref[pl.ds(s, rows, stride=N_SUB), :] = val[:, s * LANES:(s + 1) * LANES].astype(ref.dtype)


def _load_token_major(ref, rows, dtype=F32):
    return jnp.concatenate([ref[pl.ds(s, rows, stride=N_SUB), :].astype(dtype) for s in range(N_SUB)],
                           axis=1)


def _token_slab(ref, t):
    return ref.at[pl.ds(pl.multiple_of(t * N_SUB, N_SUB), N_SUB)]


def _norm_mod_body(x_ref, g_ref, sh_ref, sc_ref):
    x = x_ref[...]
    ms = jnp.mean(x * x, axis=-1, keepdims=True)
    y = x * lax.rsqrt(ms + EPS) * g_ref[...]
    return y * (1.0 + sc_ref[...]) + sh_ref[...]


def _norm_mod_kernel(x_ref, g_ref, sh_ref, sc_ref, h_ref):
    h_ref[...] = _norm_mod_body(x_ref, g_ref, sh_ref, sc_ref).astype(h_ref.dtype)


R_E0, R_E1, R_RANK0, R_RANK1, R_W0, R_W1 = range(6)


def _norm_mod_router_kernel(x_ref, g_ref, sh_ref, sc_ref, r_ref, h2_ref, route_ref, cnt_ref, run_ref):
    i = pl.program_id(0)

    @pl.when(i == 0)
    def _():
        run_ref[...] = jnp.zeros_like(run_ref)

    h = _norm_mod_body(x_ref, g_ref, sh_ref, sc_ref)
    _store_token_major(h2_ref, h)
    tm = h.shape[0]
    r = r_ref[...]
    h_hi = h.astype(BF16)
    r_hi = r.astype(BF16)
    h_lo = (h - h_hi.astype(F32)).astype(BF16)
    r_lo = (r - r_hi.astype(F32)).astype(BF16)
    lg = _dot(h_hi, r_hi) + (_dot(h_hi, r_lo) + _dot(h_lo, r_hi))
    lane = lax.broadcasted_iota(jnp.int32, (tm, LANES), 1)
    lane_f = lane.astype(F32)
    lg = jnp.where(lane < N_EXPERTS, lg, -jnp.inf)
    v0 = jnp.max(lg, axis=-1, keepdims=True)
    e0 = jnp.min(jnp.where(lg == v0, lane_f, float(LANES)), axis=-1, keepdims=True)
    lg1 = jnp.where(lane_f == e0, -jnp.inf, lg)
    v1 = jnp.max(lg1, axis=-1, keepdims=True)
    e1 = jnp.min(jnp.where(lg1 == v1, lane_f, float(LANES)), axis=-1, keepdims=True)
    ex = jnp.exp(v1 - v0)
    w0 = 1.0 / (1.0 + ex)
    w1 = ex / (1.0 + ex)
    oh0 = jnp.where(lane_f == e0, 1.0, 0.0)
    oh1 = jnp.where(lane_f == e1, 1.0, 0.0)
    oh = oh0 + oh1
    earlier = (lax.broadcasted_iota(jnp.int32, (tm, tm), 0)
               > lax.broadcasted_iota(jnp.int32, (tm, tm), 1))
    before = jnp.dot(jnp.where(earlier, 1.0, 0.0).astype(BF16), oh.astype(BF16),
                     preferred_element_type=F32) + run_ref[0:1, :]
    rank0 = jnp.sum(oh0 * before, axis=-1, keepdims=True)
    rank1 = jnp.sum(oh1 * before, axis=-1, keepdims=True)
    rec = jnp.zeros((tm, LANES), F32)
    for k, val in ((R_E0, e0), (R_E1, e1), (R_RANK0, rank0), (R_RANK1, rank1), (R_W0, w0), (R_W1, w1)):
        rec = jnp.where(lane == k, val, rec)
    route_ref[...] = rec
    run_ref[0:1, :] = run_ref[0:1, :] + jnp.sum(oh, axis=0, keepdims=True)
    cnt_ref[...] = run_ref[...]


def _norm_mod(x, g, mod, l, k_shift, k_scale, *, router=None):
    tm = 512
    g3 = g.reshape(DEPTH, 1, D_MODEL)
    in_specs = [
        pl.BlockSpec((tm, D_MODEL), lambda i: (i, 0)),
        pl.BlockSpec((None, 1, D_MODEL), lambda i: (l, 0, 0)),
        pl.BlockSpec((None, None, 1, D_MODEL), lambda i: (_group_of_tile(i, tm), k_shift, 0, 0)),
        pl.BlockSpec((None, None, 1, D_MODEL), lambda i: (_group_of_tile(i, tm), k_scale, 0, 0)),
    ]
    if router is not None:
        in_specs.append(pl.BlockSpec((D_MODEL, LANES), lambda i: (0, 0)))
        return pl.pallas_call(
            _norm_mod_router_kernel,
            out_shape=(jax.ShapeDtypeStruct((T_ALL * N_SUB, LANES), F32),
                       jax.ShapeDtypeStruct((T_ALL, LANES), F32),
                       jax.ShapeDtypeStruct((8, LANES), F32)),
            grid=(T_ALL // tm,),
            in_specs=in_specs,
            out_specs=(pl.BlockSpec((tm * N_SUB, LANES), lambda i: (i, 0)),
                       pl.BlockSpec((tm, LANES), lambda i: (i, 0)),
                       pl.BlockSpec((8, LANES), lambda i: (0, 0))),
            scratch_shapes=[pltpu.VMEM((8, LANES), F32)],
            compiler_params=_cparams(("arbitrary",)), name="norm_mod_router",
        )(x, g3, mod, mod, router)
    return pl.pallas_call(
        _norm_mod_kernel, out_shape=jax.ShapeDtypeStruct((T_ALL, D_MODEL), BF16), grid=(T_ALL // tm,),
        in_specs=in_specs, out_specs=pl.BlockSpec((tm, D_MODEL), lambda i: (i, 0)),
        compiler_params=_cparams(("parallel",)), name="norm_mod",
    )(x, g3, mod, mod)


def _mm_kernel(a_ref, w_ref, o_ref):
    o_ref[...] = jnp.dot(a_ref[...].astype(BF16), w_ref[...].astype(BF16),
                         preferred_element_type=F32)


def _mm_res_kernel(a_ref, w_ref, x_ref, gate_ref, o_ref):
    acc = jnp.dot(a_ref[...].astype(BF16), w_ref[...].astype(BF16),
                  preferred_element_type=F32)
    o_ref[...] = x_ref[...] + gate_ref[...] * acc


def _matmul(a, w, l, *, resid=None, mod=None, k_gate=None):
    tm, tn = (2048 if resid is None else DEC_SEQ), 512
    t, k = a.shape
    n = w.shape[-1]
    grid = (t // tm, pl.cdiv(n, tn))
    in_specs = [
        pl.BlockSpec((tm, k), lambda i, j: (i, 0)),
        pl.BlockSpec((None, k, tn), lambda i, j: (l, 0, j)),
    ]
    args = [a, w]
    kern = _mm_kernel
    if resid is not None:
        in_specs += [
            pl.BlockSpec((tm, tn), lambda i, j: (i, j)),
            pl.BlockSpec((None, None, 1, tn), lambda i, j: (_group_of_tile(i, tm), k_gate, 0, j)),
        ]
        args += [resid, mod]
        kern = _mm_res_kernel
    return pl.pallas_call(
        kern, out_shape=jax.ShapeDtypeStruct((t, n), F32), grid=grid,
        in_specs=in_specs, out_specs=pl.BlockSpec((tm, tn), lambda i, j: (i, j)),
        compiler_params=_cparams(("parallel", "arbitrary")), name="proj",
    )(*args)


FFN_TF = 256


def _ffn_step(h_ref, wa_ref, wg_ref, wo_ref, acc_ref, j, n_rows=None):
    @pl.when(j == 0)
    def _():
        acc_ref[...] = jnp.zeros_like(acc_ref)

    def rows(r0, n):
        h = h_ref[r0:r0 + n, :]
        a = jnp.dot(h, wa_ref[...].astype(BF16), preferred_element_type=F32)
        g = jnp.dot(h, wg_ref[...].astype(BF16), preferred_element_type=F32)
        act = (a * jax.nn.sigmoid(a) * g).astype(BF16)
        acc_ref[r0:r0 + n, :] += jnp.dot(act, wo_ref[...].astype(BF16), preferred_element_type=F32)

    tm = h_ref.shape[0]
    if n_rows is None:
        rows(0, tm)
    else:
        quarter = tm // 4
        for k in range(1, 5):
            @pl.when(jnp.logical_and(n_rows > (k - 1) * quarter, n_rows <= k * quarter))
            def _():
                rows(0, k * quarter)


def _ffn_dense_kernel(h_ref, wa_ref, wg_ref, wo_ref, x_ref, gate_ref, o_ref, *, n_f):
    j = pl.program_id(1)
    _ffn_step(h_ref, wa_ref, wg_ref, wo_ref, o_ref, j)

    @pl.when(j == n_f - 1)
    def _():
        o_ref[...] = x_ref[...] + gate_ref[...] * o_ref[...]


def _ffn_dense(h, x, mod, k_gate, w_in, w_out, e, d_ff, tm):
    tf = FFN_TF
    n_f = d_ff // tf
    return pl.pallas_call(
        functools.partial(_ffn_dense_kernel, n_f=n_f),
        out_shape=jax.ShapeDtypeStruct((h.shape[0], D_MODEL), F32),
        grid=(h.shape[0] // tm, n_f),
        in_specs=[
            pl.BlockSpec((tm, D_MODEL), lambda i, j: (i, 0)),
            pl.BlockSpec((None, D_MODEL, tf), lambda i, j: (e, 0, j)),
            pl.BlockSpec((None, D_MODEL, tf), lambda i, j: (e, 0, j + n_f)),
            pl.BlockSpec((None, tf, D_MODEL), lambda i, j: (e, j, 0)),
            pl.BlockSpec((tm, D_MODEL), lambda i, j: (i, 0), pipeline_mode=pl.Buffered(1)),
            pl.BlockSpec((None, None, 1, D_MODEL), lambda i, j: (_group_of_tile(i, tm), k_gate, 0, 0)),
        ],
        out_specs=pl.BlockSpec((tm, D_MODEL), lambda i, j: (i, 0)),
        compiler_params=_cparams(("parallel", "arbitrary")), name="ffn_dense",
    )(h, w_in, w_in, w_out, x, mod)


N_DMA_PRIORITIES = 2


def _ffn_expert_kernel(te_ref, nt_ref, nrows_ref, tok_ref, h2_ref, wa_ref, wg_ref, wo_ref, o_ref,
                       gbuf_ref, hb_ref, acc_ref, sem, *, n_f, tm):
    i = pl.program_id(0)
    j = pl.program_id(1)
    valid = i < nt_ref[0]

    def start_gather(tile):
        def body(p, c):
            for u in range(N_DMA_PRIORITIES):
                r = p * N_DMA_PRIORITIES + u
                pltpu.make_async_copy(_token_slab(h2_ref, tok_ref[tile * tm + r]),
                                      _token_slab(gbuf_ref, r), sem).start(priority=u)
            return c
        lax.fori_loop(0, tm // N_DMA_PRIORITIES, body, 0, unroll=4)

    @pl.when(jnp.logical_and(valid, j == 0))
    def _():
        @pl.when(i == 0)
        def _():
            start_gather(0)

        pltpu.make_async_copy(h2_ref.at[pl.ds(0, tm * N_SUB)], gbuf_ref, sem).wait()
        for s in range(N_SUB):
            hb_ref[:, s * LANES:(s + 1) * LANES] = gbuf_ref[pl.ds(s, tm, stride=N_SUB), :].astype(BF16)

        @pl.when(i + 1 < nt_ref[0])
        def _():
            start_gather(i + 1)

    @pl.when(valid)
    def _():
        _ffn_step(hb_ref, wa_ref, wg_ref, wo_ref, acc_ref, j, nrows_ref[i])

        @pl.when(j == n_f - 1)
        def _():
            _store_token_major(o_ref, acc_ref[...])

    @pl.when(jnp.logical_and(jnp.logical_not(valid), j == 0))
    def _():
        o_ref[...] = jnp.zeros_like(o_ref)


def _ffn_expert(h2, row_token, te, nt, tile_rows, w_in, w_out, d_ff, tm):
    tf = FFN_TF
    r = row_token.shape[0]
    n_f = d_ff // tf

    def jj(i, j, nt_ref):
        return jnp.where(i < nt_ref[0], j, n_f - 1)

    grid_spec = pltpu.PrefetchScalarGridSpec(
        num_scalar_prefetch=4,
        grid=(r // tm, n_f),
        in_specs=[
            pl.BlockSpec(memory_space=pl.ANY),
            pl.BlockSpec((None, D_MODEL, tf), lambda i, j, te_ref, nt_ref, *_: (te_ref[i], 0, jj(i, j, nt_ref))),
            pl.BlockSpec((None, D_MODEL, tf), lambda i, j, te_ref, nt_ref, *_: (te_ref[i], 0, jj(i, j, nt_ref) + n_f)),
            pl.BlockSpec((None, tf, D_MODEL), lambda i, j, te_ref, nt_ref, *_: (te_ref[i], jj(i, j, nt_ref), 0)),
        ],
        out_specs=pl.BlockSpec((tm * N_SUB, LANES), lambda i, j, *_: (i, 0)),
        scratch_shapes=[pltpu.VMEM((tm * N_SUB, LANES), F32), pltpu.VMEM((tm, D_MODEL), BF16),
                        pltpu.VMEM((tm, D_MODEL), F32), pltpu.SemaphoreType.DMA],
    )
    return pl.pallas_call(
        functools.partial(_ffn_expert_kernel, n_f=n_f, tm=tm),
        out_shape=jax.ShapeDtypeStruct((r * N_SUB, LANES), F32),
        grid_spec=grid_spec,
        compiler_params=_cparams(("arbitrary", "arbitrary")), name="ffn_expert",
    )(te, nt, tile_rows, row_token, h2, w_in, w_in, w_out)


COMBINE_TM = 512


def _combine_kernel(eid_ref, rank_ref, first_row_ref, x_ref, route_ref, gate_ref, ys2_ref, *refs,
                    final):
    if final:
        fg_ref, yc_ref, yl_ref, buf_ref, sem = refs
    else:
        o_ref, buf_ref, sem = refs
    i = pl.program_id(0)
    n = pl.num_programs(0)
    tm = COMBINE_TM

    def start_tile(tile, slot):
        def start(t, c):
            for k in range(TOP_K):
                a = (tile * tm + t) * TOP_K + k
                row = first_row_ref[eid_ref[a]] + rank_ref[a]
                pltpu.make_async_copy(_token_slab(ys2_ref, row), _token_slab(buf_ref.at[slot, k], t),
                                      sem.at[slot]).start(priority=k % N_DMA_PRIORITIES)
            return c
        lax.fori_loop(0, tm, start, 0, unroll=4)

    @pl.when(i == 0)
    def _():
        start_tile(0, 0)

    @pl.when(i + 1 < n)
    def _():
        start_tile(i + 1, (i + 1) % 2)

    slot = i % 2
    for k in range(TOP_K):
        pltpu.make_async_copy(ys2_ref.at[pl.ds(0, tm * N_SUB)], buf_ref.at[slot, k], sem.at[slot]).wait()

    rec = route_ref[...]
    w0 = rec[:, R_W0:R_W0 + 1]
    w1 = rec[:, R_W1:R_W1 + 1]
    y = (w0 * _load_token_major(buf_ref.at[slot, 0], tm) + w1 * _load_token_major(buf_ref.at[slot, 1], tm))
    val = x_ref[...] + gate_ref[...] * y
    if not final:
        o_ref[...] = val
        return
    val = val * lax.rsqrt(jnp.mean(val * val, axis=-1, keepdims=True) + EPS) * fg_ref[...]

    @pl.when(i < T_CTX // tm)
    def _():
        yc_ref[...] = val

    @pl.when(i >= T_CTX // tm)
    def _():
        yl_ref[...] = val


def _combine(x, ys2, route, eid, rank, first_row, mod, k_gate, final_g=None):
    tm = COMBINE_TM
    final = final_g is not None
    n_ctx = T_CTX // tm
    row_spec = pl.BlockSpec((tm, D_MODEL), lambda i, *_: (i, 0))
    if final:
        extra_in = [pl.BlockSpec((1, D_MODEL), lambda i, *_: (0, 0))]
        extra_args = [final_g.reshape(1, D_MODEL)]
        out_specs = (pl.BlockSpec((tm, D_MODEL), lambda i, *_: (jnp.minimum(i, n_ctx - 1), 0)),
                     pl.BlockSpec((tm, D_MODEL), lambda i, *_: (jnp.maximum(i - n_ctx, 0), 0)))
        out_shape = (jax.ShapeDtypeStruct((T_CTX, D_MODEL), F32), jax.ShapeDtypeStruct((T_LAT, D_MODEL), F32))
    else:
        extra_in, extra_args, out_specs = [], [], row_spec
        out_shape = jax.ShapeDtypeStruct((T_ALL, D_MODEL), F32)
    grid_spec = pltpu.PrefetchScalarGridSpec(
        num_scalar_prefetch=3,
        grid=(T_ALL // tm,),
        in_specs=[
            row_spec,
            pl.BlockSpec((tm, LANES), lambda i, *_: (i, 0)),
            pl.BlockSpec((None, None, 1, D_MODEL), lambda i, *_: (_group_of_tile(i, tm), k_gate, 0, 0)),
            pl.BlockSpec(memory_space=pl.ANY),
            *extra_in,
        ],
        out_specs=out_specs,
        scratch_shapes=[pltpu.VMEM((2, TOP_K, tm * N_SUB, LANES), F32), pltpu.SemaphoreType.DMA((2,))],
    )
    return pl.pallas_call(
        functools.partial(_combine_kernel, final=final),
        out_shape=out_shape,
        grid_spec=grid_spec,
        compiler_params=_cparams(("arbitrary",)), name="moe_combine",
    )(eid, rank, first_row, x, route, mod, ys2, *extra_args)


def _final_norm_kernel(x_ref, g_ref, o_ref):
    x = x_ref[...]
    ms = jnp.mean(x * x, axis=-1, keepdims=True)
    o_ref[...] = x * lax.rsqrt(ms + EPS) * g_ref[...]


def _final_norm(x, g, row0, rows):
    tm = 512
    return pl.pallas_call(
        _final_norm_kernel,
        out_shape=jax.ShapeDtypeStruct((rows, D_MODEL), F32), grid=(rows // tm,),
        in_specs=[pl.BlockSpec((tm, D_MODEL), lambda i: (row0 // tm + i, 0)),
                  pl.BlockSpec((1, D_MODEL), lambda i: (0, 0))],
        out_specs=pl.BlockSpec((tm, D_MODEL), lambda i: (i, 0)),
        compiler_params=_cparams(("parallel",)), name="final_norm",
    )(x, g.reshape(1, D_MODEL))


FFN_TM = 1024
MOE_TM = 1024
MOE_TILES = (TOP_K * T_ALL) // MOE_TM + N_EXPERTS


def _moe(x, h2, route, counts, w_in, w_out, mod, k_gate, final_g=None):
    tm = MOE_TM
    eid = route[:, R_E0:R_E1 + 1].astype(jnp.int32).reshape(-1)
    rank = route[:, R_RANK0:R_RANK1 + 1].astype(jnp.int32).reshape(-1)
    counts = counts[0, :N_EXPERTS].astype(jnp.int32)
    ptiles = (counts + tm - 1) // tm
    tile_end = jnp.cumsum(ptiles)
    first_row = (tile_end - ptiles) * tm
    n_tiles = tile_end[-1]
    tile_ids = jnp.arange(MOE_TILES, dtype=jnp.int32)
    te = jnp.sum((tile_ids[:, None] >= tile_end[None, :]).astype(jnp.int32), axis=1)
    te = jnp.minimum(te, N_EXPERTS - 1)
    te_last = jnp.sum(jnp.where(tile_ids == n_tiles - 1, te, 0))
    te = jnp.where(tile_ids < n_tiles, te, te_last)
    experts = jnp.arange(N_EXPERTS, dtype=jnp.int32)
    row_end = jnp.sum(jnp.where(te[:, None] == experts[None, :], (first_row + counts)[None, :], 0), axis=1)
    tile_rows = jnp.clip(row_end - tile_ids * tm, 1, tm)
    pos = jnp.sum(jnp.where(eid[:, None] == experts[None, :], first_row[None, :], 0), axis=1) + rank
    tok = jnp.arange(TOP_K * T_ALL, dtype=jnp.int32) // TOP_K
    row_token = jnp.zeros((MOE_TILES * tm,), jnp.int32).at[pos].set(tok)
    ys2 = _ffn_expert(h2, row_token, te, n_tiles.reshape(1), tile_rows, w_in, w_out, D_FF_EXPERT, tm)
    return _combine(x, ys2, route, eid, rank, first_row, mod, k_gate, final_g)


CB_RQ, CB_RK, CB_RV, CB_RG = 0, 6, 12, 18
CB_NQ, CB_NK, CB_NV = 24, 29, 34
CB_WQ, CB_WK, CB_WV = 39, 44, 45
OB_RET, OB_NA, OB_WIN = 0, 6, 11
N_RET_PAIRS = H_RET // 2
N_NA_PAIRS = H_NA // 2
N_WIN_PAIRS = H_WIN // 2
NA_WIN_KEYS = NA_ROWS * GRID_W
LAT_ROWS = DEC_SEQ // GRID_W
WIN_Q = 2 * WIN_BLOCK


def _lane_lo(shape):
    return lax.broadcasted_iota(jnp.int32, shape, len(shape) - 1) < HEAD_DIM


def _dot_nt(a, b):
    return lax.dot_general(a, b, (((1,), (1,)), ((), ())), preferred_element_type=F32)


def _dot(a, b):
    return jnp.dot(a, b, preferred_element_type=F32)


def _attn_pair(q, ks, vs, biases, sinks):
    lo = _lane_lo(q.shape)
    outs = []
    for half in (0, 1):
        qm = jnp.where(lo if half == 0 else jnp.logical_not(lo), q, 0.0).astype(BF16)
        ss = []
        for kb, bb in zip(ks, biases):
            s = _dot_nt(qm, kb)
            if bb is not None:
                s = s + bb[half]
            ss.append(s)
        m = jnp.max(ss[0], axis=-1, keepdims=True)
        for s in ss[1:]:
            m = jnp.maximum(m, jnp.max(s, axis=-1, keepdims=True))
        if sinks is not None:
            m = jnp.maximum(m, sinks[half])
        ps = [jnp.exp(s - m) for s in ss]
        den = jnp.sum(ps[0], axis=-1, keepdims=True)
        for p in ps[1:]:
            den = den + jnp.sum(p, axis=-1, keepdims=True)
        if sinks is not None:
            den = den + jnp.exp(sinks[half] - m)
        o = _dot(ps[0].astype(BF16), vs[0])
        for p, vb in zip(ps[1:], vs[1:]):
            o = o + _dot(p.astype(BF16), vb)
        outs.append(o / den)
    return jnp.where(lo, outs[0], outs[1])


def _block_diag(s0, s1):
    z = jnp.zeros_like(s0)
    return jnp.concatenate([jnp.concatenate([s0, z], axis=1), jnp.concatenate([z, s1], axis=1)], axis=0)


def _ret_pair(q, k, v, g, lgf, lgb, s0f, s0b, seq):
    c = RET_CHUNK
    n = seq // c
    lo1 = _lane_lo((1, LANES))
    lo = _lane_lo((c, LANES))
    lgf_v = jnp.where(lo1, lgf[0], lgf[1])
    lgb_v = jnp.where(lo1, lgb[0], lgb[1])
    pos = lax.broadcasted_iota(jnp.int32, (c, LANES), 0).astype(F32)
    qw_f = jnp.exp(lgf_v * (pos + 1.0))
    kw_f = jnp.exp(lgf_v * (c - 1.0 - pos))
    qw_b = jnp.exp(lgb_v * (c - pos))
    kw_b = jnp.exp(lgb_v * pos)
    gc_f = jnp.exp(lgf_v * float(c))
    gc_b = jnp.exp(lgb_v * float(c))
    diff = (lax.broadcasted_iota(jnp.int32, (c, c), 0)
            - lax.broadcasted_iota(jnp.int32, (c, c), 1)).astype(F32)
    decay = [jnp.where(diff >= 0, jnp.exp(lgf[h] * jnp.maximum(diff, 0.0)), 0.0)
             + jnp.where(diff <= 0, jnp.exp(lgb[h] * jnp.maximum(-diff, 0.0)), 0.0) for h in (0, 1)]
    bd = ((lax.broadcasted_iota(jnp.int32, (LANES, LANES), 0) < HEAD_DIM)
          == (lax.broadcasted_iota(jnp.int32, (LANES, LANES), 1) < HEAD_DIM))
    zero = jnp.zeros((LANES, LANES), F32)
    sf = zero if s0f is None else s0f
    sb = zero if s0b is None else s0b
    qs = [q[i * c:(i + 1) * c] for i in range(n)]
    ks = [k[i * c:(i + 1) * c] * SCALE for i in range(n)]
    vs = [v[i * c:(i + 1) * c].astype(BF16) for i in range(n)]
    outs = []
    for i in range(n):
        kb = ks[i].astype(BF16)
        p0 = (_dot_nt(jnp.where(lo, qs[i], 0.0).astype(BF16), kb) * decay[0]).astype(BF16)
        p1 = (_dot_nt(jnp.where(lo, 0.0, qs[i]).astype(BF16), kb) * decay[1]).astype(BF16)
        o = jnp.where(lo, _dot(p0, vs[i]), _dot(p1, vs[i]))
        o = o + _dot((qs[i] * qw_f).astype(BF16), sf.astype(BF16))
        sf = gc_f * sf + jnp.where(bd, _dot((ks[i] * kw_f).T.astype(BF16), vs[i]), 0.0)
        outs.append(o)
    for i in reversed(range(n)):
        outs[i] = outs[i] + _dot((qs[i] * qw_b).astype(BF16), sb.astype(BF16))
        sb = gc_b * sb + jnp.where(bd, _dot((ks[i] * kw_b).T.astype(BF16), vs[i]), 0.0)
    o = jnp.concatenate(outs, axis=0) if n > 1 else outs[0]
    lo_s = _lane_lo((seq, LANES))
    inv_d = 1.0 / HEAD_DIM
    mu = jnp.where(lo_s, jnp.sum(jnp.where(lo_s, o, 0.0), axis=-1, keepdims=True),
                   jnp.sum(jnp.where(lo_s, 0.0, o), axis=-1, keepdims=True)) * inv_d
    d = o - mu
    d2 = d * d
    var = jnp.where(lo_s, jnp.sum(jnp.where(lo_s, d2, 0.0), axis=-1, keepdims=True),
                    jnp.sum(jnp.where(lo_s, 0.0, d2), axis=-1, keepdims=True)) * inv_d
    y = d * lax.rsqrt(var + EPS) * (g * jax.nn.sigmoid(g))
    return y, sf, sb


N_STATES = 6


def _ctx_mixer_kernel(lgf_ref, lgb_ref, sink_ref, p_ref, o_in_ref, *refs, n_prev):
    prev_refs = refs[:N_STATES] if n_prev else ()
    o_ref, sf_ref, sb_ref, nk_ref, nv_ref, wk_ref, wv_ref = refs[len(prev_refs):]
    for prev, cur in zip(prev_refs, (sf_ref, sb_ref, nk_ref, nv_ref, wk_ref, wv_ref)):
        cur[:n_prev] = prev[...]
    l = n_prev

    def col(blk):
        return p_ref[:, blk * LANES:(blk + 1) * LANES]

    for hp in range(N_RET_PAIRS):
        y, sf, sb = _ret_pair(col(CB_RQ + hp), col(CB_RK + hp), col(CB_RV + hp), col(CB_RG + hp),
                              (lgf_ref[2 * hp], lgf_ref[2 * hp + 1]),
                              (lgb_ref[2 * hp], lgb_ref[2 * hp + 1]), None, None, SEQ)
        o_ref[:, (OB_RET + hp) * LANES:(OB_RET + hp + 1) * LANES] = y.astype(o_ref.dtype)
        sf_ref[l, 2 * hp] = sf[:HEAD_DIM, :HEAD_DIM]
        sf_ref[l, 2 * hp + 1] = sf[HEAD_DIM:, HEAD_DIM:]
        sb_ref[l, 2 * hp] = sb[:HEAD_DIM, :HEAD_DIM]
        sb_ref[l, 2 * hp + 1] = sb[HEAD_DIM:, HEAD_DIM:]

    for hp in range(N_NA_PAIRS):
        k = col(CB_NK + hp)
        v = col(CB_NV + hp)
        o = _attn_pair(col(CB_NQ + hp) * SCALE, [k.astype(BF16)], [v.astype(BF16)], [None], None)
        o_ref[:, (OB_NA + hp) * LANES:(OB_NA + hp + 1) * LANES] = o.astype(o_ref.dtype)
        nk_ref[l, 2 * hp] = k[:, :HEAD_DIM]
        nk_ref[l, 2 * hp + 1] = k[:, HEAD_DIM:]
        nv_ref[l, 2 * hp] = v[:, :HEAD_DIM]
        nv_ref[l, 2 * hp + 1] = v[:, HEAD_DIM:]

    k = col(CB_WK)
    v = col(CB_WV)
    for kv in range(KV_WIN):
        wk_ref[l, kv] = k[:, kv * HEAD_DIM:(kv + 1) * HEAD_DIM]
        wv_ref[l, kv] = v[:, kv * HEAD_DIM:(kv + 1) * HEAD_DIM]
    lo = _lane_lo(k.shape)
    k_sw = pltpu.roll(k, HEAD_DIM, 1)
    v_sw = pltpu.roll(v, HEAD_DIM, 1)
    for hp in range(N_WIN_PAIRS):
        kv_lo = (2 * hp) // G_WIN
        kv_hi = (2 * hp + 1) // G_WIN
        kk = jnp.where(lo, k if kv_lo == 0 else k_sw, k if kv_hi == 1 else k_sw).astype(BF16)
        vv = jnp.where(lo, v if kv_lo == 0 else v_sw, v if kv_hi == 1 else v_sw).astype(BF16)
        o = _attn_pair(col(CB_WQ + hp) * SCALE, [kk], [vv], [None],
                       (sink_ref[2 * hp], sink_ref[2 * hp + 1]))
        o_ref[:, (OB_WIN + hp) * LANES:(OB_WIN + hp + 1) * LANES] = o.astype(o_ref.dtype)


def _smem_spec():
    return pl.BlockSpec(memory_space=pltpu.SMEM)


def _ctx_mixer(proj, lgf, lgb, sink, n_ctx, prev_states):
    t = proj.shape[0]
    n_prev = prev_states[0].shape[1] if prev_states else 0
    o_init = jnp.zeros((t, W_MIX), BF16)
    dims = [(H_RET, HEAD_DIM, HEAD_DIM)] * 2 + [(H_NA, SEQ, HEAD_DIM)] * 2 + [(KV_WIN, SEQ, HEAD_DIM)] * 2
    st = lambda n, d: pl.BlockSpec((None, n) + d, lambda i: (i, 0, 0, 0, 0))
    return pl.pallas_call(
        functools.partial(_ctx_mixer_kernel, n_prev=n_prev),
        out_shape=(jax.ShapeDtypeStruct((t, W_MIX), BF16),
                   *[jax.ShapeDtypeStruct((n_ctx, n_prev + 1) + d, F32) for d in dims]),
        grid=(n_ctx,),
        in_specs=[_smem_spec(), _smem_spec(), _smem_spec(),
                  pl.BlockSpec((SEQ, W_IN), lambda i: (i, 0)), pl.BlockSpec(memory_space=pl.ANY),
                  *[st(n_prev, d) for d in dims[:len(prev_states)]]],
        out_specs=(pl.BlockSpec((SEQ, W_MIX), lambda i: (i, 0)), *[st(n_prev + 1, d) for d in dims]),
        input_output_aliases={4: 0},
        compiler_params=_cparams(("parallel",)), name="ctx_mixer",
    )(lgf, lgb, sink, proj, o_init, *prev_states)


def _ret_lat_kernel(lgf_ref, lgb_ref, q_ref, k_ref, v_ref, g_ref, s0f_ref, s0b_ref, o_in_ref, o_ref):
    hp = pl.program_id(1)
    y, _, _ = _ret_pair(q_ref[...], k_ref[...], v_ref[...], g_ref[...],
                        (lgf_ref[2 * hp], lgf_ref[2 * hp + 1]), (lgb_ref[2 * hp], lgb_ref[2 * hp + 1]),
                        _block_diag(s0f_ref[0], s0f_ref[1]), _block_diag(s0b_ref[0], s0b_ref[1]), DEC_SEQ)
    o_ref[...] = y.astype(o_ref.dtype)


def _ret_latent(proj, o, lgf, lgb, s0f, s0b, l, n_lat):
    rb0 = (proj.shape[0] - n_lat * DEC_SEQ) // DEC_SEQ
    cb = lambda c0: pl.BlockSpec((DEC_SEQ, LANES), lambda b, hp: (rb0 + b, c0 + hp))
    st = pl.BlockSpec((None, None, 2, HEAD_DIM, HEAD_DIM), lambda b, hp: (b, l, hp, 0, 0))
    return pl.pallas_call(
        _ret_lat_kernel, out_shape=jax.ShapeDtypeStruct(o.shape, o.dtype),
        grid=(n_lat, N_RET_PAIRS),
        in_specs=[_smem_spec(), _smem_spec(), cb(CB_RQ), cb(CB_RK), cb(CB_RV), cb(CB_RG), st, st,
                  pl.BlockSpec(memory_space=pl.ANY)],
        out_specs=pl.BlockSpec((DEC_SEQ, LANES), lambda b, hp: (rb0 + b, OB_RET + hp)),
        input_output_aliases={8: 0},
        compiler_params=_cparams(("parallel", "parallel")), name="ret_latent",
    )(lgf, lgb, proj, proj, proj, proj, s0f, s0b, o)


NA_QROWS = 4
NA_KROWS = NA_ROWS + NA_QROWS
N_ROW_OFFS = 2 * NA_ROWS - 1


def _na_lat_kernel(q_ref, k_ref, v_ref, kc_ref, vc_ref, bias_ref, o_in_ref, o_ref):
    kc = jnp.concatenate([kc_ref[0], kc_ref[1]], axis=1).astype(BF16)
    vc = jnp.concatenate([vc_ref[0], vc_ref[1]], axis=1).astype(BF16)
    n_keys = NA_KROWS * GRID_W
    key_row = lax.broadcasted_iota(jnp.int32, (GRID_W, n_keys), 1) // GRID_W
    for blk in range(LAT_ROWS // NA_QROWS):
        w0 = min(max(blk * NA_QROWS - NA_ROWS // 2, 0), LAT_ROWS - NA_KROWS)
        q = q_ref[blk * NA_QROWS * GRID_W:(blk + 1) * NA_QROWS * GRID_W, :] * SCALE
        kw = k_ref[w0 * GRID_W:(w0 + NA_KROWS) * GRID_W, :].astype(BF16)
        vw = v_ref[w0 * GRID_W:(w0 + NA_KROWS) * GRID_W, :].astype(BF16)
        bias = [[], []]
        for qr in range(NA_QROWS):
            r = blk * NA_QROWS + qr
            r0 = min(max(r - NA_ROWS // 2, 0), LAT_ROWS - NA_ROWS)
            in_window = jnp.logical_and(key_row >= r0 - w0, key_row < r0 - w0 + NA_ROWS)
            for h in (0, 1):
                tiles = [bias_ref[h, min(max(w0 + 2 * m - r + NA_ROWS, 0), N_ROW_OFFS)]
                         for m in range(NA_KROWS // 2)]
                bias[h].append(jnp.where(in_window, jnp.concatenate(tiles, axis=1), NEG))
        bias = [jnp.concatenate(b, axis=0) for b in bias]
        o = _attn_pair(q, [kw, kc], [vw, vc], [bias, None], None)
        o_ref[blk * NA_QROWS * GRID_W:(blk + 1) * NA_QROWS * GRID_W, :] = o.astype(o_ref.dtype)


def _na_bias_table(rpb):
    cols = np.arange(GRID_W)
    c0 = np.clip(cols - NA_COLS // 2, 0, GRID_W - NA_COLS)
    col_ok = (cols[None, :] >= c0[:, None]) & (cols[None, :] < c0[:, None] + NA_COLS)
    dc = np.clip(cols[None, :] - cols[:, None], -(NA_COLS - 1), NA_COLS - 1) + (NA_COLS - 1)
    pick = jnp.asarray(dc.reshape(-1)[None, :] == np.arange(2 * NA_COLS - 1)[:, None], F32)
    by_col = jnp.einsum('hoc,cn->hon', rpb.astype(F32), pick, precision=lax.Precision.HIGHEST)
    by_col = jnp.where(col_ok[None, None], by_col.reshape(H_NA, N_ROW_OFFS, GRID_W, GRID_W), NEG)
    masked = jnp.full((H_NA, 1, GRID_W, GRID_W), NEG, F32)
    by_col = jnp.concatenate([masked, by_col, masked], axis=1)
    return jnp.concatenate([by_col[:, :-1], by_col[:, 1:]], axis=-1)


def _na_latent(proj, o, kc, vc, bias, l, n_lat):
    rb0 = (proj.shape[0] - n_lat * DEC_SEQ) // DEC_SEQ
    cb = lambda c0: pl.BlockSpec((DEC_SEQ, LANES), lambda b, hp: (rb0 + b, c0 + hp))
    cache = pl.BlockSpec((None, None, 2, PAST_LEN, HEAD_DIM), lambda b, hp: (b, l, hp, 0, 0))
    return pl.pallas_call(
        _na_lat_kernel, out_shape=jax.ShapeDtypeStruct(o.shape, o.dtype),
        grid=(n_lat, N_NA_PAIRS),
        in_specs=[cb(CB_NQ), cb(CB_NK), cb(CB_NV), cache, cache,
                  pl.BlockSpec((2, N_ROW_OFFS + 1, GRID_W, LANES), lambda b, hp: (hp, 0, 0, 0)),
                  pl.BlockSpec(memory_space=pl.ANY)],
        out_specs=pl.BlockSpec((DEC_SEQ, LANES), lambda b, hp: (rb0 + b, OB_NA + hp)),
        input_output_aliases={6: 0},
        compiler_params=_cparams(("parallel", "parallel")), name="na_latent",
    )(proj, proj, proj, kc, vc, bias, o)


def _rope_tables():
    t = jnp.arange(DEC_SEQ)
    d = np.arange(LANES) % HEAD_DIM
    quarter = HEAD_DIM // 4
    inv = ROPE_BASE ** (-jnp.arange(quarter, dtype=F32) / quarter)
    pos = jnp.where(jnp.asarray(d < HEAD_DIM // 2)[None, :], (t // GRID_W)[:, None], (t % GRID_W)[:, None])
    ang = pos.astype(F32) * inv[d % quarter][None, :]
    sign = jnp.asarray(np.where((d & quarter) == 0, -1.0, 1.0), F32)
    return jnp.cos(ang), jnp.sin(ang) * sign[None, :]


def _win_lat_kernel(sink_ref, q_ref, k_ref, v_ref, kc_ref, vc_ref, cos_ref, sin_ref, o_in_ref, o_ref,
                    qs_ref, kp_ref, vp_ref):
    hp = pl.program_id(1)
    quarter = HEAD_DIM // 4
    lane = lax.broadcasted_iota(jnp.int32, (DEC_SEQ, LANES), 1)
    first = (lane & quarter) == 0
    lo = lane < HEAD_DIM
    cos = cos_ref[...]
    sin = sin_ref[...]

    def rope(x):
        sw = jnp.where(first, pltpu.roll(x, LANES - quarter, 1), pltpu.roll(x, quarter, 1))
        return x * cos + sw * sin

    qs_ref[...] = rope(q_ref[...]) * SCALE
    k = rope(k_ref[...])
    v = v_ref[...]
    lo_orig = (2 * hp) // G_WIN == 0
    hi_orig = (2 * hp + 1) // G_WIN == 1
    orig = jnp.where(lo, lo_orig.astype(jnp.int32), hi_orig.astype(jnp.int32)) == 1
    zeros = jnp.zeros((WIN_BLOCK, LANES), BF16)
    kp_ref[:WIN_BLOCK] = zeros
    kp_ref[WIN_BLOCK + DEC_SEQ:] = zeros
    vp_ref[:WIN_BLOCK] = zeros
    vp_ref[WIN_BLOCK + DEC_SEQ:] = zeros
    kp_ref[WIN_BLOCK:WIN_BLOCK + DEC_SEQ] = jnp.where(orig, k, pltpu.roll(k, HEAD_DIM, 1)).astype(BF16)
    vp_ref[WIN_BLOCK:WIN_BLOCK + DEC_SEQ] = jnp.where(orig, v, pltpu.roll(v, HEAD_DIM, 1)).astype(BF16)
    kc = jnp.concatenate([jnp.where(lo_orig, kc_ref[0], kc_ref[1]),
                          jnp.where(hi_orig, kc_ref[1], kc_ref[0])], axis=1).astype(BF16)
    vc = jnp.concatenate([jnp.where(lo_orig, vc_ref[0], vc_ref[1]),
                          jnp.where(hi_orig, vc_ref[1], vc_ref[0])], axis=1).astype(BF16)
    sinks = (sink_ref[2 * hp], sink_ref[2 * hp + 1])
    n_band = WIN_Q + 2 * WIN_BLOCK
    qi = lax.broadcasted_iota(jnp.int32, (WIN_Q, n_band), 0)
    kj = lax.broadcasted_iota(jnp.int32, (WIN_Q, n_band), 1)
    near = jnp.abs(qi + WIN_BLOCK - kj) <= WIN_HALF
    for n in range(DEC_SEQ // WIN_Q):
        q0 = n * WIN_Q
        kpos = q0 - WIN_BLOCK + kj
        ok = jnp.logical_and(near, jnp.logical_and(kpos >= 0, kpos < DEC_SEQ))
        band = jnp.where(ok, 0.0, NEG)
        o = _attn_pair(qs_ref[q0:q0 + WIN_Q, :],
                       [kp_ref[q0:q0 + n_band, :], kc], [vp_ref[q0:q0 + n_band, :], vc],
                       [(band, band), None], sinks)
        o_ref[q0:q0 + WIN_Q, :] = o.astype(o_ref.dtype)


def _win_latent(proj, o, kc, vc, sink, cos, sin, l, n_lat):
    rb0 = (proj.shape[0] - n_lat * DEC_SEQ) // DEC_SEQ
    cache = pl.BlockSpec((None, None, KV_WIN, PAST_LEN, HEAD_DIM), lambda b, hp: (b, l, 0, 0, 0))
    tbl = pl.BlockSpec((DEC_SEQ, LANES), lambda b, hp: (0, 0))
    return pl.pallas_call(
        _win_lat_kernel, out_shape=jax.ShapeDtypeStruct(o.shape, o.dtype),
        grid=(n_lat, N_WIN_PAIRS),
        in_specs=[_smem_spec(),
                  pl.BlockSpec((DEC_SEQ, LANES), lambda b, hp: (rb0 + b, CB_WQ + hp)),
                  pl.BlockSpec((DEC_SEQ, LANES), lambda b, hp: (rb0 + b, CB_WK)),
                  pl.BlockSpec((DEC_SEQ, LANES), lambda b, hp: (rb0 + b, CB_WV)),
                  cache, cache, tbl, tbl, pl.BlockSpec(memory_space=pl.ANY)],
        out_specs=pl.BlockSpec((DEC_SEQ, LANES), lambda b, hp: (rb0 + b, OB_WIN + hp)),
        scratch_shapes=[pltpu.VMEM((DEC_SEQ, LANES), F32),
                        pltpu.VMEM((DEC_SEQ + 2 * WIN_BLOCK, LANES), BF16),
                        pltpu.VMEM((DEC_SEQ + 2 * WIN_BLOCK, LANES), BF16)],
        input_output_aliases={8: 0},
        compiler_params=_cparams(("parallel", "parallel")), name="win_latent",
    )(sink, proj, proj, proj, kc, vc, cos, sin, o)


def _mixers(proj, l, n_ctx, n_lat, prev_states, state_ret_fwd, state_ret_bwd, cache_na_k, cache_na_v,
            cache_win_k, cache_win_v, ret_decay_fwd, ret_decay_bwd, na_rpb, win_sink):
    lgf = jax.nn.log_sigmoid(ret_decay_fwd[l].astype(F32))
    lgb = jax.nn.log_sigmoid(ret_decay_bwd[l].astype(F32))
    sink = win_sink[l].astype(F32)
    o, *states = _ctx_mixer(proj, lgf, lgb, sink, n_ctx, prev_states)
    o = _ret_latent(proj, o, lgf, lgb, state_ret_fwd, state_ret_bwd, l, n_lat)
    o = _na_latent(proj, o, cache_na_k, cache_na_v, _na_bias_table(na_rpb[l]), l, n_lat)
    cos, sin = _rope_tables()
    o = _win_latent(proj, o, cache_win_k, cache_win_v, sink, cos, sin, l, n_lat)
    return o, tuple(states)


def kernel(x_prompt, x_sample, c, state_ret_fwd, state_ret_bwd, cache_na_k, cache_na_v, cache_win_k, cache_win_v, c_ctx, norm1_g, norm2_g, ada_w, ada_b, w_in, w_out, ret_decay_fwd, ret_decay_bwd, na_rpb, win_sink, ffn_w_in, ffn_w_out, moe_router, moe_w_in, moe_w_out, final_norm_g):
    x = jnp.concatenate([x_prompt.reshape(T_CTX, D_MODEL), x_sample.reshape(T_LAT, D_MODEL)], axis=0)
    cond = jnp.concatenate([c_ctx[None, :], c, jnp.zeros((N_GROUPS - 1 - DEC_BATCH, D_MODEL), F32)], axis=0)
    states = ()
    for l in range(DEPTH):
        mod = _adaln(cond, ada_w, ada_b, l)
        h = _norm_mod(x, norm1_g, mod, l, 0, 1)
        proj = _matmul(h, w_in, l)
        o, states = _mixers(proj, l, BATCH, DEC_BATCH, states, state_ret_fwd, state_ret_bwd, cache_na_k, cache_na_v,
                         cache_win_k, cache_win_v, ret_decay_fwd, ret_decay_bwd, na_rpb, win_sink)
        x = _matmul(o, w_out, l, resid=x, mod=mod, k_gate=2)
        i = l // 2
        if l % 2 == 0:
            h = _norm_mod(x, norm2_g, mod, l, 3, 4)
            x = _ffn_dense(h, x, mod, 5, ffn_w_in, ffn_w_out, i, D_FF, FFN_TM)
        else:
            router = jnp.pad(moe_router[i], ((0, 0), (0, LANES - N_EXPERTS)))
            h2, route, counts = _norm_mod(x, norm2_g, mod, l, 3, 4, router=router)
            x = _moe(x, h2, route, counts, moe_w_in[i], moe_w_out[i], mod, 5,
                     final_g=final_norm_g if l == DEPTH - 1 else None)
    if DEPTH % 2 == 0:
        y_prompt, y_sample = x
    else:
        y_prompt = _final_norm(x, final_norm_g, 0, T_CTX)
        y_sample = _final_norm(x, final_norm_g, T_CTX, T_LAT)
    return (y_prompt.reshape(BATCH, SEQ, D_MODEL), y_sample.reshape(DEC_BATCH, DEC_SEQ, D_MODEL), *states)
```

```python
import functools

import jax
import jax.numpy as jnp
from jax import lax
import numpy as np
from jax.experimental import pallas as pl
from jax.experimental.pallas import tpu as pltpu

D_MODEL = 2048
BATCH = 32
SEQ = 256
DEPTH = 2
DEC_BATCH = 4
DEC_SEQ = 1024
PAST_LEN = 256

GRID_W = 64
HEAD_DIM = 64
H_RET = 12
H_NA = 10
H_WIN = 10
KV_WIN = 2
G_WIN = H_WIN // KV_WIN
W_RET = H_RET * HEAD_DIM
W_NA = H_NA * HEAD_DIM
W_WIN = H_WIN * HEAD_DIM
W_MIX = W_RET + W_NA + W_WIN
W_IN = 4 * W_RET + 3 * W_NA + W_WIN + 2 * KV_WIN * HEAD_DIM
RET_CHUNK = 128
NA_ROWS = 8
NA_COLS = 16
WIN_HALF = 128
WIN_BLOCK = 128
ROPE_BASE = 10000.0
D_FF = 5632
N_EXPERTS = 8
TOP_K = 2
D_FF_EXPERT = 7168
EPS = 1e-6
NEG = -1e30
SCALE = HEAD_DIM ** -0.5

T_CTX = BATCH * SEQ
T_LAT = DEC_BATCH * DEC_SEQ
T_ALL = T_CTX + T_LAT
N_GROUPS = 8
LANES = 128

F32 = jnp.float32
BF16 = jnp.bfloat16

VMEM_LIMIT = 56 * 1024 * 1024


def _group_of_tile(i, tm):
    return jnp.maximum((i * tm - T_CTX) // DEC_SEQ + 1, 0)


def _cparams(sem):
    return pltpu.CompilerParams(dimension_semantics=sem, vmem_limit_bytes=VMEM_LIMIT)


def _adaln_kernel(c_ref, w_ref, b_ref, o_ref):
    c = c_ref[...]
    s = (c * jax.nn.sigmoid(c)).astype(BF16)
    o_ref[...] = jnp.dot(s, w_ref[...].astype(BF16), preferred_element_type=F32) + b_ref[...]


def _adaln(cond, ada_w, ada_b, l):
    tn = 1024
    n = 6 * D_MODEL
    out = pl.pallas_call(
        _adaln_kernel,
        out_shape=jax.ShapeDtypeStruct((N_GROUPS, n), F32),
        grid=(n // tn,),
        in_specs=[
            pl.BlockSpec((N_GROUPS, D_MODEL), lambda j: (0, 0)),
            pl.BlockSpec((None, D_MODEL, tn), lambda j: (l, 0, j)),
            pl.BlockSpec((None, 1, tn), lambda j: (l, 0, j)),
        ],
        out_specs=pl.BlockSpec((N_GROUPS, tn), lambda j: (0, j)),
        compiler_params=_cparams(("arbitrary",)),
        name="adaln",
    )(cond, ada_w, ada_b.reshape(DEPTH, 1, n))
    return out.reshape(N_GROUPS, 6, 1, D_MODEL)


N_SUB = D_MODEL // LANES


def _store_token_major(ref, val):
    rows = val.shape[0]
    for s in range(N_SUB):
        ref[pl.ds(s, rows, stride=N_SUB), :] = val[:, s * LANES:(s + 1) * LANES].astype(ref.dtype)


def _load_token_major(ref, rows, dtype=F32):
    return jnp.concatenate([ref[pl.ds(s, rows, stride=N_SUB), :].astype(dtype) for s in range(N_SUB)],
                           axis=1)


def _token_slab(ref, t):
    return ref.at[pl.ds(pl.multiple_of(t * N_SUB, N_SUB), N_SUB)]


def _norm_mod_body(x_ref, g_ref, sh_ref, sc_ref):
    x = x_ref[...]
    ms = jnp.mean(x * x, axis=-1, keepdims=True)
    y = x * lax.rsqrt(ms + EPS) * g_ref[...]
    return y * (1.0 + sc_ref[...]) + sh_ref[...]


def _norm_mod_kernel(x_ref, g_ref, sh_ref, sc_ref, h_ref):
    h_ref[...] = _norm_mod_body(x_ref, g_ref, sh_ref, sc_ref).astype(h_ref.dtype)


R_E0, R_E1, R_RANK0, R_RANK1, R_W0, R_W1 = range(6)


def _norm_mod_router_kernel(x_ref, g_ref, sh_ref, sc_ref, r_ref, h2_ref, route_ref, cnt_ref, run_ref):
    i = pl.program_id(0)

    @pl.when(i == 0)
    def _():
        run_ref[...] = jnp.zeros_like(run_ref)

    h = _norm_mod_body(x_ref, g_ref, sh_ref, sc_ref)
    _store_token_major(h2_ref, h)
    tm = h.shape[0]
    r = r_ref[...]
    h_hi = h.astype(BF16)
    r_hi = r.astype(BF16)
    h_lo = (h - h_hi.astype(F32)).astype(BF16)
    r_lo = (r - r_hi.astype(F32)).astype(BF16)
    lg = _dot(h_hi, r_hi) + (_dot(h_hi, r_lo) + _dot(h_lo, r_hi))
    lane = lax.broadcasted_iota(jnp.int32, (tm, LANES), 1)
    lane_f = lane.astype(F32)
    lg = jnp.where(lane < N_EXPERTS, lg, -jnp.inf)
    v0 = jnp.max(lg, axis=-1, keepdims=True)
    e0 = jnp.min(jnp.where(lg == v0, lane_f, float(LANES)), axis=-1, keepdims=True)
    lg1 = jnp.where(lane_f == e0, -jnp.inf, lg)
    v1 = jnp.max(lg1, axis=-1, keepdims=True)
    e1 = jnp.min(jnp.where(lg1 == v1, lane_f, float(LANES)), axis=-1, keepdims=True)
    ex = jnp.exp(v1 - v0)
    w0 = 1.0 / (1.0 + ex)
    w1 = ex / (1.0 + ex)
    oh0 = jnp.where(lane_f == e0, 1.0, 0.0)
    oh1 = jnp.where(lane_f == e1, 1.0, 0.0)
    oh = oh0 + oh1
    earlier = (lax.broadcasted_iota(jnp.int32, (tm, tm), 0)
               > lax.broadcasted_iota(jnp.int32, (tm, tm), 1))
    before = jnp.dot(jnp.where(earlier, 1.0, 0.0).astype(BF16), oh.astype(BF16),
                     preferred_element_type=F32) + run_ref[0:1, :]
    rank0 = jnp.sum(oh0 * before, axis=-1, keepdims=True)
    rank1 = jnp.sum(oh1 * before, axis=-1, keepdims=True)
    rec = jnp.zeros((tm, LANES), F32)
    for k, val in ((R_E0, e0), (R_E1, e1), (R_RANK0, rank0), (R_RANK1, rank1), (R_W0, w0), (R_W1, w1)):
        rec = jnp.where(lane == k, val, rec)
    route_ref[...] = rec
    run_ref[0:1, :] = run_ref[0:1, :] + jnp.sum(oh, axis=0, keepdims=True)
    cnt_ref[...] = run_ref[...]


def _norm_mod(x, g, mod, l, k_shift, k_scale, *, router=None):
    tm = 512
    g3 = g.reshape(DEPTH, 1, D_MODEL)
    in_specs = [
        pl.BlockSpec((tm, D_MODEL), lambda i: (i, 0)),
        pl.BlockSpec((None, 1, D_MODEL), lambda i: (l, 0, 0)),
        pl.BlockSpec((None, None, 1, D_MODEL), lambda i: (_group_of_tile(i, tm), k_shift, 0, 0)),
        pl.BlockSpec((None, None, 1, D_MODEL), lambda i: (_group_of_tile(i, tm), k_scale, 0, 0)),
    ]
    if router is not None:
        in_specs.append(pl.BlockSpec((D_MODEL, LANES), lambda i: (0, 0)))
        return pl.pallas_call(
            _norm_mod_router_kernel,
            out_shape=(jax.ShapeDtypeStruct((T_ALL * N_SUB, LANES), F32),
                       jax.ShapeDtypeStruct((T_ALL, LANES), F32),
                       jax.ShapeDtypeStruct((8, LANES), F32)),
            grid=(T_ALL // tm,),
            in_specs=in_specs,
            out_specs=(pl.BlockSpec((tm * N_SUB, LANES), lambda i: (i, 0)),
                       pl.BlockSpec((tm, LANES), lambda i: (i, 0)),
                       pl.BlockSpec((8, LANES), lambda i: (0, 0))),
            scratch_shapes=[pltpu.VMEM((8, LANES), F32)],
            compiler_params=_cparams(("arbitrary",)), name="norm_mod_router",
        )(x, g3, mod, mod, router)
    return pl.pallas_call(
        _norm_mod_kernel, out_shape=jax.ShapeDtypeStruct((T_ALL, D_MODEL), BF16), grid=(T_ALL // tm,),
        in_specs=in_specs, out_specs=pl.BlockSpec((tm, D_MODEL), lambda i: (i, 0)),
        compiler_params=_cparams(("parallel",)), name="norm_mod",
    )(x, g3, mod, mod)


def _mm_kernel(a_ref, w_ref, o_ref):
    o_ref[...] = jnp.dot(a_ref[...].astype(BF16), w_ref[...].astype(BF16),
                         preferred_element_type=F32)


def _mm_res_kernel(a_ref, w_ref, x_ref, gate_ref, o_ref):
    acc = jnp.dot(a_ref[...].astype(BF16), w_ref[...].astype(BF16),
                  preferred_element_type=F32)
    o_ref[...] = x_ref[...] + gate_ref[...] * acc


def _matmul(a, w, l, *, resid=None, mod=None, k_gate=None):
    tm, tn = (2048 if resid is None else DEC_SEQ), 512
    t, k = a.shape
    n = w.shape[-1]
    grid = (t // tm, pl.cdiv(n, tn))
    in_specs = [
        pl.BlockSpec((tm, k), lambda i, j: (i, 0)),
        pl.BlockSpec((None, k, tn), lambda i, j: (l, 0, j)),
    ]
    args = [a, w]
    kern = _mm_kernel
    if resid is not None:
        in_specs += [
            pl.BlockSpec((tm, tn), lambda i, j: (i, j)),
            pl.BlockSpec((None, None, 1, tn), lambda i, j: (_group_of_tile(i, tm), k_gate, 0, j)),
        ]
        args += [resid, mod]
        kern = _mm_res_kernel
    return pl.pallas_call(
        kern, out_shape=jax.ShapeDtypeStruct((t, n), F32), grid=grid,
        in_specs=in_specs, out_specs=pl.BlockSpec((tm, tn), lambda i, j: (i, j)),
        compiler_params=_cparams(("parallel", "arbitrary")), name="proj",
    )(*args)


FFN_TF = 256


def _ffn_step(h_ref, wa_ref, wg_ref, wo_ref, acc_ref, j, n_rows=None):
    @pl.when(j == 0)
    def _():
        acc_ref[...] = jnp.zeros_like(acc_ref)

    def rows(r0, n):
        h = h_ref[r0:r0 + n, :]
        a = jnp.dot(h, wa_ref[...].astype(BF16), preferred_element_type=F32)
        g = jnp.dot(h, wg_ref[...].astype(BF16), preferred_element_type=F32)
        act = (a * jax.nn.sigmoid(a) * g).astype(BF16)
        acc_ref[r0:r0 + n, :] += jnp.dot(act, wo_ref[...].astype(BF16), preferred_element_type=F32)

    tm = h_ref.shape[0]
    if n_rows is None:
        rows(0, tm)
    else:
        quarter = tm // 4
        for k in range(1, 5):
            @pl.when(jnp.logical_and(n_rows > (k - 1) * quarter, n_rows <= k * quarter))
            def _():
                rows(0, k * quarter)


def _ffn_dense_kernel(h_ref, wa_ref, wg_ref, wo_ref, x_hbm_ref, gate_ref, o_ref, x_ref, sem, *, n_f):
    i = pl.program_id(0)
    j = pl.program_id(1)
    tm = x_ref.shape[0]
    x_copy = pltpu.make_async_copy(x_hbm_ref.at[pl.ds(pl.multiple_of(i * tm, tm), tm)], x_ref, sem)

    @pl.when(j == 0)
    def _():
        x_copy.start()

    _ffn_step(h_ref, wa_ref, wg_ref, wo_ref, o_ref, j)

    @pl.when(j == n_f - 1)
    def _():
        x_copy.wait()
        o_ref[...] = x_ref[...] + gate_ref[...] * o_ref[...]


def _ffn_dense(h, x, mod, k_gate, w_in, w_out, e, d_ff, tm):
    tf = FFN_TF
    n_f = d_ff // tf
    return pl.pallas_call(
        functools.partial(_ffn_dense_kernel, n_f=n_f),
        out_shape=jax.ShapeDtypeStruct((h.shape[0], D_MODEL), F32),
        grid=(h.shape[0] // tm, n_f),
        in_specs=[
            pl.BlockSpec((tm, D_MODEL), lambda i, j: (i, 0)),
            pl.BlockSpec((None, D_MODEL, tf), lambda i, j: (e, 0, j)),
            pl.BlockSpec((None, D_MODEL, tf), lambda i, j: (e, 0, j + n_f)),
            pl.BlockSpec((None, tf, D_MODEL), lambda i, j: (e, j, 0)),
            pl.BlockSpec(memory_space=pl.ANY),
            pl.BlockSpec((None, None, 1, D_MODEL), lambda i, j: (_group_of_tile(i, tm), k_gate, 0, 0)),
        ],
        out_specs=pl.BlockSpec((tm, D_MODEL), lambda i, j: (i, 0)),
        scratch_shapes=[pltpu.VMEM((tm, D_MODEL), F32), pltpu.SemaphoreType.DMA],
        compiler_params=_cparams(("arbitrary", "arbitrary")), name="ffn_dense",
    )(h, w_in, w_in, w_out, x, mod)


N_DMA_PRIORITIES = 2


def _ffn_expert_kernel(te_ref, nt_ref, nrows_ref, tok_ref, h2_ref, wa_ref, wg_ref, wo_ref, o_ref,
                       gbuf_ref, hb_ref, acc_ref, sem, *, n_f, tm):
    i = pl.program_id(0)
    j = pl.program_id(1)
    valid = i < nt_ref[0]

    def start_gather(tile):
        def body(p, c):
            for u in range(N_DMA_PRIORITIES):
                r = p * N_DMA_PRIORITIES + u
                pltpu.make_async_copy(_token_slab(h2_ref, tok_ref[tile * tm + r]),
                                      _token_slab(gbuf_ref, r), sem).start(priority=u)
            return c
        lax.fori_loop(0, tm // N_DMA_PRIORITIES, body, 0, unroll=4)

    @pl.when(jnp.logical_and(valid, j == 0))
    def _():
        @pl.when(i == 0)
        def _():
            start_gather(0)

        pltpu.make_async_copy(h2_ref.at[pl.ds(0, tm * N_SUB)], gbuf_ref, sem).wait()
        for s in range(N_SUB):
            hb_ref[:, s * LANES:(s + 1) * LANES] = gbuf_ref[pl.ds(s, tm, stride=N_SUB), :].astype(BF16)

        @pl.when(i + 1 < nt_ref[0])
        def _():
            start_gather(i + 1)

    @pl.when(valid)
    def _():
        _ffn_step(hb_ref, wa_ref, wg_ref, wo_ref, acc_ref, j, nrows_ref[i])

        @pl.when(j == n_f - 1)
        def _():
            _store_token_major(o_ref, acc_ref[...])

    @pl.when(jnp.logical_and(jnp.logical_not(valid), j == 0))
    def _():
        o_ref[...] = jnp.zeros_like(o_ref)


def _ffn_expert(h2, row_token, te, nt, tile_rows, w_in, w_out, d_ff, tm):
    tf = FFN_TF
    r = row_token.shape[0]
    n_f = d_ff // tf

    def jj(i, j, nt_ref):
        return jnp.where(i < nt_ref[0], j, n_f - 1)

    grid_spec = pltpu.PrefetchScalarGridSpec(
        num_scalar_prefetch=4,
        grid=(r // tm, n_f),
        in_specs=[
            pl.BlockSpec(memory_space=pl.ANY),
            pl.BlockSpec((None, D_MODEL, tf), lambda i, j, te_ref, nt_ref, *_: (te_ref[i], 0, jj(i, j, nt_ref))),
            pl.BlockSpec((None, D_MODEL, tf), lambda i, j, te_ref, nt_ref, *_: (te_ref[i], 0, jj(i, j, nt_ref) + n_f)),
            pl.BlockSpec((None, tf, D_MODEL), lambda i, j, te_ref, nt_ref, *_: (te_ref[i], jj(i, j, nt_ref), 0)),
        ],
        out_specs=pl.BlockSpec((tm * N_SUB, LANES), lambda i, j, *_: (i, 0)),
        scratch_shapes=[pltpu.VMEM((tm * N_SUB, LANES), F32), pltpu.VMEM((tm, D_MODEL), BF16),
                        pltpu.VMEM((tm, D_MODEL), F32), pltpu.SemaphoreType.DMA],
    )
    return pl.pallas_call(
        functools.partial(_ffn_expert_kernel, n_f=n_f, tm=tm),
        out_shape=jax.ShapeDtypeStruct((r * N_SUB, LANES), F32),
        grid_spec=grid_spec,
        compiler_params=_cparams(("arbitrary", "arbitrary")), name="ffn_expert",
    )(te, nt, tile_rows, row_token, h2, w_in, w_in, w_out)


COMBINE_TM = 256


def _combine_kernel(eid_ref, rank_ref, first_row_ref, x_ref, route_ref, gate_ref, ys2_ref, *refs,
                    final):
    if final:
        fg_ref, yc_ref, yl_ref, buf_ref, sem = refs
    else:
        o_ref, buf_ref, sem = refs
    i = pl.program_id(0)
    n = pl.num_programs(0)
    tm = COMBINE_TM

    def start_tile(tile, slot):
        def start(t, c):
            for k in range(TOP_K):
                a = (tile * tm + t) * TOP_K + k
                row = first_row_ref[eid_ref[a]] + rank_ref[a]
                pltpu.make_async_copy(_token_slab(ys2_ref, row), _token_slab(buf_ref.at[slot, k], t),
                                      sem.at[slot]).start(priority=k % N_DMA_PRIORITIES)
            return c
        lax.fori_loop(0, tm, start, 0, unroll=4)

    @pl.when(i == 0)
    def _():
        start_tile(0, 0)

    @pl.when(i + 1 < n)
    def _():
        start_tile(i + 1, (i + 1) % 2)

    slot = i % 2
    for k in range(TOP_K):
        pltpu.make_async_copy(ys2_ref.at[pl.ds(0, tm * N_SUB)], buf_ref.at[slot, k], sem.at[slot]).wait()

    rec = route_ref[...]
    w0 = rec[:, R_W0:R_W0 + 1]
    w1 = rec[:, R_W1:R_W1 + 1]
    y = (w0 * _load_token_major(buf_ref.at[slot, 0], tm) + w1 * _load_token_major(buf_ref.at[slot, 1], tm))
    val = x_ref[...] + gate_ref[...] * y
    if not final:
        o_ref[...] = val
        return
    val = val * lax.rsqrt(jnp.mean(val * val, axis=-1, keepdims=True) + EPS) * fg_ref[...]

    @pl.when(i < T_CTX // tm)
    def _():
        yc_ref[...] = val

    @pl.when(i >= T_CTX // tm)
    def _():
        yl_ref[...] = val


def _combine(x, ys2, route, eid, rank, first_row, mod, k_gate, final_g=None):
    tm = COMBINE_TM
    final = final_g is not None
    n_ctx = T_CTX // tm
    row_spec = pl.BlockSpec((tm, D_MODEL), lambda i, *_: (i, 0))
    if final:
        extra_in = [pl.BlockSpec((1, D_MODEL), lambda i, *_: (0, 0))]
        extra_args = [final_g.reshape(1, D_MODEL)]
        out_specs = (pl.BlockSpec((tm, D_MODEL), lambda i, *_: (jnp.minimum(i, n_ctx - 1), 0)),
                     pl.BlockSpec((tm, D_MODEL), lambda i, *_: (jnp.maximum(i - n_ctx, 0), 0)))
        out_shape = (jax.ShapeDtypeStruct((T_CTX, D_MODEL), F32), jax.ShapeDtypeStruct((T_LAT, D_MODEL), F32))
    else:
        extra_in, extra_args, out_specs = [], [], row_spec
        out_shape = jax.ShapeDtypeStruct((T_ALL, D_MODEL), F32)
    grid_spec = pltpu.PrefetchScalarGridSpec(
        num_scalar_prefetch=3,
        grid=(T_ALL // tm,),
        in_specs=[
            row_spec,
            pl.BlockSpec((tm, LANES), lambda i, *_: (i, 0)),
            pl.BlockSpec((None, None, 1, D_MODEL), lambda i, *_: (_group_of_tile(i, tm), k_gate, 0, 0)),
            pl.BlockSpec(memory_space=pl.ANY),
            *extra_in,
        ],
        out_specs=out_specs,
        scratch_shapes=[pltpu.VMEM((2, TOP_K, tm * N_SUB, LANES), F32), pltpu.SemaphoreType.DMA((2,))],
    )
    return pl.pallas_call(
        functools.partial(_combine_kernel, final=final),
        out_shape=out_shape,
        grid_spec=grid_spec,
        compiler_params=_cparams(("arbitrary",)), name="moe_combine",
    )(eid, rank, first_row, x, route, mod, ys2, *extra_args)


def _final_norm_kernel(x_ref, g_ref, o_ref):
    x = x_ref[...]
    ms = jnp.mean(x * x, axis=-1, keepdims=True)
    o_ref[...] = x * lax.rsqrt(ms + EPS) * g_ref[...]


def _final_norm(x, g, row0, rows):
    tm = 512
    return pl.pallas_call(
        _final_norm_kernel,
        out_shape=jax.ShapeDtypeStruct((rows, D_MODEL), F32), grid=(rows // tm,),
        in_specs=[pl.BlockSpec((tm, D_MODEL), lambda i: (row0 // tm + i, 0)),
                  pl.BlockSpec((1, D_MODEL), lambda i: (0, 0))],
        out_specs=pl.BlockSpec((tm, D_MODEL), lambda i: (i, 0)),
        compiler_params=_cparams(("parallel",)), name="final_norm",
    )(x, g.reshape(1, D_MODEL))


FFN_TM = 1024
MOE_TM = 1024
MOE_TILES = (TOP_K * T_ALL) // MOE_TM + N_EXPERTS


def _moe(x, h2, route, counts, w_in, w_out, mod, k_gate, final_g=None):
    tm = MOE_TM
    eid = route[:, R_E0:R_E1 + 1].astype(jnp.int32).reshape(-1)
    rank = route[:, R_RANK0:R_RANK1 + 1].astype(jnp.int32).reshape(-1)
    counts = counts[0, :N_EXPERTS].astype(jnp.int32)
    ptiles = (counts + tm - 1) // tm
    tile_end = jnp.cumsum(ptiles)
    first_row = (tile_end - ptiles) * tm
    n_tiles = tile_end[-1]
    tile_ids = jnp.arange(MOE_TILES, dtype=jnp.int32)
    te = jnp.sum((tile_ids[:, None] >= tile_end[None, :]).astype(jnp.int32), axis=1)
    te = jnp.minimum(te, N_EXPERTS - 1)
    te_last = jnp.sum(jnp.where(tile_ids == n_tiles - 1, te, 0))
    te = jnp.where(tile_ids < n_tiles, te, te_last)
    experts = jnp.arange(N_EXPERTS, dtype=jnp.int32)
    row_end = jnp.sum(jnp.where(te[:, None] == experts[None, :], (first_row + counts)[None, :], 0), axis=1)
    tile_rows = jnp.clip(row_end - tile_ids * tm, 1, tm)
    pos = jnp.sum(jnp.where(eid[:, None] == experts[None, :], first_row[None, :], 0), axis=1) + rank
    tok = jnp.arange(TOP_K * T_ALL, dtype=jnp.int32) // TOP_K
    row_token = jnp.zeros((MOE_TILES * tm,), jnp.int32).at[pos].set(tok)
    ys2 = _ffn_expert(h2, row_token, te, n_tiles.reshape(1), tile_rows, w_in, w_out, D_FF_EXPERT, tm)
    return _combine(x, ys2, route, eid, rank, first_row, mod, k_gate, final_g)


CB_RQ, CB_RK, CB_RV, CB_RG = 0, 6, 12, 18
CB_NQ, CB_NK, CB_NV = 24, 29, 34
CB_WQ, CB_WK, CB_WV = 39, 44, 45
OB_RET, OB_NA, OB_WIN = 0, 6, 11
N_RET_PAIRS = H_RET // 2
N_NA_PAIRS = H_NA // 2
N_WIN_PAIRS = H_WIN // 2
NA_WIN_KEYS = NA_ROWS * GRID_W
LAT_ROWS = DEC_SEQ // GRID_W
WIN_Q = 2 * WIN_BLOCK


def _lane_lo(shape):
    return lax.broadcasted_iota(jnp.int32, shape, len(shape) - 1) < HEAD_DIM


def _dot_nt(a, b):
    return lax.dot_general(a, b, (((1,), (1,)), ((), ())), preferred_element_type=F32)


def _dot(a, b):
    return jnp.dot(a, b, preferred_element_type=F32)


def _attn_pair(q, ks, vs, biases, sinks):
    lo = _lane_lo(q.shape)
    outs = []
    for half in (0, 1):
        qm = jnp.where(lo if half == 0 else jnp.logical_not(lo), q, 0.0).astype(BF16)
        ss = []
        for kb, bb in zip(ks, biases):
            s = _dot_nt(qm, kb)
            if bb is not None:
                s = s + bb[half]
            ss.append(s)
        m = jnp.max(ss[0], axis=-1, keepdims=True)
        for s in ss[1:]:
            m = jnp.maximum(m, jnp.max(s, axis=-1, keepdims=True))
        if sinks is not None:
            m = jnp.maximum(m, sinks[half])
        ps = [jnp.exp(s - m) for s in ss]
        den = jnp.sum(ps[0], axis=-1, keepdims=True)
        for p in ps[1:]:
            den = den + jnp.sum(p, axis=-1, keepdims=True)
        if sinks is not None:
            den = den + jnp.exp(sinks[half] - m)
        o = _dot(ps[0].astype(BF16), vs[0])
        for p, vb in zip(ps[1:], vs[1:]):
            o = o + _dot(p.astype(BF16), vb)
        outs.append(o / den)
    return jnp.where(lo, outs[0], outs[1])


def _block_diag(s0, s1):
    z = jnp.zeros_like(s0)
    return jnp.concatenate([jnp.concatenate([s0, z], axis=1), jnp.concatenate([z, s1], axis=1)], axis=0)


def _ret_pair(q, k, v, g, lgf, lgb, s0f, s0b, seq):
    c = RET_CHUNK
    n = seq // c
    lo1 = _lane_lo((1, LANES))
    lo = _lane_lo((c, LANES))
    lgf_v = jnp.where(lo1, lgf[0], lgf[1])
    lgb_v = jnp.where(lo1, lgb[0], lgb[1])
    pos = lax.broadcasted_iota(jnp.int32, (c, LANES), 0).astype(F32)
    qw_f = jnp.exp(lgf_v * (pos + 1.0))
    kw_f = jnp.exp(lgf_v * (c - 1.0 - pos))
    qw_b = jnp.exp(lgb_v * (c - pos))
    kw_b = jnp.exp(lgb_v * pos)
    gc_f = jnp.exp(lgf_v * float(c))
    gc_b = jnp.exp(lgb_v * float(c))
    diff = (lax.broadcasted_iota(jnp.int32, (c, c), 0)
            - lax.broadcasted_iota(jnp.int32, (c, c), 1)).astype(F32)
    decay = [jnp.where(diff >= 0, jnp.exp(lgf[h] * jnp.maximum(diff, 0.0)), 0.0)
             + jnp.where(diff <= 0, jnp.exp(lgb[h] * jnp.maximum(-diff, 0.0)), 0.0) for h in (0, 1)]
    bd = ((lax.broadcasted_iota(jnp.int32, (LANES, LANES), 0) < HEAD_DIM)
          == (lax.broadcasted_iota(jnp.int32, (LANES, LANES), 1) < HEAD_DIM))
    zero = jnp.zeros((LANES, LANES), F32)
    sf = zero if s0f is None else s0f
    sb = zero if s0b is None else s0b
    qs = [q[i * c:(i + 1) * c] for i in range(n)]
    ks = [k[i * c:(i + 1) * c] * SCALE for i in range(n)]
    vs = [v[i * c:(i + 1) * c].astype(BF16) for i in range(n)]
    outs = []
    for i in range(n):
        kb = ks[i].astype(BF16)
        p0 = (_dot_nt(jnp.where(lo, qs[i], 0.0).astype(BF16), kb) * decay[0]).astype(BF16)
        p1 = (_dot_nt(jnp.where(lo, 0.0, qs[i]).astype(BF16), kb) * decay[1]).astype(BF16)
        o = jnp.where(lo, _dot(p0, vs[i]), _dot(p1, vs[i]))
        o = o + _dot((qs[i] * qw_f).astype(BF16), sf.astype(BF16))
        sf = gc_f * sf + jnp.where(bd, _dot((ks[i] * kw_f).T.astype(BF16), vs[i]), 0.0)
        outs.append(o)
    for i in reversed(range(n)):
        outs[i] = outs[i] + _dot((qs[i] * qw_b).astype(BF16), sb.astype(BF16))
        sb = gc_b * sb + jnp.where(bd, _dot((ks[i] * kw_b).T.astype(BF16), vs[i]), 0.0)
    o = jnp.concatenate(outs, axis=0) if n > 1 else outs[0]
    lo_s = _lane_lo((seq, LANES))
    inv_d = 1.0 / HEAD_DIM
    mu = jnp.where(lo_s, jnp.sum(jnp.where(lo_s, o, 0.0), axis=-1, keepdims=True),
                   jnp.sum(jnp.where(lo_s, 0.0, o), axis=-1, keepdims=True)) * inv_d
    d = o - mu
    d2 = d * d
    var = jnp.where(lo_s, jnp.sum(jnp.where(lo_s, d2, 0.0), axis=-1, keepdims=True),
                    jnp.sum(jnp.where(lo_s, 0.0, d2), axis=-1, keepdims=True)) * inv_d
    y = d * lax.rsqrt(var + EPS) * (g * jax.nn.sigmoid(g))
    return y, sf, sb


N_STATES = 6


def _ctx_mixer_kernel(lgf_ref, lgb_ref, sink_ref, p_ref, o_in_ref, *refs, n_prev):
    prev_refs = refs[:N_STATES] if n_prev else ()
    o_ref, sf_ref, sb_ref, nk_ref, nv_ref, wk_ref, wv_ref = refs[len(prev_refs):]
    for prev, cur in zip(prev_refs, (sf_ref, sb_ref, nk_ref, nv_ref, wk_ref, wv_ref)):
        cur[:n_prev] = prev[...]
    l = n_prev

    def col(blk):
        return p_ref[:, blk * LANES:(blk + 1) * LANES]

    for hp in range(N_RET_PAIRS):
        y, sf, sb = _ret_pair(col(CB_RQ + hp), col(CB_RK + hp), col(CB_RV + hp), col(CB_RG + hp),
                              (lgf_ref[2 * hp], lgf_ref[2 * hp + 1]),
                              (lgb_ref[2 * hp], lgb_ref[2 * hp + 1]), None, None, SEQ)
        o_ref[:, (OB_RET + hp) * LANES:(OB_RET + hp + 1) * LANES] = y.astype(o_ref.dtype)
        sf_ref[l, 2 * hp] = sf[:HEAD_DIM, :HEAD_DIM]
        sf_ref[l, 2 * hp + 1] = sf[HEAD_DIM:, HEAD_DIM:]
        sb_ref[l, 2 * hp] = sb[:HEAD_DIM, :HEAD_DIM]
        sb_ref[l, 2 * hp + 1] = sb[HEAD_DIM:, HEAD_DIM:]

    for hp in range(N_NA_PAIRS):
        k = col(CB_NK + hp)
        v = col(CB_NV + hp)
        o = _attn_pair(col(CB_NQ + hp) * SCALE, [k.astype(BF16)], [v.astype(BF16)], [None], None)
        o_ref[:, (OB_NA + hp) * LANES:(OB_NA + hp + 1) * LANES] = o.astype(o_ref.dtype)
        nk_ref[l, 2 * hp] = k[:, :HEAD_DIM]
        nk_ref[l, 2 * hp + 1] = k[:, HEAD_DIM:]
        nv_ref[l, 2 * hp] = v[:, :HEAD_DIM]
        nv_ref[l, 2 * hp + 1] = v[:, HEAD_DIM:]

    k = col(CB_WK)
    v = col(CB_WV)
    for kv in range(KV_WIN):
        wk_ref[l, kv] = k[:, kv * HEAD_DIM:(kv + 1) * HEAD_DIM]
        wv_ref[l, kv] = v[:, kv * HEAD_DIM:(kv + 1) * HEAD_DIM]
    lo = _lane_lo(k.shape)
    k_sw = pltpu.roll(k, HEAD_DIM, 1)
    v_sw = pltpu.roll(v, HEAD_DIM, 1)
    for hp in range(N_WIN_PAIRS):
        kv_lo = (2 * hp) // G_WIN
        kv_hi = (2 * hp + 1) // G_WIN
        kk = jnp.where(lo, k if kv_lo == 0 else k_sw, k if kv_hi == 1 else k_sw).astype(BF16)
        vv = jnp.where(lo, v if kv_lo == 0 else v_sw, v if kv_hi == 1 else v_sw).astype(BF16)
        o = _attn_pair(col(CB_WQ + hp) * SCALE, [kk], [vv], [None],
                       (sink_ref[2 * hp], sink_ref[2 * hp + 1]))
        o_ref[:, (OB_WIN + hp) * LANES:(OB_WIN + hp + 1) * LANES] = o.astype(o_ref.dtype)


def _smem_spec():
    return pl.BlockSpec(memory_space=pltpu.SMEM)


def _ctx_mixer(proj, lgf, lgb, sink, n_ctx, prev_states):
    t = proj.shape[0]
    n_prev = prev_states[0].shape[1] if prev_states else 0
    o_init = jnp.zeros((t, W_MIX), BF16)
    dims = [(H_RET, HEAD_DIM, HEAD_DIM)] * 2 + [(H_NA, SEQ, HEAD_DIM)] * 2 + [(KV_WIN, SEQ, HEAD_DIM)] * 2
    st = lambda n, d: pl.BlockSpec((None, n) + d, lambda i: (i, 0, 0, 0, 0))
    return pl.pallas_call(
        functools.partial(_ctx_mixer_kernel, n_prev=n_prev),
        out_shape=(jax.ShapeDtypeStruct((t, W_MIX), BF16),
                   *[jax.ShapeDtypeStruct((n_ctx, n_prev + 1) + d, F32) for d in dims]),
        grid=(n_ctx,),
        in_specs=[_smem_spec(), _smem_spec(), _smem_spec(),
                  pl.BlockSpec((SEQ, W_IN), lambda i: (i, 0)), pl.BlockSpec(memory_space=pl.ANY),
                  *[st(n_prev, d) for d in dims[:len(prev_states)]]],
        out_specs=(pl.BlockSpec((SEQ, W_MIX), lambda i: (i, 0)), *[st(n_prev + 1, d) for d in dims]),
        input_output_aliases={4: 0},
        compiler_params=_cparams(("parallel",)), name="ctx_mixer",
    )(lgf, lgb, sink, proj, o_init, *prev_states)


def _ret_lat_kernel(lgf_ref, lgb_ref, q_ref, k_ref, v_ref, g_ref, s0f_ref, s0b_ref, o_in_ref, o_ref):
    hp = pl.program_id(1)
    y, _, _ = _ret_pair(q_ref[...], k_ref[...], v_ref[...], g_ref[...],
                        (lgf_ref[2 * hp], lgf_ref[2 * hp + 1]), (lgb_ref[2 * hp], lgb_ref[2 * hp + 1]),
                        _block_diag(s0f_ref[0], s0f_ref[1]), _block_diag(s0b_ref[0], s0b_ref[1]), DEC_SEQ)
    o_ref[...] = y.astype(o_ref.dtype)


def _ret_latent(proj, o, lgf, lgb, s0f, s0b, l, n_lat):
    rb0 = (proj.shape[0] - n_lat * DEC_SEQ) // DEC_SEQ
    cb = lambda c0: pl.BlockSpec((DEC_SEQ, LANES), lambda b, hp: (rb0 + b, c0 + hp))
    st = pl.BlockSpec((None, None, 2, HEAD_DIM, HEAD_DIM), lambda b, hp: (b, l, hp, 0, 0))
    return pl.pallas_call(
        _ret_lat_kernel, out_shape=jax.ShapeDtypeStruct(o.shape, o.dtype),
        grid=(n_lat, N_RET_PAIRS),
        in_specs=[_smem_spec(), _smem_spec(), cb(CB_RQ), cb(CB_RK), cb(CB_RV), cb(CB_RG), st, st,
                  pl.BlockSpec(memory_space=pl.ANY)],
        out_specs=pl.BlockSpec((DEC_SEQ, LANES), lambda b, hp: (rb0 + b, OB_RET + hp)),
        input_output_aliases={8: 0},
        compiler_params=_cparams(("parallel", "parallel")), name="ret_latent",
    )(lgf, lgb, proj, proj, proj, proj, s0f, s0b, o)


NA_QROWS = 4
NA_KROWS = NA_ROWS + NA_QROWS
N_ROW_OFFS = 2 * NA_ROWS - 1


def _na_lat_kernel(q_ref, k_ref, v_ref, kc_ref, vc_ref, bias_ref, o_in_ref, o_ref):
    kc = jnp.concatenate([kc_ref[0], kc_ref[1]], axis=1).astype(BF16)
    vc = jnp.concatenate([vc_ref[0], vc_ref[1]], axis=1).astype(BF16)
    n_keys = NA_KROWS * GRID_W
    key_row = lax.broadcasted_iota(jnp.int32, (GRID_W, n_keys), 1) // GRID_W
    for blk in range(LAT_ROWS // NA_QROWS):
        w0 = min(max(blk * NA_QROWS - NA_ROWS // 2, 0), LAT_ROWS - NA_KROWS)
        q = q_ref[blk * NA_QROWS * GRID_W:(blk + 1) * NA_QROWS * GRID_W, :] * SCALE
        kw = k_ref[w0 * GRID_W:(w0 + NA_KROWS) * GRID_W, :].astype(BF16)
        vw = v_ref[w0 * GRID_W:(w0 + NA_KROWS) * GRID_W, :].astype(BF16)
        bias = [[], []]
        for qr in range(NA_QROWS):
            r = blk * NA_QROWS + qr
            r0 = min(max(r - NA_ROWS // 2, 0), LAT_ROWS - NA_ROWS)
            in_window = jnp.logical_and(key_row >= r0 - w0, key_row < r0 - w0 + NA_ROWS)
            for h in (0, 1):
                tiles = [bias_ref[h, min(max(w0 + 2 * m - r + NA_ROWS, 0), N_ROW_OFFS)]
                         for m in range(NA_KROWS // 2)]
                bias[h].append(jnp.where(in_window, jnp.concatenate(tiles, axis=1), NEG))
        bias = [jnp.concatenate(b, axis=0) for b in bias]
        o = _attn_pair(q, [kw, kc], [vw, vc], [bias, None], None)
        o_ref[blk * NA_QROWS * GRID_W:(blk + 1) * NA_QROWS * GRID_W, :] = o.astype(o_ref.dtype)


def _na_bias_table(rpb):
    cols = np.arange(GRID_W)
    c0 = np.clip(cols - NA_COLS // 2, 0, GRID_W - NA_COLS)
    col_ok = (cols[None, :] >= c0[:, None]) & (cols[None, :] < c0[:, None] + NA_COLS)
    dc = np.clip(cols[None, :] - cols[:, None], -(NA_COLS - 1), NA_COLS - 1) + (NA_COLS - 1)
    pick = jnp.asarray(dc.reshape(-1)[None, :] == np.arange(2 * NA_COLS - 1)[:, None], F32)
    by_col = jnp.einsum('hoc,cn->hon', rpb.astype(F32), pick, precision=lax.Precision.HIGHEST)
    by_col = jnp.where(col_ok[None, None], by_col.reshape(H_NA, N_ROW_OFFS, GRID_W, GRID_W), NEG)
    masked = jnp.full((H_NA, 1, GRID_W, GRID_W), NEG, F32)
    by_col = jnp.concatenate([masked, by_col, masked], axis=1)
    return jnp.concatenate([by_col[:, :-1], by_col[:, 1:]], axis=-1)


def _na_latent(proj, o, kc, vc, bias, l, n_lat):
    rb0 = (proj.shape[0] - n_lat * DEC_SEQ) // DEC_SEQ
    cb = lambda c0: pl.BlockSpec((DEC_SEQ, LANES), lambda b, hp: (rb0 + b, c0 + hp))
    cache = pl.BlockSpec((None, None, 2, PAST_LEN, HEAD_DIM), lambda b, hp: (b, l, hp, 0, 0))
    return pl.pallas_call(
        _na_lat_kernel, out_shape=jax.ShapeDtypeStruct(o.shape, o.dtype),
        grid=(n_lat, N_NA_PAIRS),
        in_specs=[cb(CB_NQ), cb(CB_NK), cb(CB_NV), cache, cache,
                  pl.BlockSpec((2, N_ROW_OFFS + 1, GRID_W, LANES), lambda b, hp: (hp, 0, 0, 0)),
                  pl.BlockSpec(memory_space=pl.ANY)],
        out_specs=pl.BlockSpec((DEC_SEQ, LANES), lambda b, hp: (rb0 + b, OB_NA + hp)),
        input_output_aliases={6: 0},
        compiler_params=_cparams(("parallel", "parallel")), name="na_latent",
    )(proj, proj, proj, kc, vc, bias, o)


def _rope_tables():
    t = jnp.arange(DEC_SEQ)
    d = np.arange(LANES) % HEAD_DIM
    quarter = HEAD_DIM // 4
    inv = ROPE_BASE ** (-jnp.arange(quarter, dtype=F32) / quarter)
    pos = jnp.where(jnp.asarray(d < HEAD_DIM // 2)[None, :], (t // GRID_W)[:, None], (t % GRID_W)[:, None])
    ang = pos.astype(F32) * inv[d % quarter][None, :]
    sign = jnp.asarray(np.where((d & quarter) == 0, -1.0, 1.0), F32)
    return jnp.cos(ang), jnp.sin(ang) * sign[None, :]


def _win_lat_kernel(sink_ref, q_ref, k_ref, v_ref, kc_ref, vc_ref, cos_ref, sin_ref, o_in_ref, o_ref,
                    qs_ref, kp_ref, vp_ref):
    hp = pl.program_id(1)
    quarter = HEAD_DIM // 4
    lane = lax.broadcasted_iota(jnp.int32, (DEC_SEQ, LANES), 1)
    first = (lane & quarter) == 0
    lo = lane < HEAD_DIM
    cos = cos_ref[...]
    sin = sin_ref[...]

    def rope(x):
        sw = jnp.where(first, pltpu.roll(x, LANES - quarter, 1), pltpu.roll(x, quarter, 1))
        return x * cos + sw * sin

    qs_ref[...] = rope(q_ref[...]) * SCALE
    k = rope(k_ref[...])
    v = v_ref[...]
    lo_orig = (2 * hp) // G_WIN == 0
    hi_orig = (2 * hp + 1) // G_WIN == 1
    orig = jnp.where(lo, lo_orig.astype(jnp.int32), hi_orig.astype(jnp.int32)) == 1
    zeros = jnp.zeros((WIN_BLOCK, LANES), BF16)
    kp_ref[:WIN_BLOCK] = zeros
    kp_ref[WIN_BLOCK + DEC_SEQ:] = zeros
    vp_ref[:WIN_BLOCK] = zeros
    vp_ref[WIN_BLOCK + DEC_SEQ:] = zeros
    kp_ref[WIN_BLOCK:WIN_BLOCK + DEC_SEQ] = jnp.where(orig, k, pltpu.roll(k, HEAD_DIM, 1)).astype(BF16)
    vp_ref[WIN_BLOCK:WIN_BLOCK + DEC_SEQ] = jnp.where(orig, v, pltpu.roll(v, HEAD_DIM, 1)).astype(BF16)
    kc = jnp.concatenate([jnp.where(lo_orig, kc_ref[0], kc_ref[1]),
                          jnp.where(hi_orig, kc_ref[1], kc_ref[0])], axis=1).astype(BF16)
    vc = jnp.concatenate([jnp.where(lo_orig, vc_ref[0], vc_ref[1]),
                          jnp.where(hi_orig, vc_ref[1], vc_ref[0])], axis=1).astype(BF16)
    sinks = (sink_ref[2 * hp], sink_ref[2 * hp + 1])
    n_band = WIN_Q + 2 * WIN_BLOCK
    qi = lax.broadcasted_iota(jnp.int32, (WIN_Q, n_band), 0)
    kj = lax.broadcasted_iota(jnp.int32, (WIN_Q, n_band), 1)
    near = jnp.abs(qi + WIN_BLOCK - kj) <= WIN_HALF
    for n in range(DEC_SEQ // WIN_Q):
        q0 = n * WIN_Q
        kpos = q0 - WIN_BLOCK + kj
        ok = jnp.logical_and(near, jnp.logical_and(kpos >= 0, kpos < DEC_SEQ))
        band = jnp.where(ok, 0.0, NEG)
        o = _attn_pair(qs_ref[q0:q0 + WIN_Q, :],
                       [kp_ref[q0:q0 + n_band, :], kc], [vp_ref[q0:q0 + n_band, :], vc],
                       [(band, band), None], sinks)
        o_ref[q0:q0 + WIN_Q, :] = o.astype(o_ref.dtype)


def _win_latent(proj, o, kc, vc, sink, cos, sin, l, n_lat):
    rb0 = (proj.shape[0] - n_lat * DEC_SEQ) // DEC_SEQ
    cache = pl.BlockSpec((None, None, KV_WIN, PAST_LEN, HEAD_DIM), lambda b, hp: (b, l, 0, 0, 0))
    tbl = pl.BlockSpec((DEC_SEQ, LANES), lambda b, hp: (0, 0))
    return pl.pallas_call(
        _win_lat_kernel, out_shape=jax.ShapeDtypeStruct(o.shape, o.dtype),
        grid=(n_lat, N_WIN_PAIRS),
        in_specs=[_smem_spec(),
                  pl.BlockSpec((DEC_SEQ, LANES), lambda b, hp: (rb0 + b, CB_WQ + hp)),
                  pl.BlockSpec((DEC_SEQ, LANES), lambda b, hp: (rb0 + b, CB_WK)),
                  pl.BlockSpec((DEC_SEQ, LANES), lambda b, hp: (rb0 + b, CB_WV)),
                  cache, cache, tbl, tbl, pl.BlockSpec(memory_space=pl.ANY)],
        out_specs=pl.BlockSpec((DEC_SEQ, LANES), lambda b, hp: (rb0 + b, OB_WIN + hp)),
        scratch_shapes=[pltpu.VMEM((DEC_SEQ, LANES), F32),
                        pltpu.VMEM((DEC_SEQ + 2 * WIN_BLOCK, LANES), BF16),
                        pltpu.VMEM((DEC_SEQ + 2 * WIN_BLOCK, LANES), BF16)],
        input_output_aliases={8: 0},
        compiler_params=_cparams(("parallel", "parallel")), name="win_latent",
    )(sink, proj, proj, proj, kc, vc, cos, sin, o)


def _mixers(proj, l, n_ctx, n_lat, prev_states, state_ret_fwd, state_ret_bwd, cache_na_k, cache_na_v,
            cache_win_k, cache_win_v, ret_decay_fwd, ret_decay_bwd, na_rpb, win_sink):
    lgf = jax.nn.log_sigmoid(ret_decay_fwd[l].astype(F32))
    lgb = jax.nn.log_sigmoid(ret_decay_bwd[l].astype(F32))
    sink = win_sink[l].astype(F32)
    o, *states = _ctx_mixer(proj, lgf, lgb, sink, n_ctx, prev_states)
    o = _ret_latent(proj, o, lgf, lgb, state_ret_fwd, state_ret_bwd, l, n_lat)
    o = _na_latent(proj, o, cache_na_k, cache_na_v, _na_bias_table(na_rpb[l]), l, n_lat)
    cos, sin = _rope_tables()
    o = _win_latent(proj, o, cache_win_k, cache_win_v, sink, cos, sin, l, n_lat)
    return o, tuple(states)


def kernel(x_prompt, x_sample, c, state_ret_fwd, state_ret_bwd, cache_na_k, cache_na_v, cache_win_k, cache_win_v, c_ctx, norm1_g, norm2_g, ada_w, ada_b, w_in, w_out, ret_decay_fwd, ret_decay_bwd, na_rpb, win_sink, ffn_w_in, ffn_w_out, moe_router, moe_w_in, moe_w_out, final_norm_g):
    x = jnp.concatenate([x_prompt.reshape(T_CTX, D_MODEL), x_sample.reshape(T_LAT, D_MODEL)], axis=0)
    cond = jnp.concatenate([c_ctx[None, :], c, jnp.zeros((N_GROUPS - 1 - DEC_BATCH, D_MODEL), F32)], axis=0)
    states = ()
    for l in range(DEPTH):
        mod = _adaln(cond, ada_w, ada_b, l)
        h = _norm_mod(x, norm1_g, mod, l, 0, 1)
        proj = _matmul(h, w_in, l)
        o, states = _mixers(proj, l, BATCH, DEC_BATCH, states, state_ret_fwd, state_ret_bwd, cache_na_k, cache_na_v,
                         cache_win_k, cache_win_v, ret_decay_fwd, ret_decay_bwd, na_rpb, win_sink)
        x = _matmul(o, w_out, l, resid=x, mod=mod, k_gate=2)
        i = l // 2
        if l % 2 == 0:
            h = _norm_mod(x, norm2_g, mod, l, 3, 4)
            x = _ffn_dense(h, x, mod, 5, ffn_w_in, ffn_w_out, i, D_FF, FFN_TM)
        else:
            router = jnp.pad(moe_router[i], ((0, 0), (0, LANES - N_EXPERTS)))
            h2, route, counts = _norm_mod(x, norm2_g, mod, l, 3, 4, router=router)
            x = _moe(x, h2, route, counts, moe_w_in[i], moe_w_out[i], mod, 5,
                     final_g=final_norm_g if l == DEPTH - 1 else None)
    if DEPTH % 2 == 0:
        y_prompt, y_sample = x
    else:
        y_prompt = _final_norm(x, final_norm_g, 0, T_CTX)
        y_sample = _final_norm(x, final_norm_g, T_CTX, T_LAT)
    return (y_prompt.reshape(BATCH, SEQ, D_MODEL), y_sample.reshape(DEC_BATCH, DEC_SEQ, D_MODEL), *states)
```
